```python
import math
import jax, jax.numpy as jnp
from jax import lax
import numpy as np

D_MODEL = 1024
BATCH = 4
SEQ = 4096
DEPTH = 1

N_MEM = 256
MEM_HEADS = 4
MEM_HEAD_DIM = D_MODEL // MEM_HEADS
HEAD_DIM = 64
SWA_HEADS = D_MODEL // HEAD_DIM
SWA_KV_HEADS = 4
WINDOW = 128
BLOCK = 128
REL_BUCKETS = 32
REL_MAX_DIST = 128
CONV_WIDTH = D_MODEL
CONV_K = 3
D_FF = 2816
EPS = 1e-6
NEG_INF = -1e30
POS_PAD = 1 << 30

Q_WIDTH = SWA_HEADS * HEAD_DIM
KV_WIDTH = SWA_KV_HEADS * HEAD_DIM
IN_WIDTHS = (Q_WIDTH, KV_WIDTH, KV_WIDTH, CONV_WIDTH, CONV_WIDTH, CONV_WIDTH, D_MODEL, D_MODEL)
IN_WIDTH = Q_WIDTH + 2 * KV_WIDTH + 3 * CONV_WIDTH + 2 * D_MODEL

kernel_name = "hybrid_gated_swa_shortconv_macaron"


def rms_norm(x, g):
    xf = x.astype(jnp.float32)
    y = xf * lax.rsqrt(jnp.mean(xf * xf, axis=-1, keepdims=True) + EPS)
    return (y * g.astype(jnp.float32)).astype(x.dtype)


def swiglu_ffn(h, w_gu, w_down):
    gate, up = jnp.split(h @ w_gu, 2, axis=-1)
    return (jax.nn.silu(gate) * up) @ w_down


def split_cols(t, widths):
    outs, start = [], 0
    for w in widths:
        outs.append(t[..., start:start + w])
        start += w
    return outs


def t5_causal_bucket(rel):
    n = jnp.maximum(rel, 0)
    max_exact = REL_BUCKETS // 2
    nf = jnp.maximum(n, 1).astype(jnp.float32)
    large = max_exact + (jnp.log(nf / max_exact) / math.log(REL_MAX_DIST / max_exact)
                         * (REL_BUCKETS - max_exact)).astype(jnp.int32)
    large = jnp.minimum(large, REL_BUCKETS - 1)
    return jnp.where(n < max_exact, n, large)


def with_prev_block(t, fill):
    b, s = t.shape[0], t.shape[1]
    nb = s // BLOCK
    tb = t.reshape((b, nb, BLOCK) + t.shape[2:])
    pad = jnp.full_like(tb[:, :1], fill)
    prev = jnp.concatenate([pad, tb[:, :-1]], axis=1)
    return jnp.concatenate([prev, tb], axis=2)


def sliding_window_gqa(q, k, v, positions, rel_bias, sinks):
    b, s, _ = q.shape
    nb = s // BLOCK
    grp = SWA_HEADS // SWA_KV_HEADS
    qb = q.reshape(b, nb, BLOCK, SWA_KV_HEADS, grp, HEAD_DIM)
    kb = with_prev_block(k.reshape(b, s, SWA_KV_HEADS, HEAD_DIM), 0)
    vb = with_prev_block(v.reshape(b, s, SWA_KV_HEADS, HEAD_DIM), 0)
    pq = positions.reshape(b, nb, BLOCK)
    pk = with_prev_block(positions, POS_PAD)
    rel = pq[:, :, :, None] - pk[:, :, None, :]
    visible = (rel >= 0) & (rel < WINDOW)
    bias = jnp.moveaxis(rel_bias[t5_causal_bucket(rel)], -1, 2)
    bias = bias.reshape(b, nb, SWA_KV_HEADS, grp, BLOCK, 2 * BLOCK).astype(jnp.float32)
    logits = jnp.einsum('bnqhgd,bnkhd->bnhgqk', qb, kb).astype(jnp.float32) * (HEAD_DIM ** -0.5)
    logits = jnp.where(visible[:, :, None, None], logits + bias, NEG_INF)
    sink = jnp.broadcast_to(sinks.astype(jnp.float32).reshape(1, 1, SWA_KV_HEADS, grp, 1, 1),
                            logits.shape[:-1] + (1,))
    probs = jax.nn.softmax(jnp.concatenate([logits, sink], axis=-1), axis=-1)[..., :-1]
    out = jnp.einsum('bnhgqk,bnkhd->bnqhgd', probs.astype(vb.dtype), vb)
    return out.reshape(b, s, Q_WIDTH)


def short_conv(u, w_conv):
    s = u.shape[1]
    up = jnp.pad(u, ((0, 0), (CONV_K - 1, 0), (0, 0)))
    return sum(w_conv[j] * up[:, j:j + s] for j in range(CONV_K))


def memory_cross_attention(h, mem_h, w_q, w_kv, w_o):
    b, s, _ = h.shape
    m = mem_h.shape[1]
    q = (h @ w_q).reshape(b, s, MEM_HEADS, MEM_HEAD_DIM)
    k, v = jnp.split(mem_h @ w_kv, 2, axis=-1)
    k = k.reshape(b, m, MEM_HEADS, MEM_HEAD_DIM)
    v = v.reshape(b, m, MEM_HEADS, MEM_HEAD_DIM)
    logits = jnp.einsum('bshd,bmhd->bhsm', q, k).astype(jnp.float32) * (MEM_HEAD_DIM ** -0.5)
    probs = jax.nn.softmax(logits, axis=-1)
    o = jnp.einsum('bhsm,bmhd->bshd', probs.astype(v.dtype), v).reshape(b, s, D_MODEL)
    return o @ w_o


def setup_inputs(seed: int = 0) -> dict:
    key = jax.random.key(seed)
    ks = jax.random.split(key, 24)
    f32 = jnp.float32

    def dense(k, shape, fan_in):
        return jax.random.normal(k, shape, f32) * (fan_in ** -0.5)

    def gain(k, shape):
        return 1.0 + 0.05 * jax.random.normal(k, shape, f32)

    L = DEPTH
    x = jax.random.normal(ks[0], (BATCH, SEQ, D_MODEL), f32)
    mem = jax.random.normal(ks[1], (BATCH, N_MEM, D_MODEL), f32)
    offsets = jax.random.randint(ks[2], (BATCH, 1), 0, 1024, dtype=jnp.int32)
    positions = offsets + jnp.arange(SEQ, dtype=jnp.int32)[None, :]
    return {
        "x": x,
        "mem": mem,
        "positions": positions,
        "rel_bias": 0.5 * jax.random.normal(ks[3], (REL_BUCKETS, SWA_HEADS), f32),
        "ffn1_norm": gain(ks[4], (L, D_MODEL)),
        "ffn1_w_gu": dense(ks[5], (L, D_MODEL, 2 * D_FF), D_MODEL),
        "ffn1_w_down": dense(ks[6], (L, D_FF, D_MODEL), D_FF),
        "mix_norm": gain(ks[7], (L, D_MODEL)),
        "w_in": dense(ks[8], (L, D_MODEL, IN_WIDTH), D_MODEL),
        "sinks": 0.5 * jax.random.normal(ks[9], (L, SWA_HEADS), f32),
        "conv_w": dense(ks[10], (L, CONV_K, CONV_WIDTH), CONV_K),
        "w_out": dense(ks[11], (L, D_MODEL, D_MODEL), D_MODEL),
        "xattn_norm": gain(ks[12], (L, D_MODEL)),
        "mem_norm": gain(ks[13], (L, D_MODEL)),
        "xattn_wq": dense(ks[14], (L, D_MODEL, D_MODEL), D_MODEL),
        "xattn_wkv": dense(ks[15], (L, D_MODEL, 2 * D_MODEL), D_MODEL),
        "xattn_wo": dense(ks[16], (L, D_MODEL, D_MODEL), D_MODEL),
        "ffn2_norm": gain(ks[17], (L, D_MODEL)),
        "ffn2_w_gu": dense(ks[18], (L, D_MODEL, 2 * D_FF), D_MODEL),
        "ffn2_w_down": dense(ks[19], (L, D_FF, D_MODEL), D_FF),
        "final_norm": gain(ks[20], (D_MODEL,)),
    }


def reference(x, mem, positions, rel_bias, ffn1_norm, ffn1_w_gu, ffn1_w_down, mix_norm, w_in,
              sinks, conv_w, w_out, xattn_norm, mem_norm, xattn_wq, xattn_wkv, xattn_wo,
              ffn2_norm, ffn2_w_gu, ffn2_w_down, final_norm):
    for l in range(DEPTH):
        x = x + 0.5 * swiglu_ffn(rms_norm(x, ffn1_norm[l]), ffn1_w_gu[l], ffn1_w_down[l])
        h = rms_norm(x, mix_norm[l])
        q, k, v, c_pre, b_post, u, g_attn, g_conv = split_cols(h @ w_in[l], IN_WIDTHS)
        attn = sliding_window_gqa(q, k, v, positions, rel_bias, sinks[l])
        conv = b_post * short_conv(c_pre * u, conv_w[l])
        merged = jax.nn.sigmoid(g_attn) * attn + jax.nn.sigmoid(g_conv) * conv
        x = x + merged @ w_out[l]
        x = x + memory_cross_attention(rms_norm(x, xattn_norm[l]), rms_norm(mem, mem_norm[l]),
                                       xattn_wq[l], xattn_wkv[l], xattn_wo[l])
        x = x + 0.5 * swiglu_ffn(rms_norm(x, ffn2_norm[l]), ffn2_w_gu[l], ffn2_w_down[l])
    return rms_norm(x, final_norm)
```

```python
import functools
import math

import jax
import jax.numpy as jnp
from jax import lax
from jax.experimental import pallas as pl
from jax.experimental.pallas import tpu as pltpu

D_MODEL = 1024
D_FF = 2816
N_MEM = 256
MEM_HEADS = 4
MEM_HEAD_DIM = D_MODEL // MEM_HEADS
HEAD_DIM = 64
SWA_HEADS = 16
SWA_KV_HEADS = 4
WINDOW = 128
BLOCK = 128
REL_BUCKETS = 32
REL_MAX_DIST = 128
CONV_K = 3
EPS = 1e-6
NEG_INF = -1e30
POS_PAD = 1 << 30

Q_WIDTH = SWA_HEADS * HEAD_DIM
KV_WIDTH = SWA_KV_HEADS * HEAD_DIM
OFF_Q = 0
OFF_K = OFF_Q + Q_WIDTH
OFF_V = OFF_K + KV_WIDTH
OFF_C = OFF_V + KV_WIDTH
OFF_B = OFF_C + D_MODEL
OFF_U = OFF_B + D_MODEL
OFF_GA = OFF_U + D_MODEL
OFF_GC = OFF_GA + D_MODEL
IN_WIDTH = OFF_GC + D_MODEL

LANES = 128
SUBLANES = 8
MXU_COLS = 256
VMEM_LIMIT_BYTES = 56 * 1024 * 1024

TOKEN_TILE = 512
FF_CHUNK = MXU_COLS
PAIR = 2 * HEAD_DIM

BF16 = jnp.bfloat16
F32 = jnp.float32


def _rms(x, g):
    return x * lax.rsqrt(jnp.mean(x * x, axis=-1, keepdims=True) + EPS) * g


def _dot(a, b):
    return jnp.dot(a, b, preferred_element_type=F32)


def _dot_nt(a, b):
    return lax.dot_general(a, b, (((1,), (1,)), ((), ())), preferred_element_type=F32)


def _resident(shape):
    return pl.BlockSpec(shape, lambda i: (0,) * len(shape), pipeline_mode=pl.Buffered(1))


def _params():
    return pltpu.CompilerParams(dimension_semantics=("arbitrary",), vmem_limit_bytes=VMEM_LIMIT_BYTES)


def _ffn_kernel(x_ref, g_ref, wgu_ref, wd_ref, fg_ref, o_ref, act_ref, *, final_norm):
    x = x_ref[...]
    h = _rms(x, g_ref[...]).astype(BF16)
    for j in range(D_FF // FF_CHUNK):
        lo = j * FF_CHUNK
        gate = _dot(h, wgu_ref[:, lo:lo + FF_CHUNK])
        up = _dot(h, wgu_ref[:, D_FF + lo:D_FF + lo + FF_CHUNK])
        act_ref[:, lo:lo + FF_CHUNK] = (gate * jax.nn.sigmoid(gate) * up).astype(BF16)
    y = x + 0.5 * _dot(act_ref[...], wd_ref[...])
    if final_norm:
        y = _rms(y, fg_ref[...])
    o_ref[...] = y


def _ffn(x, gain, w_gu, w_down, final_gain, *, final_norm):
    t = x.shape[0]
    tm = TOKEN_TILE
    return pl.pallas_call(
        functools.partial(_ffn_kernel, final_norm=final_norm),
        grid=(t // tm,),
        in_specs=[
            pl.BlockSpec((tm, D_MODEL), lambda i: (i, 0)),
            _resident((1, D_MODEL)),
            _resident((D_MODEL, 2 * D_FF)),
            _resident((D_FF, D_MODEL)),
            _resident((1, D_MODEL)),
        ],
        out_specs=pl.BlockSpec((tm, D_MODEL), lambda i: (i, 0)),
        out_shape=jax.ShapeDtypeStruct((t, D_MODEL), F32),
        scratch_shapes=[pltpu.VMEM((tm, D_FF), BF16)],
        compiler_params=_params(),
        name="ffn_final" if final_norm else "ffn",
    )(x, gain, w_gu, w_down, final_gain)


def _bias_lut_kernel(tab_ref, lut_ref):
    n = lax.broadcasted_iota(jnp.int32, (SWA_HEADS, WINDOW), 1)
    max_exact = REL_BUCKETS // 2
    nf = jnp.maximum(n, 1).astype(F32)
    large = max_exact + (jnp.log(nf / max_exact) / math.log(REL_MAX_DIST / max_exact)
                         * (REL_BUCKETS - max_exact)).astype(jnp.int32)
    large = jnp.minimum(large, REL_BUCKETS - 1)
    bucket = jnp.where(n < max_exact, n, large)
    lut = jnp.zeros((SWA_HEADS, WINDOW), F32)
    for b in range(REL_BUCKETS):
        lut = jnp.where(bucket == b, tab_ref[:, b:b + 1], lut)
    lut_ref[...] = lut


def _bias_lut(rel_bias_t):
    return pl.pallas_call(
        _bias_lut_kernel,
        out_shape=jax.ShapeDtypeStruct((SWA_HEADS, WINDOW), F32),
        name="bias_lut",
    )(rel_bias_t)


def _split_kv_heads(t):
    lane = lax.broadcasted_iota(jnp.int32, (t.shape[0], PAIR), 1)
    low_half = lane < HEAD_DIM
    los, his = [], []
    for p in range(SWA_KV_HEADS // 2):
        pair = t[:, p * PAIR:(p + 1) * PAIR]
        swapped = pltpu.roll(pair, HEAD_DIM, axis=1)
        zero = jnp.zeros_like(pair)
        los += [jnp.where(low_half, pair, zero), jnp.where(low_half, swapped, zero)]
        his += [jnp.where(low_half, zero, swapped), jnp.where(low_half, zero, pair)]
    return [a.astype(BF16) for a in los], [a.astype(BF16) for a in his]


def _mix_kernel(x_ref, posq_ref, posk_ref, lut_ref, sink_ref, g_ref, win_ref, convw_ref, wout_ref, o_ref,
                q_s, klo_s, khi_s, vlo_s, vhi_s, cu_s, attn_s, *, tiles_per_seq):
    tm = x_ref.shape[0]
    first = (pl.program_id(0) % tiles_per_seq) == 0

    @pl.when(first)
    def _():
        for s in (klo_s, khi_s, vlo_s, vhi_s):
            s[:, 0:BLOCK, :] = jnp.zeros((SWA_KV_HEADS, BLOCK, PAIR), BF16)
        cu_s[0:SUBLANES, :] = jnp.zeros((SUBLANES, D_MODEL), F32)

    @pl.when(jnp.logical_not(first))
    def _():
        for s in (klo_s, khi_s, vlo_s, vhi_s):
            s[:, 0:BLOCK, :] = s[:, tm:tm + BLOCK, :]
        cu_s[0:SUBLANES, :] = cu_s[tm:tm + SUBLANES, :]

    x = x_ref[...]
    h = _rms(x, g_ref[...]).astype(BF16)

    q_s[...] = (_dot(h, win_ref[:, OFF_Q:OFF_Q + Q_WIDTH]) * (HEAD_DIM ** -0.5)).astype(BF16)
    k_lo, k_hi = _split_kv_heads(_dot(h, win_ref[:, OFF_K:OFF_K + KV_WIDTH]))
    v_lo, v_hi = _split_kv_heads(_dot(h, win_ref[:, OFF_V:OFF_V + KV_WIDTH]))
    for g in range(SWA_KV_HEADS):
        klo_s[g, BLOCK:BLOCK + tm, :] = k_lo[g]
        khi_s[g, BLOCK:BLOCK + tm, :] = k_hi[g]
        vlo_s[g, BLOCK:BLOCK + tm, :] = v_lo[g]
        vhi_s[g, BLOCK:BLOCK + tm, :] = v_hi[g]

    def attend_block(blk, carry):
        r0 = pl.multiple_of(blk * BLOCK, BLOCK)
        rel = posq_ref[pl.ds(r0, BLOCK), :] - posk_ref[0, pl.ds(blk, 1), :]
        visible = (rel >= 0) & (rel < WINDOW)
        rel_c = jnp.clip(rel, 0, WINDOW - 1)
        for g in range(SWA_KV_HEADS):
            c0 = g * 2 * PAIR
            q2 = jnp.concatenate([q_s[pl.ds(r0, BLOCK), c0:c0 + PAIR],
                                  q_s[pl.ds(r0, BLOCK), c0 + PAIR:c0 + 2 * PAIR]], axis=0)
            keys = pl.ds(r0, 2 * BLOCK)
            logits = (_dot_nt(q2, klo_s[g, keys, :]), _dot_nt(q2, khi_s[g, keys, :]))
            probs = [[None, None], [None, None]]
            for half in range(2):
                for pair in range(2):
                    head = g * 4 + pair * 2 + half
                    lg = logits[half][pair * BLOCK:(pair + 1) * BLOCK, :]
                    lut = jnp.broadcast_to(lut_ref[head:head + 1, :], (BLOCK, WINDOW))
                    bias = jnp.concatenate(
                        [jnp.take_along_axis(lut, rel_c[:, 0:BLOCK], axis=1),
                         jnp.take_along_axis(lut, rel_c[:, BLOCK:2 * BLOCK], axis=1)], axis=1)
                    lg = jnp.where(visible, lg + bias, NEG_INF)
                    sink = sink_ref[0:1, head:head + 1]
                    m = jnp.maximum(jnp.max(lg, axis=1, keepdims=True), sink)
                    e = jnp.exp(lg - m)
                    denom = jnp.sum(e, axis=1, keepdims=True) + jnp.exp(sink - m)
                    probs[half][pair] = (e / denom).astype(BF16)
            p_lo = jnp.concatenate(probs[0], axis=0)
            p_hi = jnp.concatenate(probs[1], axis=0)
            out = _dot(p_lo, vlo_s[g, keys, :]) + _dot(p_hi, vhi_s[g, keys, :])
            attn_s[pl.ds(r0, BLOCK), c0:c0 + PAIR] = out[0:BLOCK]
            attn_s[pl.ds(r0, BLOCK), c0 + PAIR:c0 + 2 * PAIR] = out[BLOCK:2 * BLOCK]
        return carry

    lax.fori_loop(0, tm // BLOCK, attend_block, 0)

    merged = jax.nn.sigmoid(_dot(h, win_ref[:, OFF_GA:OFF_GA + D_MODEL])) * attn_s[...]

    cu = _dot(h, win_ref[:, OFF_C:OFF_C + D_MODEL]) * _dot(h, win_ref[:, OFF_U:OFF_U + D_MODEL])
    cu_s[SUBLANES:SUBLANES + tm, :] = cu
    conv = (convw_ref[0:1, :] * cu_s[SUBLANES - 2:SUBLANES - 2 + tm, :]
            + convw_ref[1:2, :] * cu_s[SUBLANES - 1:SUBLANES - 1 + tm, :]
            + convw_ref[2:3, :] * cu)
    conv = _dot(h, win_ref[:, OFF_B:OFF_B + D_MODEL]) * conv
    merged = merged + jax.nn.sigmoid(_dot(h, win_ref[:, OFF_GC:OFF_GC + D_MODEL])) * conv

    o_ref[...] = x + _dot(merged.astype(BF16), wout_ref[...])


def _mix(x, posq, posk, lut, sinks, gain, w_in, conv_w, w_out, *, seq_len):
    t = x.shape[0]
    tm = TOKEN_TILE
    nblk = tm // BLOCK
    kv_scratch = pltpu.VMEM((SWA_KV_HEADS, BLOCK + tm, PAIR), BF16)
    return pl.pallas_call(
        functools.partial(_mix_kernel, tiles_per_seq=seq_len // tm),
        grid=(t // tm,),
        in_specs=[
            pl.BlockSpec((tm, D_MODEL), lambda i: (i, 0)),
            pl.BlockSpec((tm, 1), lambda i: (i, 0)),
            pl.BlockSpec((1, nblk, 2 * BLOCK), lambda i: (i, 0, 0)),
            _resident((SWA_HEADS, WINDOW)),
            _resident((1, SWA_HEADS)),
            _resident((1, D_MODEL)),
            _resident((D_MODEL, IN_WIDTH)),
            _resident((CONV_K, D_MODEL)),
            _resident((D_MODEL, D_MODEL)),
        ],
        out_specs=pl.BlockSpec((tm, D_MODEL), lambda i: (i, 0)),
        out_shape=jax.ShapeDtypeStruct((t, D_MODEL), F32),
        scratch_shapes=[
            pltpu.VMEM((tm, Q_WIDTH), BF16),
            kv_scratch, kv_scratch, kv_scratch, kv_scratch,
            pltpu.VMEM((SUBLANES + tm, D_MODEL), F32),
            pltpu.VMEM((tm, D_MODEL), F32),
        ],
        compiler_params=_params(),
        name="mix",
    )(x, posq, posk, lut, sinks, gain, w_in, conv_w, w_out)


def _mem_kv_kernel(mem_ref, g_ref, wkv_ref, kv_ref):
    h = _rms(mem_ref[...], g_ref[...]).astype(BF16)
    kv_ref[...] = _dot(h, wkv_ref[...]).astype(BF16)


def _mem_kv(mem, gain, w_kv):
    rows = mem.shape[0]
    tm = TOKEN_TILE
    return pl.pallas_call(
        _mem_kv_kernel,
        grid=(rows // tm,),
        in_specs=[
            pl.BlockSpec((tm, D_MODEL), lambda i: (i, 0)),
            _resident((1, D_MODEL)),
            _resident((D_MODEL, 2 * D_MODEL)),
        ],
        out_specs=pl.BlockSpec((tm, 2 * D_MODEL), lambda i: (i, 0)),
        out_shape=jax.ShapeDtypeStruct((rows, 2 * D_MODEL), BF16),
        compiler_params=_params(),
        name="mem_kv",
    )(mem, gain, w_kv)


def _xattn_kernel(x_ref, g_ref, wq_ref, kv_ref, wo_ref, o_ref, ctx_s):
    x = x_ref[...]
    h = _rms(x, g_ref[...]).astype(BF16)
    q = (_dot(h, wq_ref[...]) * (MEM_HEAD_DIM ** -0.5)).astype(BF16)
    for hd in range(MEM_HEADS):
        c0 = hd * MEM_HEAD_DIM
        lg = _dot_nt(q[:, c0:c0 + MEM_HEAD_DIM], kv_ref[0, :, c0:c0 + MEM_HEAD_DIM])
        e = jnp.exp(lg - jnp.max(lg, axis=1, keepdims=True))
        p = (e / jnp.sum(e, axis=1, keepdims=True)).astype(BF16)
        ctx_s[:, c0:c0 + MEM_HEAD_DIM] = _dot(
            p, kv_ref[0, :, D_MODEL + c0:D_MODEL + c0 + MEM_HEAD_DIM]).astype(BF16)
    o_ref[...] = x + _dot(ctx_s[...], wo_ref[...])


def _xattn(x, gain, w_q, kv, w_o, *, seq_len):
    t = x.shape[0]
    tm = TOKEN_TILE
    tiles_per_seq = seq_len // tm
    return pl.pallas_call(
        _xattn_kernel,
        grid=(t // tm,),
        in_specs=[
            pl.BlockSpec((tm, D_MODEL), lambda i: (i, 0)),
            _resident((1, D_MODEL)),
            _resident((D_MODEL, D_MODEL)),
            pl.BlockSpec((1, N_MEM, 2 * D_MODEL), lambda i: (i // tiles_per_seq, 0, 0)),
            _resident((D_MODEL, D_MODEL)),
        ],
        out_specs=pl.BlockSpec((tm, D_MODEL), lambda i: (i, 0)),
        out_shape=jax.ShapeDtypeStruct((t, D_MODEL), F32),
        scratch_shapes=[pltpu.VMEM((tm, D_MODEL), BF16)],
        compiler_params=_params(),
        name="xattn",
    )(x, gain, w_q, kv, w_o)


def kernel(x, mem, positions, rel_bias, ffn1_norm, ffn1_w_gu, ffn1_w_down, mix_norm, w_in, sinks, conv_w, w_out, xattn_norm, mem_norm, xattn_wq, xattn_wkv, xattn_wo, ffn2_norm, ffn2_w_gu, ffn2_w_down, final_norm):
    batch, seq_len, _ = x.shape
    depth = w_in.shape[0]
    tokens = batch * seq_len
    nb = seq_len // BLOCK

    xt = x.reshape(tokens, D_MODEL)
    posq = positions.reshape(tokens, 1)
    pos_blocks = positions.reshape(batch, nb, BLOCK)
    prev = jnp.concatenate([jnp.full((batch, 1, BLOCK), POS_PAD, positions.dtype), pos_blocks[:, :-1]], axis=1)
    posk = jnp.concatenate([prev, pos_blocks], axis=2).reshape(tokens // TOKEN_TILE, TOKEN_TILE // BLOCK, 2 * BLOCK)
    lut = _bias_lut(rel_bias.T)
    final_gain = final_norm.reshape(1, D_MODEL)

    for l in range(depth):
        xt = _ffn(xt, ffn1_norm[l].reshape(1, D_MODEL), ffn1_w_gu[l].astype(BF16), ffn1_w_down[l].astype(BF16),
                  final_gain, final_norm=False)
        xt = _mix(xt, posq, posk, lut, sinks[l].reshape(1, SWA_HEADS), mix_norm[l].reshape(1, D_MODEL),
                  w_in[l].astype(BF16), conv_w[l], w_out[l].astype(BF16), seq_len=seq_len)
        kv = _mem_kv(mem.reshape(batch * N_MEM, D_MODEL), mem_norm[l].reshape(1, D_MODEL), xattn_wkv[l].astype(BF16))
        xt = _xattn(xt, xattn_norm[l].reshape(1, D_MODEL), xattn_wq[l].astype(BF16),
                    kv.reshape(batch, N_MEM, 2 * D_MODEL), xattn_wo[l].astype(BF16), seq_len=seq_len)
        xt = _ffn(xt, ffn2_norm[l].reshape(1, D_MODEL), ffn2_w_gu[l].astype(BF16), ffn2_w_down[l].astype(BF16),
                  final_gain, final_norm=(l == depth - 1))
    return xt.reshape(batch, seq_len, D_MODEL)
```

```python
import functools
import math

import jax
import jax.numpy as jnp
from jax import lax
from jax.experimental import pallas as pl
from jax.experimental.pallas import tpu as pltpu

D_MODEL = 1024
D_FF = 2816
N_MEM = 256
MEM_HEADS = 4
MEM_HEAD_DIM = D_MODEL // MEM_HEADS
HEAD_DIM = 64
SWA_HEADS = 16
SWA_KV_HEADS = 4
WINDOW = 128
BLOCK = 128
REL_BUCKETS = 32
REL_MAX_DIST = 128
CONV_K = 3
EPS = 1e-6
NEG_INF = -1e30
POS_PAD = 1 << 30

Q_WIDTH = SWA_HEADS * HEAD_DIM
KV_WIDTH = SWA_KV_HEADS * HEAD_DIM
OFF_Q = 0
OFF_K = OFF_Q + Q_WIDTH
OFF_V = OFF_K + KV_WIDTH
OFF_C = OFF_V + KV_WIDTH
OFF_B = OFF_C + D_MODEL
OFF_U = OFF_B + D_MODEL
OFF_GA = OFF_U + D_MODEL
OFF_GC = OFF_GA + D_MODEL
IN_WIDTH = OFF_GC + D_MODEL

LANES = 128
SUBLANES = 8
MXU_COLS = 256
VMEM_LIMIT_BYTES = 56 * 1024 * 1024

TOKEN_TILE = 512
FF_CHUNK = MXU_COLS
PAIR = 2 * HEAD_DIM

BF16 = jnp.bfloat16
F32 = jnp.float32


def _rms(x, g):
    return x * lax.rsqrt(jnp.mean(x * x, axis=-1, keepdims=True) + EPS) * g


def _dot(a, b):
    return jnp.dot(a, b, preferred_element_type=F32)


def _dot_nt(a, b):
    return lax.dot_general(a, b, (((1,), (1,)), ((), ())), preferred_element_type=F32)


def _resident(shape):
    return pl.BlockSpec(shape, lambda i: (0,) * len(shape), pipeline_mode=pl.Buffered(1))


def _params():
    return pltpu.CompilerParams(dimension_semantics=("arbitrary",), vmem_limit_bytes=VMEM_LIMIT_BYTES)


def _ffn_kernel(x_ref, g_ref, wgu_ref, wd_ref, fg_ref, o_ref, act_ref, *, final_norm):
    x = x_ref[...]
    h = _rms(x, g_ref[...]).astype(BF16)
    for j in range(D_FF // FF_CHUNK):
        lo = j * FF_CHUNK
        gate = _dot(h, wgu_ref[:, lo:lo + FF_CHUNK])
        up = _dot(h, wgu_ref[:, D_FF + lo:D_FF + lo + FF_CHUNK])
        act_ref[:, lo:lo + FF_CHUNK] = (gate * jax.nn.sigmoid(gate) * up).astype(BF16)
    y = x + 0.5 * _dot(act_ref[...], wd_ref[...])
    if final_norm:
        y = _rms(y, fg_ref[...])
    o_ref[...] = y


def _ffn(x, gain, w_gu, w_down, final_gain, *, final_norm):
    t = x.shape[0]
    tm = TOKEN_TILE
    return pl.pallas_call(
        functools.partial(_ffn_kernel, final_norm=final_norm),
        grid=(t // tm,),
        in_specs=[
            pl.BlockSpec((tm, D_MODEL), lambda i: (i, 0)),
            _resident((1, D_MODEL)),
            _resident((D_MODEL, 2 * D_FF)),
            _resident((D_FF, D_MODEL)),
            _resident((1, D_MODEL)),
        ],
        out_specs=pl.BlockSpec((tm, D_MODEL), lambda i: (i, 0)),
        out_shape=jax.ShapeDtypeStruct((t, D_MODEL), F32),
        scratch_shapes=[pltpu.VMEM((tm, D_FF), BF16)],
        compiler_params=_params(),
        name="ffn_final" if final_norm else "ffn",
    )(x, gain, w_gu, w_down, final_gain)


def _bias_lut_kernel(tab_ref, lut_ref):
    n = lax.broadcasted_iota(jnp.int32, (SWA_HEADS, WINDOW), 1)
    max_exact = REL_BUCKETS // 2
    nf = jnp.maximum(n, 1).astype(F32)
    large = max_exact + (jnp.log(nf / max_exact) / math.log(REL_MAX_DIST / max_exact)
                         * (REL_BUCKETS - max_exact)).astype(jnp.int32)
    large = jnp.minimum(large, REL_BUCKETS - 1)
    bucket = jnp.where(n < max_exact, n, large)
    lut = jnp.zeros((SWA_HEADS, WINDOW), F32)
    for b in range(REL_BUCKETS):
        lut = jnp.where(bucket == b, tab_ref[:, b:b + 1], lut)
    lut_ref[...] = lut


def _bias_lut(rel_bias_t):
    return pl.pallas_call(
        _bias_lut_kernel,
        out_shape=jax.ShapeDtypeStruct((SWA_HEADS, WINDOW), F32),
        name="bias_lut",
    )(rel_bias_t)


def _key_head_slabs(t):
    lane = lax.broadcasted_iota(jnp.int32, (t.shape[0], PAIR), 1)
    low_half = lane < HEAD_DIM
    los, his = [], []
    for p in range(SWA_KV_HEADS // 2):
        pair = t[:, p * PAIR:(p + 1) * PAIR]
        swapped = pltpu.roll(pair, HEAD_DIM, axis=1)
        zero = jnp.zeros_like(pair)
        los += [jnp.where(low_half, pair, zero), jnp.where(low_half, swapped, zero)]
        his += [jnp.where(low_half, zero, swapped), jnp.where(low_half, zero, pair)]
    return [a.astype(BF16) for a in los], [a.astype(BF16) for a in his]


def _value_head_slabs_t(t):
    los, his = [], []
    for p in range(SWA_KV_HEADS // 2):
        pair_t = t[:, p * PAIR:(p + 1) * PAIR].T
        zero = jnp.zeros((HEAD_DIM, t.shape[0]), F32)
        for head_t in (pair_t[0:HEAD_DIM], pair_t[HEAD_DIM:PAIR]):
            los.append(jnp.concatenate([head_t, zero], axis=0))
            his.append(jnp.concatenate([zero, head_t], axis=0))
    return [a.astype(BF16) for a in los], [a.astype(BF16) for a in his]


def _mix_kernel(x_ref, posq_ref, posk_ref, lut_ref, sink_ref, g_ref, win_ref, convw_ref, wout_ref, o_ref,
                q_s, klo_s, khi_s, vtlo_s, vthi_s, cu_s, attn_s, *, tiles_per_seq):
    tm = x_ref.shape[0]
    nblk = tm // BLOCK
    first = (pl.program_id(0) % tiles_per_seq) == 0

    @pl.when(first)
    def _():
        for s in (klo_s, khi_s):
            s[:, 0:BLOCK, :] = jnp.zeros((SWA_KV_HEADS, BLOCK, PAIR), BF16)
        for s in (vtlo_s, vthi_s):
            s[:, 0] = jnp.zeros((SWA_KV_HEADS, PAIR, BLOCK), BF16)
        cu_s[0:SUBLANES, :] = jnp.zeros((SUBLANES, D_MODEL), F32)

    @pl.when(jnp.logical_not(first))
    def _():
        for s in (klo_s, khi_s):
            s[:, 0:BLOCK, :] = s[:, tm:tm + BLOCK, :]
        for s in (vtlo_s, vthi_s):
            s[:, 0] = s[:, nblk]
        cu_s[0:SUBLANES, :] = cu_s[tm:tm + SUBLANES, :]

    x = x_ref[...]
    h = _rms(x, g_ref[...]).astype(BF16)

    q_s[...] = (_dot(h, win_ref[:, OFF_Q:OFF_Q + Q_WIDTH]) * (HEAD_DIM ** -0.5)).astype(BF16)
    k_lo, k_hi = _key_head_slabs(_dot(h, win_ref[:, OFF_K:OFF_K + KV_WIDTH]))
    vt_lo, vt_hi = _value_head_slabs_t(_dot(h, win_ref[:, OFF_V:OFF_V + KV_WIDTH]))
    for g in range(SWA_KV_HEADS):
        klo_s[g, BLOCK:BLOCK + tm, :] = k_lo[g]
        khi_s[g, BLOCK:BLOCK + tm, :] = k_hi[g]
        for j in range(nblk):
            vtlo_s[g, j + 1] = vt_lo[g][:, j * BLOCK:(j + 1) * BLOCK]
            vthi_s[g, j + 1] = vt_hi[g][:, j * BLOCK:(j + 1) * BLOCK]

    def attend_block(blk, carry):
        r0 = pl.multiple_of(blk * BLOCK, BLOCK)
        rows = pl.ds(r0, BLOCK)
        keys = pl.ds(r0, 2 * BLOCK)
        pos_k = posk_ref[pl.ds(pl.multiple_of(2 * r0, 2 * BLOCK), 2 * BLOCK), :]
        rel = posq_ref[0, pl.ds(blk, 1), :] - pos_k
        visible = (rel >= 0) & (rel < WINDOW)
        rel_c = jnp.clip(rel, 0, WINDOW - 1)
        luts = [jnp.broadcast_to(lut_ref[hd:hd + 1, :], (SUBLANES, WINDOW)) for hd in range(SWA_HEADS)]
        pieces = [[] for _ in range(SWA_HEADS)]
        for kb in range(2 * BLOCK // SUBLANES):
            idx = rel_c[kb * SUBLANES:(kb + 1) * SUBLANES, :]
            for hd in range(SWA_HEADS):
                pieces[hd].append(jnp.take_along_axis(luts[hd], idx, axis=1))
        bias = [jnp.concatenate(p, axis=0) for p in pieces]

        for g in range(SWA_KV_HEADS):
            c0 = g * 2 * PAIR
            q2 = jnp.concatenate([q_s[rows, c0:c0 + PAIR], q_s[rows, c0 + PAIR:c0 + 2 * PAIR]], axis=0)
            logits_t = (_dot_nt(klo_s[g, keys, :], q2), _dot_nt(khi_s[g, keys, :], q2))
            probs_t = [[None, None], [None, None]]
            inv = [[None, None], [None, None]]
            for half in range(2):
                for pair in range(2):
                    head = g * 4 + pair * 2 + half
                    lt = logits_t[half][:, pair * BLOCK:(pair + 1) * BLOCK]
                    lt = jnp.where(visible, lt + bias[head], NEG_INF)
                    sink = sink_ref[head]
                    m = jnp.maximum(jnp.max(lt, axis=0, keepdims=True), sink)
                    e = jnp.exp(lt - m)
                    inv[half][pair] = 1.0 / (jnp.sum(e, axis=0, keepdims=True) + jnp.exp(sink - m))
                    probs_t[half][pair] = e.astype(BF16)
            v_lo_t = jnp.concatenate([vtlo_s[g, blk], vtlo_s[g, blk + 1]], axis=1)
            v_hi_t = jnp.concatenate([vthi_s[g, blk], vthi_s[g, blk + 1]], axis=1)
            out_t = (_dot(v_lo_t, jnp.concatenate(probs_t[0], axis=1))
                     + _dot(v_hi_t, jnp.concatenate(probs_t[1], axis=1)))
            scale = jnp.concatenate(
                [jnp.broadcast_to(jnp.concatenate(inv[0], axis=1), (HEAD_DIM, 2 * BLOCK)),
                 jnp.broadcast_to(jnp.concatenate(inv[1], axis=1), (HEAD_DIM, 2 * BLOCK))], axis=0)
            out = (out_t * scale).T
            attn_s[rows, c0:c0 + PAIR] = out[0:BLOCK]
            attn_s[rows, c0 + PAIR:c0 + 2 * PAIR] = out[BLOCK:2 * BLOCK]
        return carry

    lax.fori_loop(0, nblk, attend_block, 0)

    merged = jax.nn.sigmoid(_dot(h, win_ref[:, OFF_GA:OFF_GA + D_MODEL])) * attn_s[...]

    cu = _dot(h, win_ref[:, OFF_C:OFF_C + D_MODEL]) * _dot(h, win_ref[:, OFF_U:OFF_U + D_MODEL])
    cu_s[SUBLANES:SUBLANES + tm, :] = cu
    conv = (convw_ref[0:1, :] * cu_s[SUBLANES - 2:SUBLANES - 2 + tm, :]
            + convw_ref[1:2, :] * cu_s[SUBLANES - 1:SUBLANES - 1 + tm, :]
            + convw_ref[2:3, :] * cu)
    conv = _dot(h, win_ref[:, OFF_B:OFF_B + D_MODEL]) * conv
    merged = merged + jax.nn.sigmoid(_dot(h, win_ref[:, OFF_GC:OFF_GC + D_MODEL])) * conv

    o_ref[...] = x + _dot(merged.astype(BF16), wout_ref[...])


def _mix(x, posq, posk, lut, sinks, gain, w_in, conv_w, w_out, *, seq_len):
    t = x.shape[0]
    tm = TOKEN_TILE
    nblk = tm // BLOCK
    k_scratch = pltpu.VMEM((SWA_KV_HEADS, BLOCK + tm, PAIR), BF16)
    vt_scratch = pltpu.VMEM((SWA_KV_HEADS, nblk + 1, PAIR, BLOCK), BF16)
    return pl.pallas_call(
        functools.partial(_mix_kernel, tiles_per_seq=seq_len // tm),
        grid=(t // tm,),
        in_specs=[
            pl.BlockSpec((tm, D_MODEL), lambda i: (i, 0)),
            pl.BlockSpec((1, nblk, BLOCK), lambda i: (i, 0, 0)),
            pl.BlockSpec((2 * tm, 1), lambda i: (i, 0)),
            _resident((SWA_HEADS, WINDOW)),
            pl.BlockSpec(memory_space=pltpu.SMEM),
            _resident((1, D_MODEL)),
            _resident((D_MODEL, IN_WIDTH)),
            _resident((CONV_K, D_MODEL)),
            _resident((D_MODEL, D_MODEL)),
        ],
        out_specs=pl.BlockSpec((tm, D_MODEL), lambda i: (i, 0)),
        out_shape=jax.ShapeDtypeStruct((t, D_MODEL), F32),
        scratch_shapes=[
            pltpu.VMEM((tm, Q_WIDTH), BF16),
            k_scratch, k_scratch, vt_scratch, vt_scratch,
            pltpu.VMEM((SUBLANES + tm, D_MODEL), F32),
            pltpu.VMEM((tm, D_MODEL), F32),
        ],
        compiler_params=_params(),
        name="mix",
    )(x, posq, posk, lut, sinks, gain, w_in, conv_w, w_out)


def _mem_kv_kernel(mem_ref, g_ref, wkv_ref, kv_ref):
    h = _rms(mem_ref[...], g_ref[...]).astype(BF16)
    kv_ref[...] = _dot(h, wkv_ref[...]).astype(BF16)


def _mem_kv(mem, gain, w_kv):
    rows = mem.shape[0]
    tm = TOKEN_TILE
    return pl.pallas_call(
        _mem_kv_kernel,
        grid=(rows // tm,),
        in_specs=[
            pl.BlockSpec((tm, D_MODEL), lambda i: (i, 0)),
            _resident((1, D_MODEL)),
            _resident((D_MODEL, 2 * D_MODEL)),
        ],
        out_specs=pl.BlockSpec((tm, 2 * D_MODEL), lambda i: (i, 0)),
        out_shape=jax.ShapeDtypeStruct((rows, 2 * D_MODEL), BF16),
        compiler_params=_params(),
        name="mem_kv",
    )(mem, gain, w_kv)


def _xattn_kernel(x_ref, g_ref, wq_ref, kv_ref, wo_ref, o_ref, ctx_s):
    x = x_ref[...]
    h = _rms(x, g_ref[...]).astype(BF16)
    q = (_dot(h, wq_ref[...]) * (MEM_HEAD_DIM ** -0.5)).astype(BF16)
    for hd in range(MEM_HEADS):
        c0 = hd * MEM_HEAD_DIM
        lg = _dot_nt(q[:, c0:c0 + MEM_HEAD_DIM], kv_ref[0, :, c0:c0 + MEM_HEAD_DIM])
        e = jnp.exp(lg - jnp.max(lg, axis=1, keepdims=True))
        p = (e / jnp.sum(e, axis=1, keepdims=True)).astype(BF16)
        ctx_s[:, c0:c0 + MEM_HEAD_DIM] = _dot(
            p, kv_ref[0, :, D_MODEL + c0:D_MODEL + c0 + MEM_HEAD_DIM]).astype(BF16)
    o_ref[...] = x + _dot(ctx_s[...], wo_ref[...])


def _xattn(x, gain, w_q, kv, w_o, *, seq_len):
    t = x.shape[0]
    tm = TOKEN_TILE
    tiles_per_seq = seq_len // tm
    return pl.pallas_call(
        _xattn_kernel,
        grid=(t // tm,),
        in_specs=[
            pl.BlockSpec((tm, D_MODEL), lambda i: (i, 0)),
            _resident((1, D_MODEL)),
            _resident((D_MODEL, D_MODEL)),
            pl.BlockSpec((1, N_MEM, 2 * D_MODEL), lambda i: (i // tiles_per_seq, 0, 0)),
            _resident((D_MODEL, D_MODEL)),
        ],
        out_specs=pl.BlockSpec((tm, D_MODEL), lambda i: (i, 0)),
        out_shape=jax.ShapeDtypeStruct((t, D_MODEL), F32),
        scratch_shapes=[pltpu.VMEM((tm, D_MODEL), BF16)],
        compiler_params=_params(),
        name="xattn",
    )(x, gain, w_q, kv, w_o)


def kernel(x, mem, positions, rel_bias, ffn1_norm, ffn1_w_gu, ffn1_w_down, mix_norm, w_in, sinks, conv_w, w_out, xattn_norm, mem_norm, xattn_wq, xattn_wkv, xattn_wo, ffn2_norm, ffn2_w_gu, ffn2_w_down, final_norm):
    batch, seq_len, _ = x.shape
    depth = w_in.shape[0]
    tokens = batch * seq_len
    nb = seq_len // BLOCK

    xt = x.reshape(tokens, D_MODEL)
    posq = positions.reshape(tokens // TOKEN_TILE, TOKEN_TILE // BLOCK, BLOCK)
    pos_blocks = positions.reshape(batch, nb, BLOCK)
    prev = jnp.concatenate([jnp.full((batch, 1, BLOCK), POS_PAD, positions.dtype), pos_blocks[:, :-1]], axis=1)
    posk = jnp.concatenate([prev, pos_blocks], axis=2).reshape(2 * tokens, 1)
    lut = _bias_lut(rel_bias.T)
    final_gain = final_norm.reshape(1, D_MODEL)

    for l in range(depth):
        xt = _ffn(xt, ffn1_norm[l].reshape(1, D_MODEL), ffn1_w_gu[l].astype(BF16), ffn1_w_down[l].astype(BF16),
                  final_gain, final_norm=False)
        xt = _mix(xt, posq, posk, lut, sinks[l], mix_norm[l].reshape(1, D_MODEL),
                  w_in[l].astype(BF16), conv_w[l], w_out[l].astype(BF16), seq_len=seq_len)
        kv = _mem_kv(mem.reshape(batch * N_MEM, D_MODEL), mem_norm[l].reshape(1, D_MODEL), xattn_wkv[l].astype(BF16))
        xt = _xattn(xt, xattn_norm[l].reshape(1, D_MODEL), xattn_wq[l].astype(BF16),
                    kv.reshape(batch, N_MEM, 2 * D_MODEL), xattn_wo[l].astype(BF16), seq_len=seq_len)
        xt = _ffn(xt, ffn2_norm[l].reshape(1, D_MODEL), ffn2_w_gu[l].astype(BF16), ffn2_w_down[l].astype(BF16),
                  final_gain, final_norm=(l == depth - 1))
    return xt.reshape(batch, seq_len, D_MODEL)
```

```python
import functools
import math

import jax
import jax.numpy as jnp
from jax import lax
from jax.experimental import pallas as pl
from jax.experimental.pallas import tpu as pltpu

D_MODEL = 1024
D_FF = 2816
N_MEM = 256
MEM_HEADS = 4
MEM_HEAD_DIM = D_MODEL // MEM_HEADS
HEAD_DIM = 64
SWA_HEADS = 16
SWA_KV_HEADS = 4
WINDOW = 128
BLOCK = 128
REL_BUCKETS = 32
REL_MAX_DIST = 128
CONV_K = 3
EPS = 1e-6
NEG_INF = -1e30
POS_PAD = 1 << 30

Q_WIDTH = SWA_HEADS * HEAD_DIM
KV_WIDTH = SWA_KV_HEADS * HEAD_DIM
OFF_Q = 0
OFF_K = OFF_Q + Q_WIDTH
OFF_V = OFF_K + KV_WIDTH
OFF_C = OFF_V + KV_WIDTH
OFF_B = OFF_C + D_MODEL
OFF_U = OFF_B + D_MODEL
OFF_GA = OFF_U + D_MODEL
OFF_GC = OFF_GA + D_MODEL
IN_WIDTH = OFF_GC + D_MODEL

LANES = 128
SUBLANES = 8
MXU_COLS = 256
VMEM_LIMIT_BYTES = 56 * 1024 * 1024

TOKEN_TILE = 512
FF_CHUNK = MXU_COLS
PAIR = 2 * HEAD_DIM

BF16 = jnp.bfloat16
F32 = jnp.float32


def _rms(x, g):
    return x * lax.rsqrt(jnp.mean(x * x, axis=-1, keepdims=True) + EPS) * g


def _dot(a, b):
    return jnp.dot(a, b, preferred_element_type=F32)


def _dot_nt(a, b):
    return lax.dot_general(a, b, (((1,), (1,)), ((), ())), preferred_element_type=F32)


def _resident(shape):
    return pl.BlockSpec(shape, lambda i: (0,) * len(shape), pipeline_mode=pl.Buffered(1))


def _params():
    return pltpu.CompilerParams(dimension_semantics=("arbitrary",), vmem_limit_bytes=VMEM_LIMIT_BYTES)


def _ffn_kernel(x_ref, g_ref, wgu_ref, wd_ref, fg_ref, o_ref, act_ref, *, final_norm):
    x = x_ref[...]
    h = _rms(x, g_ref[...]).astype(BF16)
    for j in range(D_FF // FF_CHUNK):
        lo = j * FF_CHUNK
        gate = _dot(h, wgu_ref[:, lo:lo + FF_CHUNK])
        up = _dot(h, wgu_ref[:, D_FF + lo:D_FF + lo + FF_CHUNK])
        act_ref[:, lo:lo + FF_CHUNK] = (gate * jax.nn.sigmoid(gate) * up).astype(BF16)
    y = x + 0.5 * _dot(act_ref[...], wd_ref[...])
    if final_norm:
        y = _rms(y, fg_ref[...])
    o_ref[...] = y


def _ffn(x, gain, w_gu, w_down, final_gain, *, final_norm):
    t = x.shape[0]
    tm = TOKEN_TILE
    return pl.pallas_call(
        functools.partial(_ffn_kernel, final_norm=final_norm),
        grid=(t // tm,),
        in_specs=[
            pl.BlockSpec((tm, D_MODEL), lambda i: (i, 0)),
            _resident((1, D_MODEL)),
            _resident((D_MODEL, 2 * D_FF)),
            _resident((D_FF, D_MODEL)),
            _resident((1, D_MODEL)),
        ],
        out_specs=pl.BlockSpec((tm, D_MODEL), lambda i: (i, 0)),
        out_shape=jax.ShapeDtypeStruct((t, D_MODEL), F32),
        scratch_shapes=[pltpu.VMEM((tm, D_FF), BF16)],
        compiler_params=_params(),
        name="ffn_final" if final_norm else "ffn",
    )(x, gain, w_gu, w_down, final_gain)


def _bias_lut_kernel(tab_ref, lut_ref):
    n = lax.broadcasted_iota(jnp.int32, (SWA_HEADS, WINDOW), 1)
    max_exact = REL_BUCKETS // 2
    nf = jnp.maximum(n, 1).astype(F32)
    large = max_exact + (jnp.log(nf / max_exact) / math.log(REL_MAX_DIST / max_exact)
                         * (REL_BUCKETS - max_exact)).astype(jnp.int32)
    large = jnp.minimum(large, REL_BUCKETS - 1)
    bucket = jnp.where(n < max_exact, n, large)
    lut = jnp.zeros((SWA_HEADS, WINDOW), F32)
    for b in range(REL_BUCKETS):
        lut = jnp.where(bucket == b, tab_ref[:, b:b + 1], lut)
    lut_ref[...] = lut


def _bias_lut(rel_bias_t):
    return pl.pallas_call(
        _bias_lut_kernel,
        out_shape=jax.ShapeDtypeStruct((SWA_HEADS, WINDOW), F32),
        name="bias_lut",
    )(rel_bias_t)


def _key_head_slabs(t):
    lane = lax.broadcasted_iota(jnp.int32, (t.shape[0], PAIR), 1)
    low_half = lane < HEAD_DIM
    los, his = [], []
    for p in range(SWA_KV_HEADS // 2):
        pair = t[:, p * PAIR:(p + 1) * PAIR]
        swapped = pltpu.roll(pair, HEAD_DIM, axis=1)
        zero = jnp.zeros_like(pair)
        los += [jnp.where(low_half, pair, zero), jnp.where(low_half, swapped, zero)]
        his += [jnp.where(low_half, zero, swapped), jnp.where(low_half, zero, pair)]
    return [a.astype(BF16) for a in los], [a.astype(BF16) for a in his]


def _value_head_slabs_t(t):
    los, his = [], []
    for p in range(SWA_KV_HEADS // 2):
        pair_t = t[:, p * PAIR:(p + 1) * PAIR].T
        zero = jnp.zeros((HEAD_DIM, t.shape[0]), F32)
        for head_t in (pair_t[0:HEAD_DIM], pair_t[HEAD_DIM:PAIR]):
            los.append(jnp.concatenate([head_t, zero], axis=0))
            his.append(jnp.concatenate([zero, head_t], axis=0))
    return [a.astype(BF16) for a in los], [a.astype(BF16) for a in his]


def _mix_kernel(x_ref, posq_ref, posk_ref, lut_ref, sink_ref, g_ref, win_ref, convw_ref, wout_ref, o_ref,
                q_s, klo_s, khi_s, vtlo_s, vthi_s, cu_s, merged_s, relm_s, bias_s, *, tiles_per_seq):
    tm = x_ref.shape[0]
    nblk = tm // BLOCK
    step = pl.program_id(0)
    first = (step % tiles_per_seq) == 0

    @pl.when(first)
    def _():
        for s in (klo_s, khi_s):
            s[:, 0:BLOCK, :] = jnp.zeros((SWA_KV_HEADS, BLOCK, PAIR), BF16)
        for s in (vtlo_s, vthi_s):
            s[:, 0] = jnp.zeros((SWA_KV_HEADS, PAIR, BLOCK), BF16)
        cu_s[0:SUBLANES, :] = jnp.zeros((SUBLANES, D_MODEL), F32)

    @pl.when(jnp.logical_not(first))
    def _():
        for s in (klo_s, khi_s):
            s[:, 0:BLOCK, :] = s[:, tm:tm + BLOCK, :]
        for s in (vtlo_s, vthi_s):
            s[:, 0] = s[:, nblk]
        cu_s[0:SUBLANES, :] = cu_s[tm:tm + SUBLANES, :]

    @pl.when(step == 0)
    def _():
        relm_s[...] = jnp.full(relm_s.shape, -2, jnp.int32)

    relms, misses = [], []
    for j in range(nblk):
        pos_k = jnp.concatenate(
            [jnp.broadcast_to(posk_ref[0, j:j + 1, c * BLOCK:(c + 1) * BLOCK], (BLOCK, BLOCK)).T
             for c in range(2)], axis=0)
        rel = posq_ref[0, j:j + 1, :] - pos_k
        relm = jnp.where((rel >= 0) & (rel < WINDOW), rel, -1)
        relms.append(relm)
        misses.append(jnp.sum((relm != relm_s[j]).astype(jnp.int32)))

    for j in range(nblk):
        @pl.when(misses[j] != 0)
        def _(j=j):
            relm_s[j] = relms[j]
            shown = relms[j] >= 0
            idx = jnp.maximum(relms[j], 0)
            for hd in range(SWA_HEADS):
                lut = jnp.broadcast_to(lut_ref[hd:hd + 1, :], (2 * BLOCK, WINDOW))
                bias_s[j, hd] = jnp.where(shown, jnp.take_along_axis(lut, idx, axis=1), NEG_INF)

    x = x_ref[...]
    h = _rms(x, g_ref[...]).astype(BF16)

    q_s[...] = (_dot(h, win_ref[:, OFF_Q:OFF_Q + Q_WIDTH]) * (HEAD_DIM ** -0.5)).astype(BF16)
    k_lo, k_hi = _key_head_slabs(_dot(h, win_ref[:, OFF_K:OFF_K + KV_WIDTH]))
    vt_lo, vt_hi = _value_head_slabs_t(_dot(h, win_ref[:, OFF_V:OFF_V + KV_WIDTH]))
    for g in range(SWA_KV_HEADS):
        klo_s[g, BLOCK:BLOCK + tm, :] = k_lo[g]
        khi_s[g, BLOCK:BLOCK + tm, :] = k_hi[g]
        for j in range(nblk):
            vtlo_s[g, j + 1] = vt_lo[g][:, j * BLOCK:(j + 1) * BLOCK]
            vthi_s[g, j + 1] = vt_hi[g][:, j * BLOCK:(j + 1) * BLOCK]

    def qk(blk, g):
        rows = slice(blk * BLOCK, (blk + 1) * BLOCK)
        keys = slice(blk * BLOCK, (blk + 2) * BLOCK)
        c0 = g * 2 * PAIR
        q2 = jnp.concatenate([q_s[rows, c0:c0 + PAIR], q_s[rows, c0 + PAIR:c0 + 2 * PAIR]], axis=0)
        return _dot_nt(klo_s[g, keys, :], q2), _dot_nt(khi_s[g, keys, :], q2)

    def softmax_pv(blk, g, logits_t):
        probs_t = [[None, None], [None, None]]
        inv = [[None, None], [None, None]]
        for half in range(2):
            for pair in range(2):
                head = g * 4 + pair * 2 + half
                lt = logits_t[half][:, pair * BLOCK:(pair + 1) * BLOCK] + bias_s[blk, head]
                sink = sink_ref[head]
                m = jnp.maximum(jnp.max(lt, axis=0, keepdims=True), sink)
                e = jnp.exp(lt - m)
                inv[half][pair] = 1.0 / (jnp.sum(e, axis=0, keepdims=True) + jnp.exp(sink - m))
                probs_t[half][pair] = e.astype(BF16)
        v_lo_t = jnp.concatenate([vtlo_s[g, blk], vtlo_s[g, blk + 1]], axis=1)
        v_hi_t = jnp.concatenate([vthi_s[g, blk], vthi_s[g, blk + 1]], axis=1)
        out_t = (_dot(v_lo_t, jnp.concatenate(probs_t[0], axis=1))
                 + _dot(v_hi_t, jnp.concatenate(probs_t[1], axis=1)))
        scale = jnp.concatenate(
            [jnp.broadcast_to(jnp.concatenate(inv[0], axis=1), (HEAD_DIM, 2 * BLOCK)),
             jnp.broadcast_to(jnp.concatenate(inv[1], axis=1), (HEAD_DIM, 2 * BLOCK))], axis=0)
        return (out_t * scale).T

    units = [(blk, g) for g in range(SWA_KV_HEADS) for blk in range(nblk)]
    logits_next = qk(*units[0])
    for g in range(SWA_KV_HEADS):
        cols = slice(g * 2 * PAIR, (g + 1) * 2 * PAIR)

        def proj(off):
            return _dot(h, win_ref[:, off + g * 2 * PAIR:off + (g + 1) * 2 * PAIR])

        def conv_taps(_):
            cu = proj(OFF_C) * proj(OFF_U)
            cu_s[SUBLANES:SUBLANES + tm, cols] = cu
            return (convw_ref[0:1, cols] * cu_s[SUBLANES - 2:SUBLANES - 2 + tm, cols]
                    + convw_ref[1:2, cols] * cu_s[SUBLANES - 1:SUBLANES - 1 + tm, cols]
                    + convw_ref[2:3, cols] * cu)

        stages = [conv_taps,
                  lambda conv: proj(OFF_B) * conv,
                  lambda conv: jax.nn.sigmoid(proj(OFF_GC)) * conv]
        attn_blocks = []
        conv = None
        for blk in range(nblk):
            logits_t = logits_next
            nxt = g * nblk + blk + 1
            if nxt < len(units):
                logits_next = qk(*units[nxt])
            if blk < len(stages):
                conv = stages[blk](conv)
            out = softmax_pv(blk, g, logits_t)
            attn_blocks.append(jnp.concatenate([out[0:BLOCK], out[BLOCK:2 * BLOCK]], axis=1))
        for stage in stages[nblk:]:
            conv = stage(conv)
        attn = jnp.concatenate(attn_blocks, axis=0)
        merged_s[:, cols] = (jax.nn.sigmoid(proj(OFF_GA)) * attn + conv).astype(BF16)

    o_ref[...] = x + _dot(merged_s[...], wout_ref[...])


def _mix(x, posq, posk, lut, sinks, gain, w_in, conv_w, w_out, *, seq_len):
    t = x.shape[0]
    tm = TOKEN_TILE
    nblk = tm // BLOCK
    k_scratch = pltpu.VMEM((SWA_KV_HEADS, BLOCK + tm, PAIR), BF16)
    vt_scratch = pltpu.VMEM((SWA_KV_HEADS, nblk + 1, PAIR, BLOCK), BF16)
    return pl.pallas_call(
        functools.partial(_mix_kernel, tiles_per_seq=seq_len // tm),
        grid=(t // tm,),
        in_specs=[
            pl.BlockSpec((tm, D_MODEL), lambda i: (i, 0)),
            pl.BlockSpec((1, nblk, BLOCK), lambda i: (i, 0, 0)),
            pl.BlockSpec((1, nblk, 2 * BLOCK), lambda i: (i, 0, 0)),
            _resident((SWA_HEADS, WINDOW)),
            pl.BlockSpec(memory_space=pltpu.SMEM),
            _resident((1, D_MODEL)),
            _resident((D_MODEL, IN_WIDTH)),
            _resident((CONV_K, D_MODEL)),
            _resident((D_MODEL, D_MODEL)),
        ],
        out_specs=pl.BlockSpec((tm, D_MODEL), lambda i: (i, 0)),
        out_shape=jax.ShapeDtypeStruct((t, D_MODEL), F32),
        scratch_shapes=[
            pltpu.VMEM((tm, Q_WIDTH), BF16),
            k_scratch, k_scratch, vt_scratch, vt_scratch,
            pltpu.VMEM((SUBLANES + tm, D_MODEL), F32),
            pltpu.VMEM((tm, D_MODEL), BF16),
            pltpu.VMEM((nblk, 2 * BLOCK, BLOCK), jnp.int32),
            pltpu.VMEM((nblk, SWA_HEADS, 2 * BLOCK, BLOCK), F32),
        ],
        compiler_params=_params(),
        name="mix",
    )(x, posq, posk, lut, sinks, gain, w_in, conv_w, w_out)


def _mem_kv_kernel(mem_ref, g_ref, wkv_ref, kv_ref):
    h = _rms(mem_ref[...], g_ref[...]).astype(BF16)
    kv_ref[...] = _dot(h, wkv_ref[...]).astype(BF16)


def _mem_kv(mem, gain, w_kv):
    rows = mem.shape[0]
    tm = TOKEN_TILE
    return pl.pallas_call(
        _mem_kv_kernel,
        grid=(rows // tm,),
        in_specs=[
            pl.BlockSpec((tm, D_MODEL), lambda i: (i, 0)),
            _resident((1, D_MODEL)),
            _resident((D_MODEL, 2 * D_MODEL)),
        ],
        out_specs=pl.BlockSpec((tm, 2 * D_MODEL), lambda i: (i, 0)),
        out_shape=jax.ShapeDtypeStruct((rows, 2 * D_MODEL), BF16),
        compiler_params=_params(),
        name="mem_kv",
    )(mem, gain, w_kv)


def _xattn_kernel(x_ref, g_ref, wq_ref, kv_ref, wo_ref, o_ref, ctx_s):
    x = x_ref[...]
    h = _rms(x, g_ref[...]).astype(BF16)
    q = (_dot(h, wq_ref[...]) * (MEM_HEAD_DIM ** -0.5)).astype(BF16)
    for hd in range(MEM_HEADS):
        c0 = hd * MEM_HEAD_DIM
        lg = _dot_nt(q[:, c0:c0 + MEM_HEAD_DIM], kv_ref[0, :, c0:c0 + MEM_HEAD_DIM])
        e = jnp.exp(lg - jnp.max(lg, axis=1, keepdims=True))
        p = (e / jnp.sum(e, axis=1, keepdims=True)).astype(BF16)
        ctx_s[:, c0:c0 + MEM_HEAD_DIM] = _dot(
            p, kv_ref[0, :, D_MODEL + c0:D_MODEL + c0 + MEM_HEAD_DIM]).astype(BF16)
    o_ref[...] = x + _dot(ctx_s[...], wo_ref[...])


def _xattn(x, gain, w_q, kv, w_o, *, seq_len):
    t = x.shape[0]
    tm = TOKEN_TILE
    tiles_per_seq = seq_len // tm
    return pl.pallas_call(
        _xattn_kernel,
        grid=(t // tm,),
        in_specs=[
            pl.BlockSpec((tm, D_MODEL), lambda i: (i, 0)),
            _resident((1, D_MODEL)),
            _resident((D_MODEL, D_MODEL)),
            pl.BlockSpec((1, N_MEM, 2 * D_MODEL), lambda i: (i // tiles_per_seq, 0, 0)),
            _resident((D_MODEL, D_MODEL)),
        ],
        out_specs=pl.BlockSpec((tm, D_MODEL), lambda i: (i, 0)),
        out_shape=jax.ShapeDtypeStruct((t, D_MODEL), F32),
        scratch_shapes=[pltpu.VMEM((tm, D_MODEL), BF16)],
        compiler_params=_params(),
        name="xattn",
    )(x, gain, w_q, kv, w_o)


def kernel(x, mem, positions, rel_bias, ffn1_norm, ffn1_w_gu, ffn1_w_down, mix_norm, w_in, sinks, conv_w, w_out, xattn_norm, mem_norm, xattn_wq, xattn_wkv, xattn_wo, ffn2_norm, ffn2_w_gu, ffn2_w_down, final_norm):
    batch, seq_len, _ = x.shape
    depth = w_in.shape[0]
    tokens = batch * seq_len
    nb = seq_len // BLOCK

    xt = x.reshape(tokens, D_MODEL)
    posq = positions.reshape(tokens // TOKEN_TILE, TOKEN_TILE // BLOCK, BLOCK)
    pos_blocks = positions.reshape(batch, nb, BLOCK)
    prev = jnp.concatenate([jnp.full((batch, 1, BLOCK), POS_PAD, positions.dtype), pos_blocks[:, :-1]], axis=1)
    posk = jnp.concatenate([prev, pos_blocks], axis=2).reshape(tokens // TOKEN_TILE, TOKEN_TILE // BLOCK, 2 * BLOCK)
    lut = _bias_lut(rel_bias.T)
    final_gain = final_norm.reshape(1, D_MODEL)

    for l in range(depth):
        xt = _ffn(xt, ffn1_norm[l].reshape(1, D_MODEL), ffn1_w_gu[l].astype(BF16), ffn1_w_down[l].astype(BF16),
                  final_gain, final_norm=False)
        xt = _mix(xt, posq, posk, lut, sinks[l], mix_norm[l].reshape(1, D_MODEL),
                  w_in[l].astype(BF16), conv_w[l], w_out[l].astype(BF16), seq_len=seq_len)
        kv = _mem_kv(mem.reshape(batch * N_MEM, D_MODEL), mem_norm[l].reshape(1, D_MODEL), xattn_wkv[l].astype(BF16))
        xt = _xattn(xt, xattn_norm[l].reshape(1, D_MODEL), xattn_wq[l].astype(BF16),
                    kv.reshape(batch, N_MEM, 2 * D_MODEL), xattn_wo[l].astype(BF16), seq_len=seq_len)
        xt = _ffn(xt, ffn2_norm[l].reshape(1, D_MODEL), ffn2_w_gu[l].astype(BF16), ffn2_w_down[l].astype(BF16),
                  final_gain, final_norm=(l == depth - 1))
    return xt.reshape(batch, seq_len, D_MODEL)
```

```python
import functools
import math

import jax
import jax.numpy as jnp
from jax import lax
from jax.experimental import pallas as pl
from jax.experimental.pallas import tpu as pltpu

D_MODEL = 1024
D_FF = 2816
N_MEM = 256
MEM_HEADS = 4
MEM_HEAD_DIM = D_MODEL // MEM_HEADS
HEAD_DIM = 64
SWA_HEADS = 16
SWA_KV_HEADS = 4
WINDOW = 128
BLOCK = 128
REL_BUCKETS = 32
REL_MAX_DIST = 128
CONV_K = 3
EPS = 1e-6
NEG_INF = -1e30
POS_PAD = 1 << 30

Q_WIDTH = SWA_HEADS * HEAD_DIM
KV_WIDTH = SWA_KV_HEADS * HEAD_DIM
OFF_Q = 0
OFF_K = OFF_Q + Q_WIDTH
OFF_V = OFF_K + KV_WIDTH
OFF_C = OFF_V + KV_WIDTH
OFF_B = OFF_C + D_MODEL
OFF_U = OFF_B + D_MODEL
OFF_GA = OFF_U + D_MODEL
OFF_GC = OFF_GA + D_MODEL
IN_WIDTH = OFF_GC + D_MODEL

LANES = 128
SUBLANES = 8
MXU_COLS = 256
VMEM_LIMIT_BYTES = 56 * 1024 * 1024

TOKEN_TILE = 512
FFN_TOKEN_TILE = 1024
FF_CHUNK = MXU_COLS
PAIR = 2 * HEAD_DIM

BF16 = jnp.bfloat16
F32 = jnp.float32


def _rms(x, g):
    return x * lax.rsqrt(jnp.mean(x * x, axis=-1, keepdims=True) + EPS) * g


def _dot(a, b):
    return jnp.dot(a, b, preferred_element_type=F32)


def _dot_nt(a, b):
    return lax.dot_general(a, b, (((1,), (1,)), ((), ())), preferred_element_type=F32)


def _resident(shape):
    return pl.BlockSpec(shape, lambda i: (0,) * len(shape), pipeline_mode=pl.Buffered(1))


def _params():
    return pltpu.CompilerParams(dimension_semantics=("arbitrary",), vmem_limit_bytes=VMEM_LIMIT_BYTES)


def _ffn_kernel(x_ref, g_ref, wgu_ref, wd_ref, fg_ref, o_ref, act_ref, *, final_norm):
    x = x_ref[...]
    h = _rms(x, g_ref[...]).astype(BF16)
    for j in range(D_FF // FF_CHUNK):
        lo = j * FF_CHUNK
        gate = _dot(h, wgu_ref[:, lo:lo + FF_CHUNK])
        up = _dot(h, wgu_ref[:, D_FF + lo:D_FF + lo + FF_CHUNK])
        act_ref[:, lo:lo + FF_CHUNK] = (gate * jax.nn.sigmoid(gate) * up).astype(BF16)
    y = x + 0.5 * _dot(act_ref[...], wd_ref[...])
    if final_norm:
        y = _rms(y, fg_ref[...])
    o_ref[...] = y


def _ffn(x, gain, w_gu, w_down, final_gain, *, final_norm):
    t = x.shape[0]
    tm = FFN_TOKEN_TILE
    return pl.pallas_call(
        functools.partial(_ffn_kernel, final_norm=final_norm),
        grid=(t // tm,),
        in_specs=[
            pl.BlockSpec((tm, D_MODEL), lambda i: (i, 0)),
            _resident((1, D_MODEL)),
            _resident((D_MODEL, 2 * D_FF)),
            _resident((D_FF, D_MODEL)),
            _resident((1, D_MODEL)),
        ],
        out_specs=pl.BlockSpec((tm, D_MODEL), lambda i: (i, 0)),
        out_shape=jax.ShapeDtypeStruct((t, D_MODEL), F32),
        scratch_shapes=[pltpu.VMEM((tm, D_FF), BF16)],
        compiler_params=_params(),
        name="ffn_final" if final_norm else "ffn",
    )(x, gain, w_gu, w_down, final_gain)


def _bias_lut_kernel(tab_ref, lut_ref):
    n = lax.broadcasted_iota(jnp.int32, (SWA_HEADS, WINDOW), 1)
    max_exact = REL_BUCKETS // 2
    nf = jnp.maximum(n, 1).astype(F32)
    large = max_exact + (jnp.log(nf / max_exact) / math.log(REL_MAX_DIST / max_exact)
                         * (REL_BUCKETS - max_exact)).astype(jnp.int32)
    large = jnp.minimum(large, REL_BUCKETS - 1)
    bucket = jnp.where(n < max_exact, n, large)
    lut = jnp.zeros((SWA_HEADS, WINDOW), F32)
    for b in range(REL_BUCKETS):
        lut = jnp.where(bucket == b, tab_ref[:, b:b + 1], lut)
    lut_ref[...] = lut


def _bias_lut(rel_bias_t):
    return pl.pallas_call(
        _bias_lut_kernel,
        out_shape=jax.ShapeDtypeStruct((SWA_HEADS, WINDOW), F32),
        name="bias_lut",
    )(rel_bias_t)


def _key_head_slabs(t):
    lane = lax.broadcasted_iota(jnp.int32, (t.shape[0], PAIR), 1)
    low_half = lane < HEAD_DIM
    los, his = [], []
    for p in range(SWA_KV_HEADS // 2):
        pair = t[:, p * PAIR:(p + 1) * PAIR]
        swapped = pltpu.roll(pair, HEAD_DIM, axis=1)
        zero = jnp.zeros_like(pair)
        los += [jnp.where(low_half, pair, zero), jnp.where(low_half, swapped, zero)]
        his += [jnp.where(low_half, zero, swapped), jnp.where(low_half, zero, pair)]
    return [a.astype(BF16) for a in los], [a.astype(BF16) for a in his]


def _value_head_slabs_t(t):
    los, his = [], []
    for p in range(SWA_KV_HEADS // 2):
        pair_t = t[:, p * PAIR:(p + 1) * PAIR].T
        zero = jnp.zeros((HEAD_DIM, t.shape[0]), F32)
        for head_t in (pair_t[0:HEAD_DIM], pair_t[HEAD_DIM:PAIR]):
            los.append(jnp.concatenate([head_t, zero], axis=0))
            his.append(jnp.concatenate([zero, head_t], axis=0))
    return [a.astype(BF16) for a in los], [a.astype(BF16) for a in his]


def _mix_kernel(x_ref, posq_ref, posk_ref, lut_ref, sink_ref, g_ref, win_ref, convw_ref, wout_ref, o_ref,
                q_s, klo_s, khi_s, vtlo_s, vthi_s, cu_s, merged_s, relm_s, bias_s, *, tiles_per_seq):
    tm = x_ref.shape[0]
    nblk = tm // BLOCK
    step = pl.program_id(0)
    first = (step % tiles_per_seq) == 0

    @pl.when(first)
    def _():
        for s in (klo_s, khi_s):
            s[:, 0:BLOCK, :] = jnp.zeros((SWA_KV_HEADS, BLOCK, PAIR), BF16)
        for s in (vtlo_s, vthi_s):
            s[:, 0] = jnp.zeros((SWA_KV_HEADS, PAIR, BLOCK), BF16)
        cu_s[0:SUBLANES, :] = jnp.zeros((SUBLANES, D_MODEL), F32)

    @pl.when(jnp.logical_not(first))
    def _():
        for s in (klo_s, khi_s):
            s[:, 0:BLOCK, :] = s[:, tm:tm + BLOCK, :]
        for s in (vtlo_s, vthi_s):
            s[:, 0] = s[:, nblk]
        cu_s[0:SUBLANES, :] = cu_s[tm:tm + SUBLANES, :]

    @pl.when(step == 0)
    def _():
        relm_s[...] = jnp.full(relm_s.shape, -2, jnp.int32)

    relms, misses = [], []
    for j in range(nblk):
        pos_k = jnp.concatenate(
            [jnp.broadcast_to(posk_ref[0, j:j + 1, c * BLOCK:(c + 1) * BLOCK], (BLOCK, BLOCK)).T
             for c in range(2)], axis=0)
        rel = posq_ref[0, j:j + 1, :] - pos_k
        relm = jnp.where((rel >= 0) & (rel < WINDOW), rel, -1)
        relms.append(relm)
        misses.append(jnp.sum((relm != relm_s[j]).astype(jnp.int32)))

    for j in range(nblk):
        @pl.when(misses[j] != 0)
        def _(j=j):
            relm_s[j] = relms[j]
            shown = relms[j] >= 0
            idx = jnp.maximum(relms[j], 0)
            for hd in range(SWA_HEADS):
                lut = jnp.broadcast_to(lut_ref[hd:hd + 1, :], (2 * BLOCK, WINDOW))
                bias_s[j, hd] = jnp.where(shown, jnp.take_along_axis(lut, idx, axis=1), NEG_INF)

    x = x_ref[...]
    h = _rms(x, g_ref[...]).astype(BF16)

    q_s[...] = (_dot(h, win_ref[:, OFF_Q:OFF_Q + Q_WIDTH]) * (HEAD_DIM ** -0.5)).astype(BF16)
    k_lo, k_hi = _key_head_slabs(_dot(h, win_ref[:, OFF_K:OFF_K + KV_WIDTH]))
    vt_lo, vt_hi = _value_head_slabs_t(_dot(h, win_ref[:, OFF_V:OFF_V + KV_WIDTH]))
    for g in range(SWA_KV_HEADS):
        klo_s[g, BLOCK:BLOCK + tm, :] = k_lo[g]
        khi_s[g, BLOCK:BLOCK + tm, :] = k_hi[g]
        for j in range(nblk):
            vtlo_s[g, j + 1] = vt_lo[g][:, j * BLOCK:(j + 1) * BLOCK]
            vthi_s[g, j + 1] = vt_hi[g][:, j * BLOCK:(j + 1) * BLOCK]

    def qk(blk, g):
        rows = slice(blk * BLOCK, (blk + 1) * BLOCK)
        keys = slice(blk * BLOCK, (blk + 2) * BLOCK)
        c0 = g * 2 * PAIR
        q2 = jnp.concatenate([q_s[rows, c0:c0 + PAIR], q_s[rows, c0 + PAIR:c0 + 2 * PAIR]], axis=0)
        return _dot_nt(klo_s[g, keys, :], q2), _dot_nt(khi_s[g, keys, :], q2)

    def softmax_pv(blk, g, logits_t):
        probs_t = [[None, None], [None, None]]
        inv = [[None, None], [None, None]]
        for half in range(2):
            for pair in range(2):
                head = g * 4 + pair * 2 + half
                lt = logits_t[half][:, pair * BLOCK:(pair + 1) * BLOCK] + bias_s[blk, head]
                sink = sink_ref[head]
                m = jnp.maximum(jnp.max(lt, axis=0, keepdims=True), sink)
                e = jnp.exp(lt - m)
                inv[half][pair] = 1.0 / (jnp.sum(e, axis=0, keepdims=True) + jnp.exp(sink - m))
                probs_t[half][pair] = e.astype(BF16)
        v_lo_t = jnp.concatenate([vtlo_s[g, blk], vtlo_s[g, blk + 1]], axis=1)
        v_hi_t = jnp.concatenate([vthi_s[g, blk], vthi_s[g, blk + 1]], axis=1)
        out_t = (_dot(v_lo_t, jnp.concatenate(probs_t[0], axis=1))
                 + _dot(v_hi_t, jnp.concatenate(probs_t[1], axis=1)))
        scale = jnp.concatenate(
            [jnp.broadcast_to(jnp.concatenate(inv[0], axis=1), (HEAD_DIM, 2 * BLOCK)),
             jnp.broadcast_to(jnp.concatenate(inv[1], axis=1), (HEAD_DIM, 2 * BLOCK))], axis=0)
        return (out_t * scale).T

    units = [(blk, g) for g in range(SWA_KV_HEADS) for blk in range(nblk)]
    logits_next = qk(*units[0])
    for g in range(SWA_KV_HEADS):
        cols = slice(g * 2 * PAIR, (g + 1) * 2 * PAIR)

        def proj(off):
            return _dot(h, win_ref[:, off + g * 2 * PAIR:off + (g + 1) * 2 * PAIR])

        def conv_taps(_):
            cu = proj(OFF_C) * proj(OFF_U)
            cu_s[SUBLANES:SUBLANES + tm, cols] = cu
            return (convw_ref[0:1, cols] * cu_s[SUBLANES - 2:SUBLANES - 2 + tm, cols]
                    + convw_ref[1:2, cols] * cu_s[SUBLANES - 1:SUBLANES - 1 + tm, cols]
                    + convw_ref[2:3, cols] * cu)

        stages = [conv_taps,
                  lambda conv: proj(OFF_B) * conv,
                  lambda conv: jax.nn.sigmoid(proj(OFF_GC)) * conv]
        attn_blocks = []
        conv = None
        for blk in range(nblk):
            logits_t = logits_next
            nxt = g * nblk + blk + 1
            if nxt < len(units):
                logits_next = qk(*units[nxt])
            if blk < len(stages):
                conv = stages[blk](conv)
            out = softmax_pv(blk, g, logits_t)
            attn_blocks.append(jnp.concatenate([out[0:BLOCK], out[BLOCK:2 * BLOCK]], axis=1))
        for stage in stages[nblk:]:
            conv = stage(conv)
        attn = jnp.concatenate(attn_blocks, axis=0)
        merged_s[:, cols] = (jax.nn.sigmoid(proj(OFF_GA)) * attn + conv).astype(BF16)

    o_ref[...] = x + _dot(merged_s[...], wout_ref[...])


def _mix(x, posq, posk, lut, sinks, gain, w_in, conv_w, w_out, *, seq_len):
    t = x.shape[0]
    tm = TOKEN_TILE
    nblk = tm // BLOCK
    k_scratch = pltpu.VMEM((SWA_KV_HEADS, BLOCK + tm, PAIR), BF16)
    vt_scratch = pltpu.VMEM((SWA_KV_HEADS, nblk + 1, PAIR, BLOCK), BF16)
    return pl.pallas_call(
        functools.partial(_mix_kernel, tiles_per_seq=seq_len // tm),
        grid=(t // tm,),
        in_specs=[
            pl.BlockSpec((tm, D_MODEL), lambda i: (i, 0)),
            pl.BlockSpec((1, nblk, BLOCK), lambda i: (i, 0, 0)),
            pl.BlockSpec((1, nblk, 2 * BLOCK), lambda i: (i, 0, 0)),
            _resident((SWA_HEADS, WINDOW)),
            pl.BlockSpec(memory_space=pltpu.SMEM),
            _resident((1, D_MODEL)),
            _resident((D_MODEL, IN_WIDTH)),
            _resident((CONV_K, D_MODEL)),
            _resident((D_MODEL, D_MODEL)),
        ],
        out_specs=pl.BlockSpec((tm, D_MODEL), lambda i: (i, 0)),
        out_shape=jax.ShapeDtypeStruct((t, D_MODEL), F32),
        scratch_shapes=[
            pltpu.VMEM((tm, Q_WIDTH), BF16),
            k_scratch, k_scratch, vt_scratch, vt_scratch,
            pltpu.VMEM((SUBLANES + tm, D_MODEL), F32),
            pltpu.VMEM((tm, D_MODEL), BF16),
            pltpu.VMEM((nblk, 2 * BLOCK, BLOCK), jnp.int32),
            pltpu.VMEM((nblk, SWA_HEADS, 2 * BLOCK, BLOCK), F32),
        ],
        compiler_params=_params(),
        name="mix",
    )(x, posq, posk, lut, sinks, gain, w_in, conv_w, w_out)


def _mem_kv_kernel(mem_ref, g_ref, wkv_ref, kv_ref):
    h = _rms(mem_ref[...], g_ref[...]).astype(BF16)
    kv_ref[...] = _dot(h, wkv_ref[...]).astype(BF16)


def _mem_kv(mem, gain, w_kv):
    rows = mem.shape[0]
    tm = TOKEN_TILE
    return pl.pallas_call(
        _mem_kv_kernel,
        grid=(rows // tm,),
        in_specs=[
            pl.BlockSpec((tm, D_MODEL), lambda i: (i, 0)),
            _resident((1, D_MODEL)),
            _resident((D_MODEL, 2 * D_MODEL)),
        ],
        out_specs=pl.BlockSpec((tm, 2 * D_MODEL), lambda i: (i, 0)),
        out_shape=jax.ShapeDtypeStruct((rows, 2 * D_MODEL), BF16),
        compiler_params=_params(),
        name="mem_kv",
    )(mem, gain, w_kv)


def _xattn_kernel(x_ref, g_ref, wq_ref, kv_ref, wo_ref, o_ref):
    x = x_ref[...]
    h = _rms(x, g_ref[...]).astype(BF16)
    def head_cols(hd):
        return slice(hd * MEM_HEAD_DIM, (hd + 1) * MEM_HEAD_DIM)

    def q_proj(hd):
        return (_dot(h, wq_ref[:, head_cols(hd)]) * (MEM_HEAD_DIM ** -0.5)).astype(BF16)

    def logits(hd, q):
        return _dot_nt(q, kv_ref[0, :, head_cols(hd)])

    def context(hd, lg):
        e = jnp.exp(lg - jnp.max(lg, axis=1, keepdims=True))
        p = (e / jnp.sum(e, axis=1, keepdims=True)).astype(BF16)
        v = kv_ref[0, :, D_MODEL + hd * MEM_HEAD_DIM:D_MODEL + (hd + 1) * MEM_HEAD_DIM]
        return _dot(p, v).astype(BF16)

    qs = [q_proj(0), q_proj(1)]
    lgs = [logits(0, qs[0])]
    acc = x
    for hd in range(MEM_HEADS):
        if hd + 2 < MEM_HEADS:
            qs.append(q_proj(hd + 2))
        if hd + 1 < MEM_HEADS:
            lgs.append(logits(hd + 1, qs[hd + 1]))
        acc = acc + _dot(context(hd, lgs[hd]), wo_ref[head_cols(hd), :])
    o_ref[...] = acc


def _xattn(x, gain, w_q, kv, w_o, *, seq_len):
    t = x.shape[0]
    tm = TOKEN_TILE
    tiles_per_seq = seq_len // tm
    return pl.pallas_call(
        _xattn_kernel,
        grid=(t // tm,),
        in_specs=[
            pl.BlockSpec((tm, D_MODEL), lambda i: (i, 0)),
            _resident((1, D_MODEL)),
            _resident((D_MODEL, D_MODEL)),
            pl.BlockSpec((1, N_MEM, 2 * D_MODEL), lambda i: (i // tiles_per_seq, 0, 0)),
            _resident((D_MODEL, D_MODEL)),
        ],
        out_specs=pl.BlockSpec((tm, D_MODEL), lambda i: (i, 0)),
        out_shape=jax.ShapeDtypeStruct((t, D_MODEL), F32),
        compiler_params=_params(),
        name="xattn",
    )(x, gain, w_q, kv, w_o)


def kernel(x, mem, positions, rel_bias, ffn1_norm, ffn1_w_gu, ffn1_w_down, mix_norm, w_in, sinks, conv_w, w_out, xattn_norm, mem_norm, xattn_wq, xattn_wkv, xattn_wo, ffn2_norm, ffn2_w_gu, ffn2_w_down, final_norm):
    batch, seq_len, _ = x.shape
    depth = w_in.shape[0]
    tokens = batch * seq_len
    nb = seq_len // BLOCK

    xt = x.reshape(tokens, D_MODEL)
    posq = positions.reshape(tokens // TOKEN_TILE, TOKEN_TILE // BLOCK, BLOCK)
    pos_blocks = positions.reshape(batch, nb, BLOCK)
    prev = jnp.concatenate([jnp.full((batch, 1, BLOCK), POS_PAD, positions.dtype), pos_blocks[:, :-1]], axis=1)
    posk = jnp.concatenate([prev, pos_blocks], axis=2).reshape(tokens // TOKEN_TILE, TOKEN_TILE // BLOCK, 2 * BLOCK)
    lut = _bias_lut(rel_bias.T)
    final_gain = final_norm.reshape(1, D_MODEL)

    for l in range(depth):
        xt = _ffn(xt, ffn1_norm[l].reshape(1, D_MODEL), ffn1_w_gu[l].astype(BF16), ffn1_w_down[l].astype(BF16),
                  final_gain, final_norm=False)
        xt = _mix(xt, posq, posk, lut, sinks[l], mix_norm[l].reshape(1, D_MODEL),
                  w_in[l].astype(BF16), conv_w[l], w_out[l].astype(BF16), seq_len=seq_len)
        kv = _mem_kv(mem.reshape(batch * N_MEM, D_MODEL), mem_norm[l].reshape(1, D_MODEL), xattn_wkv[l].astype(BF16))
        xt = _xattn(xt, xattn_norm[l].reshape(1, D_MODEL), xattn_wq[l].astype(BF16),
                    kv.reshape(batch, N_MEM, 2 * D_MODEL), xattn_wo[l].astype(BF16), seq_len=seq_len)
        xt = _ffn(xt, ffn2_norm[l].reshape(1, D_MODEL), ffn2_w_gu[l].astype(BF16), ffn2_w_down[l].astype(BF16),
                  final_gain, final_norm=(l == depth - 1))
    return xt.reshape(batch, seq_len, D_MODEL)
```

```python
import functools
import math

import jax
import jax.numpy as jnp
from jax import lax
from jax.experimental import pallas as pl
from jax.experimental.pallas import tpu as pltpu

D_MODEL = 1024
D_FF = 2816
N_MEM = 256
MEM_HEADS = 4
MEM_HEAD_DIM = D_MODEL // MEM_HEADS
HEAD_DIM = 64
SWA_HEADS = 16
SWA_KV_HEADS = 4
WINDOW = 128
BLOCK = 128
REL_BUCKETS = 32
REL_MAX_DIST = 128
CONV_K = 3
EPS = 1e-6
NEG_INF = -1e30
POS_PAD = 1 << 30

Q_WIDTH = SWA_HEADS * HEAD_DIM
KV_WIDTH = SWA_KV_HEADS * HEAD_DIM
OFF_Q = 0
OFF_K = OFF_Q + Q_WIDTH
OFF_V = OFF_K + KV_WIDTH
OFF_C = OFF_V + KV_WIDTH
OFF_B = OFF_C + D_MODEL
OFF_U = OFF_B + D_MODEL
OFF_GA = OFF_U + D_MODEL
OFF_GC = OFF_GA + D_MODEL
IN_WIDTH = OFF_GC + D_MODEL

LANES = 128
SUBLANES = 8
MXU_COLS = 256
VMEM_LIMIT_BYTES = 60 * 1024 * 1024

TOKEN_TILE = 512
FFN_TOKEN_TILE = 512
FF_CHUNK = MXU_COLS
PAIR = 2 * HEAD_DIM

BF16 = jnp.bfloat16
F32 = jnp.float32


def _rms(x, g):
    return x * lax.rsqrt(jnp.mean(x * x, axis=-1, keepdims=True) + EPS) * g


def _dot(a, b):
    return jnp.dot(a.astype(BF16), b.astype(BF16), preferred_element_type=F32)


def _dot_nt(a, b):
    return lax.dot_general(a.astype(BF16), b.astype(BF16), (((1,), (1,)), ((), ())),
                           preferred_element_type=F32)


def _resident(shape):
    return pl.BlockSpec(shape, lambda i: (0,) * len(shape), pipeline_mode=pl.Buffered(1))


def _params():
    return pltpu.CompilerParams(dimension_semantics=("arbitrary",), vmem_limit_bytes=VMEM_LIMIT_BYTES)


def _ffn_kernel(x_ref, g_ref, wgu_ref, wd_ref, fg_ref, o_ref, act_ref, *, final_norm):
    x = x_ref[...]
    h = _rms(x, g_ref[...]).astype(BF16)
    for j in range(D_FF // FF_CHUNK):
        lo = j * FF_CHUNK
        gate = _dot(h, wgu_ref[:, lo:lo + FF_CHUNK])
        up = _dot(h, wgu_ref[:, D_FF + lo:D_FF + lo + FF_CHUNK])
        act_ref[:, lo:lo + FF_CHUNK] = (gate * jax.nn.sigmoid(gate) * up).astype(BF16)
    y = x + 0.5 * _dot(act_ref[...], wd_ref[...])
    if final_norm:
        y = _rms(y, fg_ref[...])
    o_ref[...] = y


def _ffn(x, gain, w_gu, w_down, final_gain, *, final_norm):
    t = x.shape[0]
    tm = FFN_TOKEN_TILE
    return pl.pallas_call(
        functools.partial(_ffn_kernel, final_norm=final_norm),
        grid=(t // tm,),
        in_specs=[
            pl.BlockSpec((tm, D_MODEL), lambda i: (i, 0)),
            _resident((1, D_MODEL)),
            _resident((D_MODEL, 2 * D_FF)),
            _resident((D_FF, D_MODEL)),
            _resident((1, D_MODEL)),
        ],
        out_specs=pl.BlockSpec((tm, D_MODEL), lambda i: (i, 0)),
        out_shape=jax.ShapeDtypeStruct((t, D_MODEL), F32),
        scratch_shapes=[pltpu.VMEM((tm, D_FF), BF16)],
        compiler_params=_params(),
        name="ffn_final" if final_norm else "ffn",
    )(x, gain, w_gu, w_down, final_gain)


def _bias_lut_kernel(tab_ref, lut_ref):
    n = lax.broadcasted_iota(jnp.int32, (SWA_HEADS, WINDOW), 1)
    max_exact = REL_BUCKETS // 2
    nf = jnp.maximum(n, 1).astype(F32)
    large = max_exact + (jnp.log(nf / max_exact) / math.log(REL_MAX_DIST / max_exact)
                         * (REL_BUCKETS - max_exact)).astype(jnp.int32)
    large = jnp.minimum(large, REL_BUCKETS - 1)
    bucket = jnp.where(n < max_exact, n, large)
    lut = jnp.zeros((SWA_HEADS, WINDOW), F32)
    for b in range(REL_BUCKETS):
        lut = jnp.where(bucket == b, tab_ref[:, b:b + 1], lut)
    lut_ref[...] = lut


def _bias_lut(rel_bias_t):
    return pl.pallas_call(
        _bias_lut_kernel,
        out_shape=jax.ShapeDtypeStruct((SWA_HEADS, WINDOW), F32),
        name="bias_lut",
    )(rel_bias_t)


def _key_head_slabs(t):
    lane = lax.broadcasted_iota(jnp.int32, (t.shape[0], PAIR), 1)
    low_half = lane < HEAD_DIM
    los, his = [], []
    for p in range(SWA_KV_HEADS // 2):
        pair = t[:, p * PAIR:(p + 1) * PAIR]
        swapped = pltpu.roll(pair, HEAD_DIM, axis=1)
        zero = jnp.zeros_like(pair)
        los += [jnp.where(low_half, pair, zero), jnp.where(low_half, swapped, zero)]
        his += [jnp.where(low_half, zero, swapped), jnp.where(low_half, zero, pair)]
    return [a.astype(BF16) for a in los], [a.astype(BF16) for a in his]


def _value_head_slabs_t(t):
    los, his = [], []
    for p in range(SWA_KV_HEADS // 2):
        pair_t = t[:, p * PAIR:(p + 1) * PAIR].T
        zero = jnp.zeros((HEAD_DIM, t.shape[0]), F32)
        for head_t in (pair_t[0:HEAD_DIM], pair_t[HEAD_DIM:PAIR]):
            los.append(jnp.concatenate([head_t, zero], axis=0))
            his.append(jnp.concatenate([zero, head_t], axis=0))
    return [a.astype(BF16) for a in los], [a.astype(BF16) for a in his]


def _mix_kernel(x_ref, posq_ref, posk_ref, lut_ref, sink_ref, g_ref, win_ref, convw_ref, wout_ref, o_ref,
                q_s, klo_s, khi_s, vtlo_s, vthi_s, cu_s, merged_s, relm_s, bias_s, *, tiles_per_seq):
    tm = x_ref.shape[0]
    nblk = tm // BLOCK
    step = pl.program_id(0)
    first = (step % tiles_per_seq) == 0

    @pl.when(first)
    def _():
        for s in (klo_s, khi_s):
            s[:, 0:BLOCK, :] = jnp.zeros((SWA_KV_HEADS, BLOCK, PAIR), BF16)
        for s in (vtlo_s, vthi_s):
            s[:, 0] = jnp.zeros((SWA_KV_HEADS, PAIR, BLOCK), BF16)
        cu_s[0:SUBLANES, :] = jnp.zeros((SUBLANES, D_MODEL), F32)

    @pl.when(jnp.logical_not(first))
    def _():
        for s in (klo_s, khi_s):
            s[:, 0:BLOCK, :] = s[:, tm:tm + BLOCK, :]
        for s in (vtlo_s, vthi_s):
            s[:, 0] = s[:, nblk]
        cu_s[0:SUBLANES, :] = cu_s[tm:tm + SUBLANES, :]

    @pl.when(step == 0)
    def _():
        relm_s[...] = jnp.full(relm_s.shape, -2, jnp.int32)

    relms, misses = [], []
    for j in range(nblk):
        pos_k = jnp.concatenate(
            [jnp.broadcast_to(posk_ref[0, j:j + 1, c * BLOCK:(c + 1) * BLOCK], (BLOCK, BLOCK)).T
             for c in range(2)], axis=0)
        rel = posq_ref[0, j:j + 1, :] - pos_k
        relm = jnp.where((rel >= 0) & (rel < WINDOW), rel, -1)
        relms.append(relm)
        misses.append(jnp.sum((relm != relm_s[j]).astype(jnp.int32)))

    for j in range(nblk):
        @pl.when(misses[j] != 0)
        def _(j=j):
            relm_s[j] = relms[j]
            shown = relms[j] >= 0
            idx = jnp.maximum(relms[j], 0)
            for hd in range(SWA_HEADS):
                lut = jnp.broadcast_to(lut_ref[hd:hd + 1, :], (2 * BLOCK, WINDOW))
                bias_s[j, hd] = jnp.where(shown, jnp.take_along_axis(lut, idx, axis=1), NEG_INF)

    x = x_ref[...]
    h = _rms(x, g_ref[...]).astype(BF16)

    q_s[...] = (_dot(h, win_ref[:, OFF_Q:OFF_Q + Q_WIDTH]) * (HEAD_DIM ** -0.5)).astype(BF16)
    k_lo, k_hi = _key_head_slabs(_dot(h, win_ref[:, OFF_K:OFF_K + KV_WIDTH]))
    vt_lo, vt_hi = _value_head_slabs_t(_dot(h, win_ref[:, OFF_V:OFF_V + KV_WIDTH]))
    for g in range(SWA_KV_HEADS):
        klo_s[g, BLOCK:BLOCK + tm, :] = k_lo[g]
        khi_s[g, BLOCK:BLOCK + tm, :] = k_hi[g]
        for j in range(nblk):
            vtlo_s[g, j + 1] = vt_lo[g][:, j * BLOCK:(j + 1) * BLOCK]
            vthi_s[g, j + 1] = vt_hi[g][:, j * BLOCK:(j + 1) * BLOCK]

    def qk(blk, g):
        rows = slice(blk * BLOCK, (blk + 1) * BLOCK)
        keys = slice(blk * BLOCK, (blk + 2) * BLOCK)
        c0 = g * 2 * PAIR
        q2 = jnp.concatenate([q_s[rows, c0:c0 + PAIR], q_s[rows, c0 + PAIR:c0 + 2 * PAIR]], axis=0)
        return _dot_nt(klo_s[g, keys, :], q2), _dot_nt(khi_s[g, keys, :], q2)

    def softmax_pv(blk, g, logits_t):
        probs_t = [[None, None], [None, None]]
        inv = [[None, None], [None, None]]
        for half in range(2):
            for pair in range(2):
                head = g * 4 + pair * 2 + half
                lt = logits_t[half][:, pair * BLOCK:(pair + 1) * BLOCK] + bias_s[blk, head]
                sink = sink_ref[head]
                m = jnp.maximum(jnp.max(lt, axis=0, keepdims=True), sink)
                e = jnp.exp(lt - m)
                inv[half][pair] = 1.0 / (jnp.sum(e, axis=0, keepdims=True) + jnp.exp(sink - m))
                probs_t[half][pair] = e.astype(BF16)
        v_lo_t = jnp.concatenate([vtlo_s[g, blk], vtlo_s[g, blk + 1]], axis=1)
        v_hi_t = jnp.concatenate([vthi_s[g, blk], vthi_s[g, blk + 1]], axis=1)
        out_t = (_dot(v_lo_t, jnp.concatenate(probs_t[0], axis=1))
                 + _dot(v_hi_t, jnp.concatenate(probs_t[1], axis=1)))
        scale = jnp.concatenate(
            [jnp.broadcast_to(jnp.concatenate(inv[0], axis=1), (HEAD_DIM, 2 * BLOCK)),
             jnp.broadcast_to(jnp.concatenate(inv[1], axis=1), (HEAD_DIM, 2 * BLOCK))], axis=0)
        return (out_t * scale).T

    units = [(blk, g) for g in range(SWA_KV_HEADS) for blk in range(nblk)]
    logits_next = qk(*units[0])
    for g in range(SWA_KV_HEADS):
        cols = slice(g * 2 * PAIR, (g + 1) * 2 * PAIR)

        def proj(off):
            return _dot(h, win_ref[:, off + g * 2 * PAIR:off + (g + 1) * 2 * PAIR])

        def conv_taps(_):
            cu = proj(OFF_C) * proj(OFF_U)
            cu_s[SUBLANES:SUBLANES + tm, cols] = cu
            return (convw_ref[0:1, cols] * cu_s[SUBLANES - 2:SUBLANES - 2 + tm, cols]
                    + convw_ref[1:2, cols] * cu_s[SUBLANES - 1:SUBLANES - 1 + tm, cols]
                    + convw_ref[2:3, cols] * cu)

        stages = [conv_taps,
                  lambda conv: proj(OFF_B) * conv,
                  lambda conv: jax.nn.sigmoid(proj(OFF_GC)) * conv]
        attn_blocks = []
        conv = None
        for blk in range(nblk):
            logits_t = logits_next
            nxt = g * nblk + blk + 1
            if nxt < len(units):
                logits_next = qk(*units[nxt])
            if blk < len(stages):
                conv = stages[blk](conv)
            out = softmax_pv(blk, g, logits_t)
            attn_blocks.append(jnp.concatenate([out[0:BLOCK], out[BLOCK:2 * BLOCK]], axis=1))
        for stage in stages[nblk:]:
            conv = stage(conv)
        attn = jnp.concatenate(attn_blocks, axis=0)
        merged_s[:, cols] = (jax.nn.sigmoid(proj(OFF_GA)) * attn + conv).astype(BF16)

    o_ref[...] = x + _dot(merged_s[...], wout_ref[...])


def _mix(x, posq, posk, lut, sinks, gain, w_in, conv_w, w_out, *, seq_len):
    t = x.shape[0]
    tm = TOKEN_TILE
    nblk = tm // BLOCK
    k_scratch = pltpu.VMEM((SWA_KV_HEADS, BLOCK + tm, PAIR), BF16)
    vt_scratch = pltpu.VMEM((SWA_KV_HEADS, nblk + 1, PAIR, BLOCK), BF16)
    return pl.pallas_call(
        functools.partial(_mix_kernel, tiles_per_seq=seq_len // tm),
        grid=(t // tm,),
        in_specs=[
            pl.BlockSpec((tm, D_MODEL), lambda i: (i, 0)),
            pl.BlockSpec((1, nblk, BLOCK), lambda i: (i, 0, 0)),
            pl.BlockSpec((1, nblk, 2 * BLOCK), lambda i: (i, 0, 0)),
            _resident((SWA_HEADS, WINDOW)),
            pl.BlockSpec(memory_space=pltpu.SMEM),
            _resident((1, D_MODEL)),
            _resident((D_MODEL, IN_WIDTH)),
            _resident((CONV_K, D_MODEL)),
            _resident((D_MODEL, D_MODEL)),
        ],
        out_specs=pl.BlockSpec((tm, D_MODEL), lambda i: (i, 0)),
        out_shape=jax.ShapeDtypeStruct((t, D_MODEL), F32),
        scratch_shapes=[
            pltpu.VMEM((tm, Q_WIDTH), BF16),
            k_scratch, k_scratch, vt_scratch, vt_scratch,
            pltpu.VMEM((SUBLANES + tm, D_MODEL), F32),
            pltpu.VMEM((tm, D_MODEL), BF16),
            pltpu.VMEM((nblk, 2 * BLOCK, BLOCK), jnp.int32),
            pltpu.VMEM((nblk, SWA_HEADS, 2 * BLOCK, BLOCK), F32),
        ],
        compiler_params=_params(),
        name="mix",
    )(x, posq, posk, lut, sinks, gain, w_in, conv_w, w_out)


def _mem_kv_kernel(mem_ref, g_ref, wkv_ref, kv_ref):
    h = _rms(mem_ref[...], g_ref[...]).astype(BF16)
    kv_ref[...] = _dot(h, wkv_ref[...]).astype(BF16)


def _mem_kv(mem, gain, w_kv):
    rows = mem.shape[0]
    tm = TOKEN_TILE
    return pl.pallas_call(
        _mem_kv_kernel,
        grid=(rows // tm,),
        in_specs=[
            pl.BlockSpec((tm, D_MODEL), lambda i: (i, 0)),
            _resident((1, D_MODEL)),
            _resident((D_MODEL, 2 * D_MODEL)),
        ],
        out_specs=pl.BlockSpec((tm, 2 * D_MODEL), lambda i: (i, 0)),
        out_shape=jax.ShapeDtypeStruct((rows, 2 * D_MODEL), BF16),
        compiler_params=_params(),
        name="mem_kv",
    )(mem, gain, w_kv)


def _xattn_kernel(x_ref, g_ref, wq_ref, kv_ref, wo_ref, o_ref):
    x = x_ref[...]
    h = _rms(x, g_ref[...]).astype(BF16)
    def head_cols(hd):
        return slice(hd * MEM_HEAD_DIM, (hd + 1) * MEM_HEAD_DIM)

    def q_proj(hd):
        return (_dot(h, wq_ref[:, head_cols(hd)]) * (MEM_HEAD_DIM ** -0.5)).astype(BF16)

    def logits(hd, q):
        return _dot_nt(q, kv_ref[0, :, head_cols(hd)])

    def context(hd, lg):
        e = jnp.exp(lg - jnp.max(lg, axis=1, keepdims=True))
        p = (e / jnp.sum(e, axis=1, keepdims=True)).astype(BF16)
        v = kv_ref[0, :, D_MODEL + hd * MEM_HEAD_DIM:D_MODEL + (hd + 1) * MEM_HEAD_DIM]
        return _dot(p, v).astype(BF16)

    qs = [q_proj(0), q_proj(1)]
    lgs = [logits(0, qs[0])]
    acc = x
    for hd in range(MEM_HEADS):
        if hd + 2 < MEM_HEADS:
            qs.append(q_proj(hd + 2))
        if hd + 1 < MEM_HEADS:
            lgs.append(logits(hd + 1, qs[hd + 1]))
        acc = acc + _dot(context(hd, lgs[hd]), wo_ref[head_cols(hd), :])
    o_ref[...] = acc


def _xattn(x, gain, w_q, kv, w_o, *, seq_len):
    t = x.shape[0]
    tm = TOKEN_TILE
    tiles_per_seq = seq_len // tm
    return pl.pallas_call(
        _xattn_kernel,
        grid=(t // tm,),
        in_specs=[
            pl.BlockSpec((tm, D_MODEL), lambda i: (i, 0)),
            _resident((1, D_MODEL)),
            _resident((D_MODEL, D_MODEL)),
            pl.BlockSpec((1, N_MEM, 2 * D_MODEL), lambda i: (i // tiles_per_seq, 0, 0)),
            _resident((D_MODEL, D_MODEL)),
        ],
        out_specs=pl.BlockSpec((tm, D_MODEL), lambda i: (i, 0)),
        out_shape=jax.ShapeDtypeStruct((t, D_MODEL), F32),
        compiler_params=_params(),
        name="xattn",
    )(x, gain, w_q, kv, w_o)


def kernel(x, mem, positions, rel_bias, ffn1_norm, ffn1_w_gu, ffn1_w_down, mix_norm, w_in, sinks, conv_w, w_out, xattn_norm, mem_norm, xattn_wq, xattn_wkv, xattn_wo, ffn2_norm, ffn2_w_gu, ffn2_w_down, final_norm):
    batch, seq_len, _ = x.shape
    depth = w_in.shape[0]
    tokens = batch * seq_len
    nb = seq_len // BLOCK

    xt = x.reshape(tokens, D_MODEL)
    posq = positions.reshape(tokens // TOKEN_TILE, TOKEN_TILE // BLOCK, BLOCK)
    pos_blocks = positions.reshape(batch, nb, BLOCK)
    prev = jnp.concatenate([jnp.full((batch, 1, BLOCK), POS_PAD, positions.dtype), pos_blocks[:, :-1]], axis=1)
    posk = jnp.concatenate([prev, pos_blocks], axis=2).reshape(tokens // TOKEN_TILE, TOKEN_TILE // BLOCK, 2 * BLOCK)
    lut = _bias_lut(rel_bias.T)
    final_gain = final_norm.reshape(1, D_MODEL)

    for l in range(depth):
        xt = _ffn(xt, ffn1_norm[l].reshape(1, D_MODEL), ffn1_w_gu[l], ffn1_w_down[l],
                  final_gain, final_norm=False)
        xt = _mix(xt, posq, posk, lut, sinks[l], mix_norm[l].reshape(1, D_MODEL),
                  w_in[l], conv_w[l], w_out[l], seq_len=seq_len)
        kv = _mem_kv(mem.reshape(batch * N_MEM, D_MODEL), mem_norm[l].reshape(1, D_MODEL), xattn_wkv[l])
        xt = _xattn(xt, xattn_norm[l].reshape(1, D_MODEL), xattn_wq[l],
                    kv.reshape(batch, N_MEM, 2 * D_MODEL), xattn_wo[l], seq_len=seq_len)
        xt = _ffn(xt, ffn2_norm[l].reshape(1, D_MODEL), ffn2_w_gu[l], ffn2_w_down[l],
                  final_gain, final_norm=(l == depth - 1))
    return xt.reshape(batch, seq_len, D_MODEL)
```

```python
import functools
import math

import jax
import jax.numpy as jnp
from jax import lax
from jax.experimental import pallas as pl
from jax.experimental.pallas import tpu as pltpu

D_MODEL = 1024
D_FF = 2816
N_MEM = 256
MEM_HEADS = 4
MEM_HEAD_DIM = D_MODEL // MEM_HEADS
HEAD_DIM = 64
SWA_HEADS = 16
SWA_KV_HEADS = 4
WINDOW = 128
BLOCK = 128
REL_BUCKETS = 32
REL_MAX_DIST = 128
CONV_K = 3
EPS = 1e-6
NEG_INF = -1e30
POS_PAD = 1 << 30

Q_WIDTH = SWA_HEADS * HEAD_DIM
KV_WIDTH = SWA_KV_HEADS * HEAD_DIM
OFF_Q = 0
OFF_K = OFF_Q + Q_WIDTH
OFF_V = OFF_K + KV_WIDTH
OFF_C = OFF_V + KV_WIDTH
OFF_B = OFF_C + D_MODEL
OFF_U = OFF_B + D_MODEL
OFF_GA = OFF_U + D_MODEL
OFF_GC = OFF_GA + D_MODEL
IN_WIDTH = OFF_GC + D_MODEL

LANES = 128
SUBLANES = 8
MXU_COLS = 256
VMEM_LIMIT_BYTES = 60 * 1024 * 1024

TOKEN_TILE = 512
XATTN_TOKEN_TILE = 1024
FFN_TOKEN_TILE = 512
FF_CHUNK = MXU_COLS
PAIR = 2 * HEAD_DIM

BF16 = jnp.bfloat16
F32 = jnp.float32


def _rms(x, g):
    return x * lax.rsqrt(jnp.mean(x * x, axis=-1, keepdims=True) + EPS) * g


def _dot(a, b):
    return jnp.dot(a.astype(BF16), b.astype(BF16), preferred_element_type=F32)


def _dot_nt(a, b):
    return lax.dot_general(a.astype(BF16), b.astype(BF16), (((1,), (1,)), ((), ())),
                           preferred_element_type=F32)


def _resident(shape):
    return pl.BlockSpec(shape, lambda i: (0,) * len(shape), pipeline_mode=pl.Buffered(1))


def _params():
    return pltpu.CompilerParams(dimension_semantics=("arbitrary",), vmem_limit_bytes=VMEM_LIMIT_BYTES)


def _ffn_kernel(x_ref, g_ref, wgu_ref, wd_ref, fg_ref, o_ref, act_ref, *, final_norm):
    x = x_ref[...]
    h = _rms(x, g_ref[...]).astype(BF16)
    for j in range(D_FF // FF_CHUNK):
        lo = j * FF_CHUNK
        gate = _dot(h, wgu_ref[:, lo:lo + FF_CHUNK])
        up = _dot(h, wgu_ref[:, D_FF + lo:D_FF + lo + FF_CHUNK])
        act_ref[:, lo:lo + FF_CHUNK] = (gate * jax.nn.sigmoid(gate) * up).astype(BF16)
    y = x + 0.5 * _dot(act_ref[...], wd_ref[...])
    if final_norm:
        y = _rms(y, fg_ref[...])
    o_ref[...] = y


def _ffn(x, gain, w_gu, w_down, final_gain, *, final_norm):
    t = x.shape[0]
    tm = FFN_TOKEN_TILE
    return pl.pallas_call(
        functools.partial(_ffn_kernel, final_norm=final_norm),
        grid=(t // tm,),
        in_specs=[
            pl.BlockSpec((tm, D_MODEL), lambda i: (i, 0)),
            _resident((1, D_MODEL)),
            _resident((D_MODEL, 2 * D_FF)),
            _resident((D_FF, D_MODEL)),
            _resident((1, D_MODEL)),
        ],
        out_specs=pl.BlockSpec((tm, D_MODEL), lambda i: (i, 0)),
        out_shape=jax.ShapeDtypeStruct((t, D_MODEL), F32),
        scratch_shapes=[pltpu.VMEM((tm, D_FF), BF16)],
        compiler_params=_params(),
        name="ffn_final" if final_norm else "ffn",
    )(x, gain, w_gu, w_down, final_gain)


def _bias_lut_kernel(tab_ref, lut_ref):
    n = lax.broadcasted_iota(jnp.int32, (SWA_HEADS, WINDOW), 1)
    max_exact = REL_BUCKETS // 2
    nf = jnp.maximum(n, 1).astype(F32)
    large = max_exact + (jnp.log(nf / max_exact) / math.log(REL_MAX_DIST / max_exact)
                         * (REL_BUCKETS - max_exact)).astype(jnp.int32)
    large = jnp.minimum(large, REL_BUCKETS - 1)
    bucket = jnp.where(n < max_exact, n, large)
    lut = jnp.zeros((SWA_HEADS, WINDOW), F32)
    for b in range(REL_BUCKETS):
        lut = jnp.where(bucket == b, tab_ref[:, b:b + 1], lut)
    lut_ref[...] = lut


def _bias_lut(rel_bias_t):
    return pl.pallas_call(
        _bias_lut_kernel,
        out_shape=jax.ShapeDtypeStruct((SWA_HEADS, WINDOW), F32),
        name="bias_lut",
    )(rel_bias_t)


def _key_head_slabs(t):
    lane = lax.broadcasted_iota(jnp.int32, (t.shape[0], PAIR), 1)
    low_half = lane < HEAD_DIM
    los, his = [], []
    for p in range(SWA_KV_HEADS // 2):
        pair = t[:, p * PAIR:(p + 1) * PAIR]
        swapped = pltpu.roll(pair, HEAD_DIM, axis=1)
        zero = jnp.zeros_like(pair)
        los += [jnp.where(low_half, pair, zero), jnp.where(low_half, swapped, zero)]
        his += [jnp.where(low_half, zero, swapped), jnp.where(low_half, zero, pair)]
    return [a.astype(BF16) for a in los], [a.astype(BF16) for a in his]


def _value_head_slabs_t(t):
    los, his = [], []
    for p in range(SWA_KV_HEADS // 2):
        pair_t = t[:, p * PAIR:(p + 1) * PAIR].T
        zero = jnp.zeros((HEAD_DIM, t.shape[0]), F32)
        for head_t in (pair_t[0:HEAD_DIM], pair_t[HEAD_DIM:PAIR]):
            los.append(jnp.concatenate([head_t, zero], axis=0))
            his.append(jnp.concatenate([zero, head_t], axis=0))
    return [a.astype(BF16) for a in los], [a.astype(BF16) for a in his]


def _mix_kernel(x_ref, posq_ref, posk_ref, lut_ref, sink_ref, g_ref, win_ref, convw_ref, wout_ref, o_ref,
                q_s, klo_s, khi_s, vtlo_s, vthi_s, cu_s, merged_s, relm_s, bias_s, *, tiles_per_seq):
    tm = x_ref.shape[0]
    nblk = tm // BLOCK
    step = pl.program_id(0)
    first = (step % tiles_per_seq) == 0

    @pl.when(first)
    def _():
        for s in (klo_s, khi_s):
            s[:, 0:BLOCK, :] = jnp.zeros((SWA_KV_HEADS, BLOCK, PAIR), BF16)
        for s in (vtlo_s, vthi_s):
            s[:, 0] = jnp.zeros((SWA_KV_HEADS, PAIR, BLOCK), BF16)
        cu_s[0:SUBLANES, :] = jnp.zeros((SUBLANES, D_MODEL), F32)

    @pl.when(jnp.logical_not(first))
    def _():
        for s in (klo_s, khi_s):
            s[:, 0:BLOCK, :] = s[:, tm:tm + BLOCK, :]
        for s in (vtlo_s, vthi_s):
            s[:, 0] = s[:, nblk]
        cu_s[0:SUBLANES, :] = cu_s[tm:tm + SUBLANES, :]

    slots = [jnp.where(first, nblk, 0)] + list(range(1, nblk))

    @pl.when(step == 0)
    def _():
        relm_s[...] = jnp.full(relm_s.shape, -2, jnp.int32)

    relms, misses = [], []
    for j in range(nblk):
        pos_k = jnp.concatenate(
            [jnp.broadcast_to(posk_ref[0, j:j + 1, c * BLOCK:(c + 1) * BLOCK], (BLOCK, BLOCK)).T
             for c in range(2)], axis=0)
        rel = posq_ref[0, j:j + 1, :] - pos_k
        relm = jnp.where((rel >= 0) & (rel < WINDOW), rel, -1)
        relms.append(relm)
        misses.append(jnp.sum((relm != relm_s[slots[j]]).astype(jnp.int32)))

    for j in range(nblk):
        @pl.when(misses[j] != 0)
        def _(j=j):
            relm_s[slots[j]] = relms[j]
            shown = relms[j] >= 0
            idx = jnp.maximum(relms[j], 0)
            for hd in range(SWA_HEADS):
                lut = jnp.broadcast_to(lut_ref[hd:hd + 1, :], (2 * BLOCK, WINDOW))
                bias_s[slots[j], hd] = jnp.where(shown, jnp.take_along_axis(lut, idx, axis=1), NEG_INF)

    x = x_ref[...]
    h = _rms(x, g_ref[...]).astype(BF16)

    q_s[...] = (_dot(h, win_ref[:, OFF_Q:OFF_Q + Q_WIDTH]) * (HEAD_DIM ** -0.5)).astype(BF16)
    k_lo, k_hi = _key_head_slabs(_dot(h, win_ref[:, OFF_K:OFF_K + KV_WIDTH]))
    vt_lo, vt_hi = _value_head_slabs_t(_dot(h, win_ref[:, OFF_V:OFF_V + KV_WIDTH]))
    for g in range(SWA_KV_HEADS):
        klo_s[g, BLOCK:BLOCK + tm, :] = k_lo[g]
        khi_s[g, BLOCK:BLOCK + tm, :] = k_hi[g]
        for j in range(nblk):
            vtlo_s[g, j + 1] = vt_lo[g][:, j * BLOCK:(j + 1) * BLOCK]
            vthi_s[g, j + 1] = vt_hi[g][:, j * BLOCK:(j + 1) * BLOCK]

    def qk(blk, g):
        rows = slice(blk * BLOCK, (blk + 1) * BLOCK)
        keys = slice(blk * BLOCK, (blk + 2) * BLOCK)
        c0 = g * 2 * PAIR
        q2 = jnp.concatenate([q_s[rows, c0:c0 + PAIR], q_s[rows, c0 + PAIR:c0 + 2 * PAIR]], axis=0)
        return _dot_nt(klo_s[g, keys, :], q2), _dot_nt(khi_s[g, keys, :], q2)

    def softmax_pv(blk, g, logits_t):
        probs_t = [[None, None], [None, None]]
        inv = [[None, None], [None, None]]
        for half in range(2):
            for pair in range(2):
                head = g * 4 + pair * 2 + half
                lt = logits_t[half][:, pair * BLOCK:(pair + 1) * BLOCK] + bias_s[slots[blk], head]
                sink = sink_ref[head]
                m = jnp.maximum(jnp.max(lt, axis=0, keepdims=True), sink)
                e = jnp.exp(lt - m)
                inv[half][pair] = 1.0 / (jnp.sum(e, axis=0, keepdims=True) + jnp.exp(sink - m))
                probs_t[half][pair] = e.astype(BF16)
        v_lo_t = jnp.concatenate([vtlo_s[g, blk], vtlo_s[g, blk + 1]], axis=1)
        v_hi_t = jnp.concatenate([vthi_s[g, blk], vthi_s[g, blk + 1]], axis=1)
        out_t = (_dot(v_lo_t, jnp.concatenate(probs_t[0], axis=1))
                 + _dot(v_hi_t, jnp.concatenate(probs_t[1], axis=1)))
        scale = jnp.concatenate(
            [jnp.broadcast_to(jnp.concatenate(inv[0], axis=1), (HEAD_DIM, 2 * BLOCK)),
             jnp.broadcast_to(jnp.concatenate(inv[1], axis=1), (HEAD_DIM, 2 * BLOCK))], axis=0)
        return (out_t * scale).T

    units = [(blk, g) for g in range(SWA_KV_HEADS) for blk in range(nblk)]
    logits_next = qk(*units[0])
    for g in range(SWA_KV_HEADS):
        cols = slice(g * 2 * PAIR, (g + 1) * 2 * PAIR)

        def proj(off):
            return _dot(h, win_ref[:, off + g * 2 * PAIR:off + (g + 1) * 2 * PAIR])

        def conv_taps(_):
            cu = proj(OFF_C) * proj(OFF_U)
            cu_s[SUBLANES:SUBLANES + tm, cols] = cu
            return (convw_ref[0:1, cols] * cu_s[SUBLANES - 2:SUBLANES - 2 + tm, cols]
                    + convw_ref[1:2, cols] * cu_s[SUBLANES - 1:SUBLANES - 1 + tm, cols]
                    + convw_ref[2:3, cols] * cu)

        stages = [conv_taps,
                  lambda conv: proj(OFF_B) * conv,
                  lambda conv: jax.nn.sigmoid(proj(OFF_GC)) * conv]
        attn_blocks = []
        conv = None
        for blk in range(nblk):
            logits_t = logits_next
            nxt = g * nblk + blk + 1
            if nxt < len(units):
                logits_next = qk(*units[nxt])
            if blk < len(stages):
                conv = stages[blk](conv)
            out = softmax_pv(blk, g, logits_t)
            attn_blocks.append(jnp.concatenate([out[0:BLOCK], out[BLOCK:2 * BLOCK]], axis=1))
        for stage in stages[nblk:]:
            conv = stage(conv)
        attn = jnp.concatenate(attn_blocks, axis=0)
        merged_s[:, cols] = (jax.nn.sigmoid(proj(OFF_GA)) * attn + conv).astype(BF16)

    o_ref[...] = x + _dot(merged_s[...], wout_ref[...])


def _mix(x, posq, posk, lut, sinks, gain, w_in, conv_w, w_out, *, seq_len):
    t = x.shape[0]
    tm = TOKEN_TILE
    nblk = tm // BLOCK
    k_scratch = pltpu.VMEM((SWA_KV_HEADS, BLOCK + tm, PAIR), BF16)
    vt_scratch = pltpu.VMEM((SWA_KV_HEADS, nblk + 1, PAIR, BLOCK), BF16)
    return pl.pallas_call(
        functools.partial(_mix_kernel, tiles_per_seq=seq_len // tm),
        grid=(t // tm,),
        in_specs=[
            pl.BlockSpec((tm, D_MODEL), lambda i: (i, 0)),
            pl.BlockSpec((1, nblk, BLOCK), lambda i: (i, 0, 0)),
            pl.BlockSpec((1, nblk, 2 * BLOCK), lambda i: (i, 0, 0)),
            _resident((SWA_HEADS, WINDOW)),
            pl.BlockSpec(memory_space=pltpu.SMEM),
            _resident((1, D_MODEL)),
            _resident((D_MODEL, IN_WIDTH)),
            _resident((CONV_K, D_MODEL)),
            _resident((D_MODEL, D_MODEL)),
        ],
        out_specs=pl.BlockSpec((tm, D_MODEL), lambda i: (i, 0)),
        out_shape=jax.ShapeDtypeStruct((t, D_MODEL), F32),
        scratch_shapes=[
            pltpu.VMEM((tm, Q_WIDTH), BF16),
            k_scratch, k_scratch, vt_scratch, vt_scratch,
            pltpu.VMEM((SUBLANES + tm, D_MODEL), F32),
            pltpu.VMEM((tm, D_MODEL), BF16),
            pltpu.VMEM((nblk + 1, 2 * BLOCK, BLOCK), jnp.int32),
            pltpu.VMEM((nblk + 1, SWA_HEADS, 2 * BLOCK, BLOCK), F32),
        ],
        compiler_params=_params(),
        name="mix",
    )(x, posq, posk, lut, sinks, gain, w_in, conv_w, w_out)


def _mem_kv_kernel(mem_ref, g_ref, wkv_ref, kv_ref):
    h = _rms(mem_ref[...], g_ref[...]).astype(BF16)
    kv_ref[...] = _dot(h, wkv_ref[...]).astype(BF16)


def _mem_kv(mem, gain, w_kv):
    rows = mem.shape[0]
    tm = TOKEN_TILE
    return pl.pallas_call(
        _mem_kv_kernel,
        grid=(rows // tm,),
        in_specs=[
            pl.BlockSpec((tm, D_MODEL), lambda i: (i, 0)),
            _resident((1, D_MODEL)),
            _resident((D_MODEL, 2 * D_MODEL)),
        ],
        out_specs=pl.BlockSpec((tm, 2 * D_MODEL), lambda i: (i, 0)),
        out_shape=jax.ShapeDtypeStruct((rows, 2 * D_MODEL), BF16),
        compiler_params=_params(),
        name="mem_kv",
    )(mem, gain, w_kv)


def _xattn_kernel(x_ref, g_ref, wq_ref, kv_ref, wo_ref, o_ref):
    x = x_ref[...]
    h = _rms(x, g_ref[...]).astype(BF16)
    def head_cols(hd):
        return slice(hd * MEM_HEAD_DIM, (hd + 1) * MEM_HEAD_DIM)

    def q_proj(hd):
        return (_dot(h, wq_ref[:, head_cols(hd)]) * (MEM_HEAD_DIM ** -0.5)).astype(BF16)

    def logits(hd, q):
        return _dot_nt(q, kv_ref[0, :, head_cols(hd)])

    def context(hd, lg):
        e = jnp.exp(lg - jnp.max(lg, axis=1, keepdims=True))
        p = (e / jnp.sum(e, axis=1, keepdims=True)).astype(BF16)
        v = kv_ref[0, :, D_MODEL + hd * MEM_HEAD_DIM:D_MODEL + (hd + 1) * MEM_HEAD_DIM]
        return _dot(p, v).astype(BF16)

    qs = [q_proj(0), q_proj(1)]
    lgs = [logits(0, qs[0])]
    acc = x
    for hd in range(MEM_HEADS):
        if hd + 2 < MEM_HEADS:
            qs.append(q_proj(hd + 2))
        if hd + 1 < MEM_HEADS:
            lgs.append(logits(hd + 1, qs[hd + 1]))
        acc = acc + _dot(context(hd, lgs[hd]), wo_ref[head_cols(hd), :])
    o_ref[...] = acc


def _xattn(x, gain, w_q, kv, w_o, *, seq_len):
    t = x.shape[0]
    tm = XATTN_TOKEN_TILE
    tiles_per_seq = seq_len // tm
    return pl.pallas_call(
        _xattn_kernel,
        grid=(t // tm,),
        in_specs=[
            pl.BlockSpec((tm, D_MODEL), lambda i: (i, 0)),
            _resident((1, D_MODEL)),
            _resident((D_MODEL, D_MODEL)),
            pl.BlockSpec((1, N_MEM, 2 * D_MODEL), lambda i: (i // tiles_per_seq, 0, 0)),
            _resident((D_MODEL, D_MODEL)),
        ],
        out_specs=pl.BlockSpec((tm, D_MODEL), lambda i: (i, 0)),
        out_shape=jax.ShapeDtypeStruct((t, D_MODEL), F32),
        compiler_params=_params(),
        name="xattn",
    )(x, gain, w_q, kv, w_o)


def kernel(x, mem, positions, rel_bias, ffn1_norm, ffn1_w_gu, ffn1_w_down, mix_norm, w_in, sinks, conv_w, w_out, xattn_norm, mem_norm, xattn_wq, xattn_wkv, xattn_wo, ffn2_norm, ffn2_w_gu, ffn2_w_down, final_norm):
    batch, seq_len, _ = x.shape
    depth = w_in.shape[0]
    tokens = batch * seq_len
    nb = seq_len // BLOCK

    xt = x.reshape(tokens, D_MODEL)
    posq = positions.reshape(tokens // TOKEN_TILE, TOKEN_TILE // BLOCK, BLOCK)
    pos_blocks = positions.reshape(batch, nb, BLOCK)
    prev = jnp.concatenate([jnp.full((batch, 1, BLOCK), POS_PAD, positions.dtype), pos_blocks[:, :-1]], axis=1)
    posk = jnp.concatenate([prev, pos_blocks], axis=2).reshape(tokens // TOKEN_TILE, TOKEN_TILE // BLOCK, 2 * BLOCK)
    lut = _bias_lut(rel_bias.T)
    final_gain = final_norm.reshape(1, D_MODEL)

    for l in range(depth):
        xt = _ffn(xt, ffn1_norm[l].reshape(1, D_MODEL), ffn1_w_gu[l], ffn1_w_down[l],
                  final_gain, final_norm=False)
        xt = _mix(xt, posq, posk, lut, sinks[l], mix_norm[l].reshape(1, D_MODEL),
                  w_in[l], conv_w[l], w_out[l], seq_len=seq_len)
        kv = _mem_kv(mem.reshape(batch * N_MEM, D_MODEL), mem_norm[l].reshape(1, D_MODEL), xattn_wkv[l])
        xt = _xattn(xt, xattn_norm[l].reshape(1, D_MODEL), xattn_wq[l],
                    kv.reshape(batch, N_MEM, 2 * D_MODEL), xattn_wo[l], seq_len=seq_len)
        xt = _ffn(xt, ffn2_norm[l].reshape(1, D_MODEL), ffn2_w_gu[l], ffn2_w_down[l],
                  final_gain, final_norm=(l == depth - 1))
    return xt.reshape(batch, seq_len, D_MODEL)
```

```python
import functools
import math

import jax
import jax.numpy as jnp
from jax import lax
from jax.experimental import pallas as pl
from jax.experimental.pallas import tpu as pltpu

D_MODEL = 1024
D_FF = 2816
N_MEM = 256
MEM_HEADS = 4
MEM_HEAD_DIM = D_MODEL // MEM_HEADS
HEAD_DIM = 64
SWA_HEADS = 16
SWA_KV_HEADS = 4
WINDOW = 128
BLOCK = 128
REL_BUCKETS = 32
REL_MAX_DIST = 128
CONV_K = 3
EPS = 1e-6
NEG_INF = -1e30
POS_PAD = 1 << 30

Q_WIDTH = SWA_HEADS * HEAD_DIM
KV_WIDTH = SWA_KV_HEADS * HEAD_DIM
OFF_Q = 0
OFF_K = OFF_Q + Q_WIDTH
OFF_V = OFF_K + KV_WIDTH
OFF_C = OFF_V + KV_WIDTH
OFF_B = OFF_C + D_MODEL
OFF_U = OFF_B + D_MODEL
OFF_GA = OFF_U + D_MODEL
OFF_GC = OFF_GA + D_MODEL
IN_WIDTH = OFF_GC + D_MODEL

LANES = 128
SUBLANES = 8
MXU_COLS = 256
VMEM_LIMIT_BYTES = 60 * 1024 * 1024

TOKEN_TILE = 512
XATTN_TOKEN_TILE = 1024
FFN_TOKEN_TILE = 512
FF_CHUNK = MXU_COLS
PAIR = 2 * HEAD_DIM

BF16 = jnp.bfloat16
F32 = jnp.float32


def _rms(x, g):
    return x * lax.rsqrt(jnp.mean(x * x, axis=-1, keepdims=True) + EPS) * g


def _dot(a, b):
    return jnp.dot(a.astype(BF16), b.astype(BF16), preferred_element_type=F32)


def _dot_nt(a, b):
    return lax.dot_general(a.astype(BF16), b.astype(BF16), (((1,), (1,)), ((), ())),
                           preferred_element_type=F32)


def _resident(shape):
    return pl.BlockSpec(shape, lambda i: (0,) * len(shape), pipeline_mode=pl.Buffered(1))


def _params():
    return pltpu.CompilerParams(dimension_semantics=("arbitrary",), vmem_limit_bytes=VMEM_LIMIT_BYTES)


def _ffn_kernel(x_ref, g_ref, wgu_ref, wd_ref, fg_ref, o_ref, act_ref, *, final_norm):
    x = x_ref[...]
    h = _rms(x, g_ref[...]).astype(BF16)
    for j in range(D_FF // FF_CHUNK):
        lo = j * FF_CHUNK
        gate = _dot(h, wgu_ref[:, lo:lo + FF_CHUNK])
        up = _dot(h, wgu_ref[:, D_FF + lo:D_FF + lo + FF_CHUNK])
        act_ref[:, lo:lo + FF_CHUNK] = (gate * jax.nn.sigmoid(gate) * up).astype(BF16)
    y = x + 0.5 * _dot(act_ref[...], wd_ref[...])
    if final_norm:
        y = _rms(y, fg_ref[...])
    o_ref[...] = y


def _ffn(x, gain, w_gu, w_down, final_gain, *, final_norm):
    t = x.shape[0]
    tm = FFN_TOKEN_TILE
    return pl.pallas_call(
        functools.partial(_ffn_kernel, final_norm=final_norm),
        grid=(t // tm,),
        in_specs=[
            pl.BlockSpec((tm, D_MODEL), lambda i: (i, 0)),
            _resident((1, D_MODEL)),
            _resident((D_MODEL, 2 * D_FF)),
            _resident((D_FF, D_MODEL)),
            _resident((1, D_MODEL)),
        ],
        out_specs=pl.BlockSpec((tm, D_MODEL), lambda i: (i, 0)),
        out_shape=jax.ShapeDtypeStruct((t, D_MODEL), F32),
        scratch_shapes=[pltpu.VMEM((tm, D_FF), BF16)],
        compiler_params=_params(),
        name="ffn_final" if final_norm else "ffn",
    )(x, gain, w_gu, w_down, final_gain)


def _bias_lut_kernel(tab_ref, lut_ref):
    n = lax.broadcasted_iota(jnp.int32, (SWA_HEADS, WINDOW), 1)
    max_exact = REL_BUCKETS // 2
    nf = jnp.maximum(n, 1).astype(F32)
    large = max_exact + (jnp.log(nf / max_exact) / math.log(REL_MAX_DIST / max_exact)
                         * (REL_BUCKETS - max_exact)).astype(jnp.int32)
    large = jnp.minimum(large, REL_BUCKETS - 1)
    bucket = jnp.where(n < max_exact, n, large)
    lut = jnp.zeros((SWA_HEADS, WINDOW), F32)
    for b in range(REL_BUCKETS):
        lut = jnp.where(bucket == b, tab_ref[:, b:b + 1], lut)
    lut_ref[...] = lut


def _bias_lut(rel_bias_t):
    return pl.pallas_call(
        _bias_lut_kernel,
        out_shape=jax.ShapeDtypeStruct((SWA_HEADS, WINDOW), F32),
        name="bias_lut",
    )(rel_bias_t)


def _key_head_slabs(t):
    lane = lax.broadcasted_iota(jnp.int32, (t.shape[0], PAIR), 1)
    low_half = lane < HEAD_DIM
    los, his = [], []
    for p in range(SWA_KV_HEADS // 2):
        pair = t[:, p * PAIR:(p + 1) * PAIR]
        swapped = pltpu.roll(pair, HEAD_DIM, axis=1)
        zero = jnp.zeros_like(pair)
        los += [jnp.where(low_half, pair, zero), jnp.where(low_half, swapped, zero)]
        his += [jnp.where(low_half, zero, swapped), jnp.where(low_half, zero, pair)]
    return [a.astype(BF16) for a in los], [a.astype(BF16) for a in his]


def _value_head_slabs_t(t):
    los, his = [], []
    for p in range(SWA_KV_HEADS // 2):
        pair_t = t[:, p * PAIR:(p + 1) * PAIR].T
        zero = jnp.zeros((HEAD_DIM, t.shape[0]), F32)
        for head_t in (pair_t[0:HEAD_DIM], pair_t[HEAD_DIM:PAIR]):
            los.append(jnp.concatenate([head_t, zero], axis=0))
            his.append(jnp.concatenate([zero, head_t], axis=0))
    return [a.astype(BF16) for a in los], [a.astype(BF16) for a in his]


def _mix_kernel(x_ref, posq_ref, posk_ref, lut_ref, sink_ref, g_ref, win_ref, convw_ref, wout_ref, o_ref,
                q_s, klo_s, khi_s, vtlo_s, vthi_s, h_s, cu_s, merged_s, relm_s, bias_s, *, tiles_per_seq):
    tm = x_ref.shape[0]
    nblk = tm // BLOCK
    step = pl.program_id(0)
    first = (step % tiles_per_seq) == 0

    @pl.when(first)
    def _():
        for s in (klo_s, khi_s):
            s[:, 0:BLOCK, :] = jnp.zeros((SWA_KV_HEADS, BLOCK, PAIR), BF16)
        for s in (vtlo_s, vthi_s):
            s[:, 0] = jnp.zeros((SWA_KV_HEADS, PAIR, BLOCK), BF16)
        cu_s[0:SUBLANES, :] = jnp.zeros((SUBLANES, D_MODEL), F32)

    @pl.when(jnp.logical_not(first))
    def _():
        for s in (klo_s, khi_s):
            s[:, 0:BLOCK, :] = s[:, tm:tm + BLOCK, :]
        for s in (vtlo_s, vthi_s):
            s[:, 0] = s[:, nblk]
        cu_s[0:SUBLANES, :] = cu_s[tm:tm + SUBLANES, :]

    @pl.when(step == 0)
    def _():
        relm_s[...] = jnp.full(relm_s.shape, -2, jnp.int32)

    slots = [jnp.where(first, nblk, 0)] + list(range(1, nblk))

    relms, misses = [], []
    for j in range(nblk):
        pos_k = jnp.concatenate(
            [jnp.broadcast_to(posk_ref[0, j:j + 1, c * BLOCK:(c + 1) * BLOCK], (BLOCK, BLOCK)).T
             for c in range(2)], axis=0)
        rel = posq_ref[0, j:j + 1, :] - pos_k
        relm = jnp.where((rel >= 0) & (rel < WINDOW), rel, -1)
        relms.append(relm)
        misses.append(jnp.sum((relm != relm_s[slots[j]]).astype(jnp.int32)))

    h_s[...] = _rms(x_ref[...], g_ref[...]).astype(BF16)
    h = h_s[...]

    k_lo, k_hi = _key_head_slabs(_dot(h, win_ref[:, OFF_K:OFF_K + KV_WIDTH]))
    vt_lo, vt_hi = _value_head_slabs_t(_dot(h, win_ref[:, OFF_V:OFF_V + KV_WIDTH]))
    for g in range(SWA_KV_HEADS):
        klo_s[g, BLOCK:BLOCK + tm, :] = k_lo[g]
        khi_s[g, BLOCK:BLOCK + tm, :] = k_hi[g]
        for j in range(nblk):
            vtlo_s[g, j + 1] = vt_lo[g][:, j * BLOCK:(j + 1) * BLOCK]
            vthi_s[g, j + 1] = vt_hi[g][:, j * BLOCK:(j + 1) * BLOCK]
    q_s[...] = (_dot(h, win_ref[:, OFF_Q:OFF_Q + Q_WIDTH]) * (HEAD_DIM ** -0.5)).astype(BF16)

    for j in range(nblk):
        @pl.when(misses[j] != 0)
        def _(j=j):
            relm_s[slots[j]] = relms[j]
            shown = relms[j] >= 0
            idx = jnp.maximum(relms[j], 0)
            for hd in range(SWA_HEADS):
                lut = jnp.broadcast_to(lut_ref[hd:hd + 1, :], (2 * BLOCK, WINDOW))
                bias_s[slots[j], hd] = jnp.where(shown, jnp.take_along_axis(lut, idx, axis=1), NEG_INF)

    h = h_s[...]

    def qk(blk, g):
        rows = slice(blk * BLOCK, (blk + 1) * BLOCK)
        keys = slice(blk * BLOCK, (blk + 2) * BLOCK)
        c0 = g * 2 * PAIR
        q2 = jnp.concatenate([q_s[rows, c0:c0 + PAIR], q_s[rows, c0 + PAIR:c0 + 2 * PAIR]], axis=0)
        return _dot_nt(klo_s[g, keys, :], q2), _dot_nt(khi_s[g, keys, :], q2)

    def softmax_pv(blk, g, logits_t):
        probs_t = [[None, None], [None, None]]
        inv = [[None, None], [None, None]]
        for half in range(2):
            for pair in range(2):
                head = g * 4 + pair * 2 + half
                lt = logits_t[half][:, pair * BLOCK:(pair + 1) * BLOCK] + bias_s[slots[blk], head]
                sink = sink_ref[head]
                m = jnp.maximum(jnp.max(lt, axis=0, keepdims=True), sink)
                e = jnp.exp(lt - m)
                inv[half][pair] = 1.0 / (jnp.sum(e, axis=0, keepdims=True) + jnp.exp(sink - m))
                probs_t[half][pair] = e.astype(BF16)
        v_lo_t = jnp.concatenate([vtlo_s[g, blk], vtlo_s[g, blk + 1]], axis=1)
        v_hi_t = jnp.concatenate([vthi_s[g, blk], vthi_s[g, blk + 1]], axis=1)
        out_t = (_dot(v_lo_t, jnp.concatenate(probs_t[0], axis=1))
                 + _dot(v_hi_t, jnp.concatenate(probs_t[1], axis=1)))
        scale = jnp.concatenate(
            [jnp.broadcast_to(jnp.concatenate(inv[0], axis=1), (HEAD_DIM, 2 * BLOCK)),
             jnp.broadcast_to(jnp.concatenate(inv[1], axis=1), (HEAD_DIM, 2 * BLOCK))], axis=0)
        return (out_t * scale).T

    units = [(blk, g) for g in range(SWA_KV_HEADS) for blk in range(nblk)]
    logits_next = qk(*units[0])
    for g in range(SWA_KV_HEADS):
        cols = slice(g * 2 * PAIR, (g + 1) * 2 * PAIR)

        def proj(off):
            return _dot(h, win_ref[:, off + g * 2 * PAIR:off + (g + 1) * 2 * PAIR])

        def conv_taps(_):
            cu = proj(OFF_C) * proj(OFF_U)
            cu_s[SUBLANES:SUBLANES + tm, cols] = cu
            return (convw_ref[0:1, cols] * cu_s[SUBLANES - 2:SUBLANES - 2 + tm, cols]
                    + convw_ref[1:2, cols] * cu_s[SUBLANES - 1:SUBLANES - 1 + tm, cols]
                    + convw_ref[2:3, cols] * cu)

        stages = [conv_taps,
                  lambda conv: proj(OFF_B) * conv,
                  lambda conv: jax.nn.sigmoid(proj(OFF_GC)) * conv]
        attn_blocks = []
        conv = None
        for blk in range(nblk):
            logits_t = logits_next
            nxt = g * nblk + blk + 1
            if nxt < len(units):
                logits_next = qk(*units[nxt])
            if blk < len(stages):
                conv = stages[blk](conv)
            out = softmax_pv(blk, g, logits_t)
            attn_blocks.append(jnp.concatenate([out[0:BLOCK], out[BLOCK:2 * BLOCK]], axis=1))
        for stage in stages[nblk:]:
            conv = stage(conv)
        attn = jnp.concatenate(attn_blocks, axis=0)
        merged_s[:, cols] = (jax.nn.sigmoid(proj(OFF_GA)) * attn + conv).astype(BF16)

    o_ref[...] = x_ref[...] + _dot(merged_s[...], wout_ref[...])


def _mix(x, posq, posk, lut, sinks, gain, w_in, conv_w, w_out, *, seq_len):
    t = x.shape[0]
    tm = TOKEN_TILE
    nblk = tm // BLOCK
    k_scratch = pltpu.VMEM((SWA_KV_HEADS, BLOCK + tm, PAIR), BF16)
    vt_scratch = pltpu.VMEM((SWA_KV_HEADS, nblk + 1, PAIR, BLOCK), BF16)
    return pl.pallas_call(
        functools.partial(_mix_kernel, tiles_per_seq=seq_len // tm),
        grid=(t // tm,),
        in_specs=[
            pl.BlockSpec((tm, D_MODEL), lambda i: (i, 0)),
            pl.BlockSpec((1, nblk, BLOCK), lambda i: (i, 0, 0)),
            pl.BlockSpec((1, nblk, 2 * BLOCK), lambda i: (i, 0, 0)),
            _resident((SWA_HEADS, WINDOW)),
            pl.BlockSpec(memory_space=pltpu.SMEM),
            _resident((1, D_MODEL)),
            _resident((D_MODEL, IN_WIDTH)),
            _resident((CONV_K, D_MODEL)),
            _resident((D_MODEL, D_MODEL)),
        ],
        out_specs=pl.BlockSpec((tm, D_MODEL), lambda i: (i, 0)),
        out_shape=jax.ShapeDtypeStruct((t, D_MODEL), F32),
        scratch_shapes=[
            pltpu.VMEM((tm, Q_WIDTH), BF16),
            k_scratch, k_scratch, vt_scratch, vt_scratch,
            pltpu.VMEM((tm, D_MODEL), BF16),
            pltpu.VMEM((SUBLANES + tm, D_MODEL), F32),
            pltpu.VMEM((tm, D_MODEL), BF16),
            pltpu.VMEM((nblk + 1, 2 * BLOCK, BLOCK), jnp.int32),
            pltpu.VMEM((nblk + 1, SWA_HEADS, 2 * BLOCK, BLOCK), F32),
        ],
        compiler_params=_params(),
        name="mix",
    )(x, posq, posk, lut, sinks, gain, w_in, conv_w, w_out)


def _mem_kv_kernel(mem_ref, g_ref, wkv_ref, kv_ref):
    h = _rms(mem_ref[...], g_ref[...]).astype(BF16)
    kv_ref[...] = _dot(h, wkv_ref[...]).astype(BF16)


def _mem_kv(mem, gain, w_kv):
    rows = mem.shape[0]
    tm = TOKEN_TILE
    return pl.pallas_call(
        _mem_kv_kernel,
        grid=(rows // tm,),
        in_specs=[
            pl.BlockSpec((tm, D_MODEL), lambda i: (i, 0)),
            _resident((1, D_MODEL)),
            _resident((D_MODEL, 2 * D_MODEL)),
        ],
        out_specs=pl.BlockSpec((tm, 2 * D_MODEL), lambda i: (i, 0)),
        out_shape=jax.ShapeDtypeStruct((rows, 2 * D_MODEL), BF16),
        compiler_params=_params(),
        name="mem_kv",
    )(mem, gain, w_kv)


def _xattn_kernel(x_ref, g_ref, wq_ref, kv_ref, wo_ref, o_ref):
    x = x_ref[...]
    h = _rms(x, g_ref[...]).astype(BF16)
    def head_cols(hd):
        return slice(hd * MEM_HEAD_DIM, (hd + 1) * MEM_HEAD_DIM)

    def q_proj(hd):
        return (_dot(h, wq_ref[:, head_cols(hd)]) * (MEM_HEAD_DIM ** -0.5)).astype(BF16)

    def logits(hd, q):
        return _dot_nt(q, kv_ref[0, :, head_cols(hd)])

    def context(hd, lg):
        e = jnp.exp(lg - jnp.max(lg, axis=1, keepdims=True))
        p = (e / jnp.sum(e, axis=1, keepdims=True)).astype(BF16)
        v = kv_ref[0, :, D_MODEL + hd * MEM_HEAD_DIM:D_MODEL + (hd + 1) * MEM_HEAD_DIM]
        return _dot(p, v).astype(BF16)

    qs = [q_proj(0), q_proj(1)]
    lgs = [logits(0, qs[0])]
    acc = x
    for hd in range(MEM_HEADS):
        if hd + 2 < MEM_HEADS:
            qs.append(q_proj(hd + 2))
        if hd + 1 < MEM_HEADS:
            lgs.append(logits(hd + 1, qs[hd + 1]))
        acc = acc + _dot(context(hd, lgs[hd]), wo_ref[head_cols(hd), :])
    o_ref[...] = acc


def _xattn(x, gain, w_q, kv, w_o, *, seq_len):
    t = x.shape[0]
    tm = XATTN_TOKEN_TILE
    tiles_per_seq = seq_len // tm
    return pl.pallas_call(
        _xattn_kernel,
        grid=(t // tm,),
        in_specs=[
            pl.BlockSpec((tm, D_MODEL), lambda i: (i, 0)),
            _resident((1, D_MODEL)),
            _resident((D_MODEL, D_MODEL)),
            pl.BlockSpec((1, N_MEM, 2 * D_MODEL), lambda i: (i // tiles_per_seq, 0, 0)),
            _resident((D_MODEL, D_MODEL)),
        ],
        out_specs=pl.BlockSpec((tm, D_MODEL), lambda i: (i, 0)),
        out_shape=jax.ShapeDtypeStruct((t, D_MODEL), F32),
        compiler_params=_params(),
        name="xattn",
    )(x, gain, w_q, kv, w_o)


def kernel(x, mem, positions, rel_bias, ffn1_norm, ffn1_w_gu, ffn1_w_down, mix_norm, w_in, sinks, conv_w, w_out, xattn_norm, mem_norm, xattn_wq, xattn_wkv, xattn_wo, ffn2_norm, ffn2_w_gu, ffn2_w_down, final_norm):
    batch, seq_len, _ = x.shape
    depth = w_in.shape[0]
    tokens = batch * seq_len
    nb = seq_len // BLOCK

    xt = x.reshape(tokens, D_MODEL)
    posq = positions.reshape(tokens // TOKEN_TILE, TOKEN_TILE // BLOCK, BLOCK)
    pos_blocks = positions.reshape(batch, nb, BLOCK)
    prev = jnp.concatenate([jnp.full((batch, 1, BLOCK), POS_PAD, positions.dtype), pos_blocks[:, :-1]], axis=1)
    posk = jnp.concatenate([prev, pos_blocks], axis=2).reshape(tokens // TOKEN_TILE, TOKEN_TILE // BLOCK, 2 * BLOCK)
    lut = _bias_lut(rel_bias.T)
    final_gain = final_norm.reshape(1, D_MODEL)

    for l in range(depth):
        xt = _ffn(xt, ffn1_norm[l].reshape(1, D_MODEL), ffn1_w_gu[l], ffn1_w_down[l],
                  final_gain, final_norm=False)
        xt = _mix(xt, posq, posk, lut, sinks[l], mix_norm[l].reshape(1, D_MODEL),
                  w_in[l], conv_w[l], w_out[l], seq_len=seq_len)
        kv = _mem_kv(mem.reshape(batch * N_MEM, D_MODEL), mem_norm[l].reshape(1, D_MODEL), xattn_wkv[l])
        xt = _xattn(xt, xattn_norm[l].reshape(1, D_MODEL), xattn_wq[l],
                    kv.reshape(batch, N_MEM, 2 * D_MODEL), xattn_wo[l], seq_len=seq_len)
        xt = _ffn(xt, ffn2_norm[l].reshape(1, D_MODEL), ffn2_w_gu[l], ffn2_w_down[l],
                  final_gain, final_norm=(l == depth - 1))
    return xt.reshape(batch, seq_len, D_MODEL)
```

```python
import functools
import math

import jax
import jax.numpy as jnp
from jax import lax
from jax.experimental import pallas as pl
from jax.experimental.pallas import tpu as pltpu

D_MODEL = 1024
D_FF = 2816
N_MEM = 256
MEM_HEADS = 4
MEM_HEAD_DIM = D_MODEL // MEM_HEADS
HEAD_DIM = 64
SWA_HEADS = 16
SWA_KV_HEADS = 4
WINDOW = 128
BLOCK = 128
REL_BUCKETS = 32
REL_MAX_DIST = 128
CONV_K = 3
EPS = 1e-6
NEG_INF = -1e30
POS_PAD = 1 << 30

Q_WIDTH = SWA_HEADS * HEAD_DIM
KV_WIDTH = SWA_KV_HEADS * HEAD_DIM
OFF_Q = 0
OFF_K = OFF_Q + Q_WIDTH
OFF_V = OFF_K + KV_WIDTH
OFF_C = OFF_V + KV_WIDTH
OFF_B = OFF_C + D_MODEL
OFF_U = OFF_B + D_MODEL
OFF_GA = OFF_U + D_MODEL
OFF_GC = OFF_GA + D_MODEL
IN_WIDTH = OFF_GC + D_MODEL

LANES = 128
SUBLANES = 8
MXU_COLS = 256
VMEM_LIMIT_BYTES = 60 * 1024 * 1024

TOKEN_TILE = 512
XATTN_TOKEN_TILE = 1024
FFN_TOKEN_TILE = 512
FF_CHUNK = MXU_COLS
PAIR = 2 * HEAD_DIM

BF16 = jnp.bfloat16
F32 = jnp.float32


def _rms(x, g):
    return x * lax.rsqrt(jnp.mean(x * x, axis=-1, keepdims=True) + EPS) * g


def _dot(a, b):
    return jnp.dot(a.astype(BF16), b.astype(BF16), preferred_element_type=F32)


def _dot_nt(a, b):
    return lax.dot_general(a.astype(BF16), b.astype(BF16), (((1,), (1,)), ((), ())),
                           preferred_element_type=F32)


def _resident(shape):
    return pl.BlockSpec(shape, lambda i: (0,) * len(shape), pipeline_mode=pl.Buffered(1))


def _params():
    return pltpu.CompilerParams(dimension_semantics=("arbitrary",), vmem_limit_bytes=VMEM_LIMIT_BYTES)


FF_CHUNKS = D_FF // FF_CHUNK


def _ffn_kernel(x_ref, g_ref, wgu_hbm, wd_hbm, fg_ref, o_ref, wgu_v, wd_v, act_ref, sems, *, final_norm):
    def gate_copy(j):
        cols = pl.ds(j * FF_CHUNK, FF_CHUNK)
        return pltpu.make_async_copy(wgu_hbm.at[:, cols], wgu_v.at[:, cols], sems.at[j])

    def up_copy(j):
        cols = pl.ds(D_FF + j * FF_CHUNK, FF_CHUNK)
        return pltpu.make_async_copy(wgu_hbm.at[:, cols], wgu_v.at[:, cols], sems.at[FF_CHUNKS + j])

    def down_copy(j):
        rows = pl.ds(j * FF_CHUNK, FF_CHUNK)
        return pltpu.make_async_copy(wd_hbm.at[rows, :], wd_v.at[rows, :], sems.at[2 * FF_CHUNKS + j])

    def body(wait_for_weights):
        x = x_ref[...]
        h = _rms(x, g_ref[...]).astype(BF16)
        for j in range(FF_CHUNKS):
            lo = j * FF_CHUNK
            if wait_for_weights:
                gate_copy(j).wait()
                up_copy(j).wait()
            gate = _dot(h, wgu_v[:, lo:lo + FF_CHUNK])
            up = _dot(h, wgu_v[:, D_FF + lo:D_FF + lo + FF_CHUNK])
            act_ref[:, lo:lo + FF_CHUNK] = (gate * jax.nn.sigmoid(gate) * up).astype(BF16)
        if wait_for_weights:
            for j in range(FF_CHUNKS):
                down_copy(j).wait()
        y = x + 0.5 * _dot(act_ref[...], wd_v[...])
        if final_norm:
            y = _rms(y, fg_ref[...])
        o_ref[...] = y

    first_step = pl.program_id(0) == 0

    @pl.when(first_step)
    def _():
        for j in range(FF_CHUNKS):
            gate_copy(j).start()
            up_copy(j).start()
        for j in range(FF_CHUNKS):
            down_copy(j).start()
        body(True)

    @pl.when(jnp.logical_not(first_step))
    def _():
        body(False)


def _ffn(x, gain, w_gu, w_down, final_gain, *, final_norm):
    t = x.shape[0]
    tm = FFN_TOKEN_TILE
    return pl.pallas_call(
        functools.partial(_ffn_kernel, final_norm=final_norm),
        grid=(t // tm,),
        in_specs=[
            pl.BlockSpec((tm, D_MODEL), lambda i: (i, 0)),
            _resident((1, D_MODEL)),
            pl.BlockSpec(memory_space=pl.ANY),
            pl.BlockSpec(memory_space=pl.ANY),
            _resident((1, D_MODEL)),
        ],
        out_specs=pl.BlockSpec((tm, D_MODEL), lambda i: (i, 0)),
        out_shape=jax.ShapeDtypeStruct((t, D_MODEL), F32),
        scratch_shapes=[
            pltpu.VMEM((D_MODEL, 2 * D_FF), F32),
            pltpu.VMEM((D_FF, D_MODEL), F32),
            pltpu.VMEM((tm, D_FF), BF16),
            pltpu.SemaphoreType.DMA((3 * FF_CHUNKS,)),
        ],
        compiler_params=_params(),
        name="ffn_final" if final_norm else "ffn",
    )(x, gain, w_gu, w_down, final_gain)


def _bias_lut_kernel(tab_ref, lut_ref):
    n = lax.broadcasted_iota(jnp.int32, (SWA_HEADS, WINDOW), 1)
    max_exact = REL_BUCKETS // 2
    nf = jnp.maximum(n, 1).astype(F32)
    large = max_exact + (jnp.log(nf / max_exact) / math.log(REL_MAX_DIST / max_exact)
                         * (REL_BUCKETS - max_exact)).astype(jnp.int32)
    large = jnp.minimum(large, REL_BUCKETS - 1)
    bucket = jnp.where(n < max_exact, n, large)
    lut = jnp.zeros((SWA_HEADS, WINDOW), F32)
    for b in range(REL_BUCKETS):
        lut = jnp.where(bucket == b, tab_ref[:, b:b + 1], lut)
    lut_ref[...] = lut


def _bias_lut(rel_bias_t):
    return pl.pallas_call(
        _bias_lut_kernel,
        out_shape=jax.ShapeDtypeStruct((SWA_HEADS, WINDOW), F32),
        name="bias_lut",
    )(rel_bias_t)


def _key_head_slabs(t):
    lane = lax.broadcasted_iota(jnp.int32, (t.shape[0], PAIR), 1)
    low_half = lane < HEAD_DIM
    los, his = [], []
    for p in range(SWA_KV_HEADS // 2):
        pair = t[:, p * PAIR:(p + 1) * PAIR]
        swapped = pltpu.roll(pair, HEAD_DIM, axis=1)
        zero = jnp.zeros_like(pair)
        los += [jnp.where(low_half, pair, zero), jnp.where(low_half, swapped, zero)]
        his += [jnp.where(low_half, zero, swapped), jnp.where(low_half, zero, pair)]
    return [a.astype(BF16) for a in los], [a.astype(BF16) for a in his]


def _value_head_slabs_t(t):
    los, his = [], []
    for p in range(SWA_KV_HEADS // 2):
        pair_t = t[:, p * PAIR:(p + 1) * PAIR].T
        zero = jnp.zeros((HEAD_DIM, t.shape[0]), F32)
        for head_t in (pair_t[0:HEAD_DIM], pair_t[HEAD_DIM:PAIR]):
            los.append(jnp.concatenate([head_t, zero], axis=0))
            his.append(jnp.concatenate([zero, head_t], axis=0))
    return [a.astype(BF16) for a in los], [a.astype(BF16) for a in his]


def _mix_kernel(x_ref, posq_ref, posk_ref, lut_ref, sink_ref, g_ref, win_ref, convw_ref, wout_ref, o_ref,
                q_s, klo_s, khi_s, vtlo_s, vthi_s, h_s, cu_s, merged_s, relm_s, bias_s, *, tiles_per_seq):
    tm = x_ref.shape[0]
    nblk = tm // BLOCK
    step = pl.program_id(0)
    first = (step % tiles_per_seq) == 0

    @pl.when(first)
    def _():
        for s in (klo_s, khi_s):
            s[:, 0:BLOCK, :] = jnp.zeros((SWA_KV_HEADS, BLOCK, PAIR), BF16)
        for s in (vtlo_s, vthi_s):
            s[:, 0] = jnp.zeros((SWA_KV_HEADS, PAIR, BLOCK), BF16)
        cu_s[0:SUBLANES, :] = jnp.zeros((SUBLANES, D_MODEL), F32)

    @pl.when(jnp.logical_not(first))
    def _():
        for s in (klo_s, khi_s):
            s[:, 0:BLOCK, :] = s[:, tm:tm + BLOCK, :]
        for s in (vtlo_s, vthi_s):
            s[:, 0] = s[:, nblk]
        cu_s[0:SUBLANES, :] = cu_s[tm:tm + SUBLANES, :]

    @pl.when(step == 0)
    def _():
        relm_s[...] = jnp.full(relm_s.shape, -2, jnp.int32)

    slots = [jnp.where(first, nblk, 0)] + list(range(1, nblk))

    relms, misses = [], []
    for j in range(nblk):
        pos_k = jnp.concatenate(
            [jnp.broadcast_to(posk_ref[0, j:j + 1, c * BLOCK:(c + 1) * BLOCK], (BLOCK, BLOCK)).T
             for c in range(2)], axis=0)
        rel = posq_ref[0, j:j + 1, :] - pos_k
        relm = jnp.where((rel >= 0) & (rel < WINDOW), rel, -1)
        relms.append(relm)
        misses.append(jnp.sum((relm != relm_s[slots[j]]).astype(jnp.int32)))

    h_s[...] = _rms(x_ref[...], g_ref[...]).astype(BF16)
    h = h_s[...]

    k_lo, k_hi = _key_head_slabs(_dot(h, win_ref[:, OFF_K:OFF_K + KV_WIDTH]))
    vt_lo, vt_hi = _value_head_slabs_t(_dot(h, win_ref[:, OFF_V:OFF_V + KV_WIDTH]))
    for g in range(SWA_KV_HEADS):
        klo_s[g, BLOCK:BLOCK + tm, :] = k_lo[g]
        khi_s[g, BLOCK:BLOCK + tm, :] = k_hi[g]
        for j in range(nblk):
            vtlo_s[g, j + 1] = vt_lo[g][:, j * BLOCK:(j + 1) * BLOCK]
            vthi_s[g, j + 1] = vt_hi[g][:, j * BLOCK:(j + 1) * BLOCK]
    q_s[...] = (_dot(h, win_ref[:, OFF_Q:OFF_Q + Q_WIDTH]) * (HEAD_DIM ** -0.5)).astype(BF16)

    for j in range(nblk):
        @pl.when(misses[j] != 0)
        def _(j=j):
            relm_s[slots[j]] = relms[j]
            shown = relms[j] >= 0
            idx = jnp.maximum(relms[j], 0)
            for hd in range(SWA_HEADS):
                lut = jnp.broadcast_to(lut_ref[hd:hd + 1, :], (2 * BLOCK, WINDOW))
                bias_s[slots[j], hd] = jnp.where(shown, jnp.take_along_axis(lut, idx, axis=1), NEG_INF)

    h = h_s[...]

    def qk(blk, g):
        rows = slice(blk * BLOCK, (blk + 1) * BLOCK)
        keys = slice(blk * BLOCK, (blk + 2) * BLOCK)
        c0 = g * 2 * PAIR
        q2 = jnp.concatenate([q_s[rows, c0:c0 + PAIR], q_s[rows, c0 + PAIR:c0 + 2 * PAIR]], axis=0)
        return _dot_nt(klo_s[g, keys, :], q2), _dot_nt(khi_s[g, keys, :], q2)

    def softmax_pv(blk, g, logits_t):
        probs_t = [[None, None], [None, None]]
        inv = [[None, None], [None, None]]
        for half in range(2):
            for pair in range(2):
                head = g * 4 + pair * 2 + half
                lt = logits_t[half][:, pair * BLOCK:(pair + 1) * BLOCK] + bias_s[slots[blk], head]
                sink = sink_ref[head]
                m = jnp.maximum(jnp.max(lt, axis=0, keepdims=True), sink)
                e = jnp.exp(lt - m)
                inv[half][pair] = 1.0 / (jnp.sum(e, axis=0, keepdims=True) + jnp.exp(sink - m))
                probs_t[half][pair] = e.astype(BF16)
        v_lo_t = jnp.concatenate([vtlo_s[g, blk], vtlo_s[g, blk + 1]], axis=1)
        v_hi_t = jnp.concatenate([vthi_s[g, blk], vthi_s[g, blk + 1]], axis=1)
        out_t = (_dot(v_lo_t, jnp.concatenate(probs_t[0], axis=1))
                 + _dot(v_hi_t, jnp.concatenate(probs_t[1], axis=1)))
        scale = jnp.concatenate(
            [jnp.broadcast_to(jnp.concatenate(inv[0], axis=1), (HEAD_DIM, 2 * BLOCK)),
             jnp.broadcast_to(jnp.concatenate(inv[1], axis=1), (HEAD_DIM, 2 * BLOCK))], axis=0)
        return (out_t * scale).T

    units = [(blk, g) for g in range(SWA_KV_HEADS) for blk in range(nblk)]
    logits_next = qk(*units[0])
    for g in range(SWA_KV_HEADS):
        cols = slice(g * 2 * PAIR, (g + 1) * 2 * PAIR)

        def proj(off):
            return _dot(h, win_ref[:, off + g * 2 * PAIR:off + (g + 1) * 2 * PAIR])

        def conv_taps(_):
            cu = proj(OFF_C) * proj(OFF_U)
            cu_s[SUBLANES:SUBLANES + tm, cols] = cu
            return (convw_ref[0:1, cols] * cu_s[SUBLANES - 2:SUBLANES - 2 + tm, cols]
                    + convw_ref[1:2, cols] * cu_s[SUBLANES - 1:SUBLANES - 1 + tm, cols]
                    + convw_ref[2:3, cols] * cu)

        stages = [conv_taps,
                  lambda conv: proj(OFF_B) * conv,
                  lambda conv: jax.nn.sigmoid(proj(OFF_GC)) * conv]
        attn_blocks = []
        conv = None
        for blk in range(nblk):
            logits_t = logits_next
            nxt = g * nblk + blk + 1
            if nxt < len(units):
                logits_next = qk(*units[nxt])
            if blk < len(stages):
                conv = stages[blk](conv)
            out = softmax_pv(blk, g, logits_t)
            attn_blocks.append(jnp.concatenate([out[0:BLOCK], out[BLOCK:2 * BLOCK]], axis=1))
        for stage in stages[nblk:]:
            conv = stage(conv)
        attn = jnp.concatenate(attn_blocks, axis=0)
        merged_s[:, cols] = (jax.nn.sigmoid(proj(OFF_GA)) * attn + conv).astype(BF16)

    o_ref[...] = x_ref[...] + _dot(merged_s[...], wout_ref[...])


def _mix(x, posq, posk, lut, sinks, gain, w_in, conv_w, w_out, *, seq_len):
    t = x.shape[0]
    tm = TOKEN_TILE
    nblk = tm // BLOCK
    k_scratch = pltpu.VMEM((SWA_KV_HEADS, BLOCK + tm, PAIR), BF16)
    vt_scratch = pltpu.VMEM((SWA_KV_HEADS, nblk + 1, PAIR, BLOCK), BF16)
    return pl.pallas_call(
        functools.partial(_mix_kernel, tiles_per_seq=seq_len // tm),
        grid=(t // tm,),
        in_specs=[
            pl.BlockSpec((tm, D_MODEL), lambda i: (i, 0)),
            pl.BlockSpec((1, nblk, BLOCK), lambda i: (i, 0, 0)),
            pl.BlockSpec((1, nblk, 2 * BLOCK), lambda i: (i, 0, 0)),
            _resident((SWA_HEADS, WINDOW)),
            pl.BlockSpec(memory_space=pltpu.SMEM),
            _resident((1, D_MODEL)),
            _resident((D_MODEL, IN_WIDTH)),
            _resident((CONV_K, D_MODEL)),
            _resident((D_MODEL, D_MODEL)),
        ],
        out_specs=pl.BlockSpec((tm, D_MODEL), lambda i: (i, 0)),
        out_shape=jax.ShapeDtypeStruct((t, D_MODEL), F32),
        scratch_shapes=[
            pltpu.VMEM((tm, Q_WIDTH), BF16),
            k_scratch, k_scratch, vt_scratch, vt_scratch,
            pltpu.VMEM((tm, D_MODEL), BF16),
            pltpu.VMEM((SUBLANES + tm, D_MODEL), F32),
            pltpu.VMEM((tm, D_MODEL), BF16),
            pltpu.VMEM((nblk + 1, 2 * BLOCK, BLOCK), jnp.int32),
            pltpu.VMEM((nblk + 1, SWA_HEADS, 2 * BLOCK, BLOCK), F32),
        ],
        compiler_params=_params(),
        name="mix",
    )(x, posq, posk, lut, sinks, gain, w_in, conv_w, w_out)


def _mem_kv_kernel(mem_ref, g_ref, wkv_ref, kv_ref):
    h = _rms(mem_ref[...], g_ref[...]).astype(BF16)
    kv_ref[...] = _dot(h, wkv_ref[...]).astype(BF16)


def _mem_kv(mem, gain, w_kv):
    rows = mem.shape[0]
    tm = TOKEN_TILE
    return pl.pallas_call(
        _mem_kv_kernel,
        grid=(rows // tm,),
        in_specs=[
            pl.BlockSpec((tm, D_MODEL), lambda i: (i, 0)),
            _resident((1, D_MODEL)),
            _resident((D_MODEL, 2 * D_MODEL)),
        ],
        out_specs=pl.BlockSpec((tm, 2 * D_MODEL), lambda i: (i, 0)),
        out_shape=jax.ShapeDtypeStruct((rows, 2 * D_MODEL), BF16),
        compiler_params=_params(),
        name="mem_kv",
    )(mem, gain, w_kv)


def _xattn_kernel(x_ref, g_ref, wq_ref, kv_ref, wo_ref, o_ref):
    x = x_ref[...]
    h = _rms(x, g_ref[...]).astype(BF16)
    def head_cols(hd):
        return slice(hd * MEM_HEAD_DIM, (hd + 1) * MEM_HEAD_DIM)

    def q_proj(hd):
        return (_dot(h, wq_ref[:, head_cols(hd)]) * (MEM_HEAD_DIM ** -0.5)).astype(BF16)

    def logits(hd, q):
        return _dot_nt(q, kv_ref[0, :, head_cols(hd)])

    def context(hd, lg):
        e = jnp.exp(lg - jnp.max(lg, axis=1, keepdims=True))
        p = (e / jnp.sum(e, axis=1, keepdims=True)).astype(BF16)
        v = kv_ref[0, :, D_MODEL + hd * MEM_HEAD_DIM:D_MODEL + (hd + 1) * MEM_HEAD_DIM]
        return _dot(p, v).astype(BF16)

    qs = [q_proj(0), q_proj(1)]
    lgs = [logits(0, qs[0])]
    acc = x
    for hd in range(MEM_HEADS):
        if hd + 2 < MEM_HEADS:
            qs.append(q_proj(hd + 2))
        if hd + 1 < MEM_HEADS:
            lgs.append(logits(hd + 1, qs[hd + 1]))
        acc = acc + _dot(context(hd, lgs[hd]), wo_ref[head_cols(hd), :])
    o_ref[...] = acc


def _xattn(x, gain, w_q, kv, w_o, *, seq_len):
    t = x.shape[0]
    tm = XATTN_TOKEN_TILE
    tiles_per_seq = seq_len // tm
    return pl.pallas_call(
        _xattn_kernel,
        grid=(t // tm,),
        in_specs=[
            pl.BlockSpec((tm, D_MODEL), lambda i: (i, 0)),
            _resident((1, D_MODEL)),
            _resident((D_MODEL, D_MODEL)),
            pl.BlockSpec((1, N_MEM, 2 * D_MODEL), lambda i: (i // tiles_per_seq, 0, 0)),
            _resident((D_MODEL, D_MODEL)),
        ],
        out_specs=pl.BlockSpec((tm, D_MODEL), lambda i: (i, 0)),
        out_shape=jax.ShapeDtypeStruct((t, D_MODEL), F32),
        compiler_params=_params(),
        name="xattn",
    )(x, gain, w_q, kv, w_o)


def kernel(x, mem, positions, rel_bias, ffn1_norm, ffn1_w_gu, ffn1_w_down, mix_norm, w_in, sinks, conv_w, w_out, xattn_norm, mem_norm, xattn_wq, xattn_wkv, xattn_wo, ffn2_norm, ffn2_w_gu, ffn2_w_down, final_norm):
    batch, seq_len, _ = x.shape
    depth = w_in.shape[0]
    tokens = batch * seq_len
    nb = seq_len // BLOCK

    xt = x.reshape(tokens, D_MODEL)
    posq = positions.reshape(tokens // TOKEN_TILE, TOKEN_TILE // BLOCK, BLOCK)
    pos_blocks = positions.reshape(batch, nb, BLOCK)
    prev = jnp.concatenate([jnp.full((batch, 1, BLOCK), POS_PAD, positions.dtype), pos_blocks[:, :-1]], axis=1)
    posk = jnp.concatenate([prev, pos_blocks], axis=2).reshape(tokens // TOKEN_TILE, TOKEN_TILE // BLOCK, 2 * BLOCK)
    lut = _bias_lut(rel_bias.T)
    final_gain = final_norm.reshape(1, D_MODEL)

    for l in range(depth):
        xt = _ffn(xt, ffn1_norm[l].reshape(1, D_MODEL), ffn1_w_gu[l], ffn1_w_down[l],
                  final_gain, final_norm=False)
        xt = _mix(xt, posq, posk, lut, sinks[l], mix_norm[l].reshape(1, D_MODEL),
                  w_in[l], conv_w[l], w_out[l], seq_len=seq_len)
        kv = _mem_kv(mem.reshape(batch * N_MEM, D_MODEL), mem_norm[l].reshape(1, D_MODEL), xattn_wkv[l])
        xt = _xattn(xt, xattn_norm[l].reshape(1, D_MODEL), xattn_wq[l],
                    kv.reshape(batch, N_MEM, 2 * D_MODEL), xattn_wo[l], seq_len=seq_len)
        xt = _ffn(xt, ffn2_norm[l].reshape(1, D_MODEL), ffn2_w_gu[l], ffn2_w_down[l],
                  final_gain, final_norm=(l == depth - 1))
    return xt.reshape(batch, seq_len, D_MODEL)
```

```python
import functools
import math

import jax
import jax.numpy as jnp
from jax import lax
from jax.experimental import pallas as pl
from jax.experimental.pallas import tpu as pltpu

D_MODEL = 1024
D_FF = 2816
N_MEM = 256
MEM_HEADS = 4
MEM_HEAD_DIM = D_MODEL // MEM_HEADS
HEAD_DIM = 64
SWA_HEADS = 16
SWA_KV_HEADS = 4
WINDOW = 128
BLOCK = 128
REL_BUCKETS = 32
REL_MAX_DIST = 128
CONV_K = 3
EPS = 1e-6
NEG_INF = -1e30
POS_PAD = 1 << 30

Q_WIDTH = SWA_HEADS * HEAD_DIM
KV_WIDTH = SWA_KV_HEADS * HEAD_DIM
OFF_Q = 0
OFF_K = OFF_Q + Q_WIDTH
OFF_V = OFF_K + KV_WIDTH
OFF_C = OFF_V + KV_WIDTH
OFF_B = OFF_C + D_MODEL
OFF_U = OFF_B + D_MODEL
OFF_GA = OFF_U + D_MODEL
OFF_GC = OFF_GA + D_MODEL
IN_WIDTH = OFF_GC + D_MODEL

LANES = 128
SUBLANES = 8
MXU_COLS = 256
VMEM_LIMIT_BYTES = 60 * 1024 * 1024

TOKEN_TILE = 512
XATTN_TOKEN_TILE = 1024
FFN_TOKEN_TILE = 512
FF_CHUNK = MXU_COLS
PAIR = 2 * HEAD_DIM

BF16 = jnp.bfloat16
F32 = jnp.float32


def _rms(x, g):
    return x * lax.rsqrt(jnp.mean(x * x, axis=-1, keepdims=True) + EPS) * g


def _dot(a, b):
    return jnp.dot(a.astype(BF16), b.astype(BF16), preferred_element_type=F32)


def _dot_nt(a, b):
    return lax.dot_general(a.astype(BF16), b.astype(BF16), (((1,), (1,)), ((), ())),
                           preferred_element_type=F32)


def _resident(shape):
    return pl.BlockSpec(shape, lambda i: (0,) * len(shape), pipeline_mode=pl.Buffered(1))


def _params():
    return pltpu.CompilerParams(dimension_semantics=("arbitrary",), vmem_limit_bytes=VMEM_LIMIT_BYTES)


FF_CHUNKS = D_FF // FF_CHUNK


def _ffn_kernel(x_ref, xnext_ref, g_ref, wgu_hbm, wd_hbm, fg_ref, o_ref, wgu_v, wd_v, act_ref, h_s, act0_s, sems,
                *, final_norm):
    def gate_copy(j):
        cols = pl.ds(j * FF_CHUNK, FF_CHUNK)
        return pltpu.make_async_copy(wgu_hbm.at[:, cols], wgu_v.at[:, cols], sems.at[j])

    def up_copy(j):
        cols = pl.ds(D_FF + j * FF_CHUNK, FF_CHUNK)
        return pltpu.make_async_copy(wgu_hbm.at[:, cols], wgu_v.at[:, cols], sems.at[FF_CHUNKS + j])

    def down_copy(j):
        rows = pl.ds(j * FF_CHUNK, FF_CHUNK)
        return pltpu.make_async_copy(wd_hbm.at[rows, :], wd_v.at[rows, :], sems.at[2 * FF_CHUNKS + j])

    step = pl.program_id(0)
    cur, nxt = step % 2, (step + 1) % 2

    def chunk_act(h, j):
        lo = j * FF_CHUNK
        gate = _dot(h, wgu_v[:, lo:lo + FF_CHUNK])
        up = _dot(h, wgu_v[:, D_FF + lo:D_FF + lo + FF_CHUNK])
        return (gate * jax.nn.sigmoid(gate) * up).astype(BF16)

    def body(first):
        if first:
            h = _rms(x_ref[...], g_ref[...]).astype(BF16)
            gate_copy(0).wait()
            up_copy(0).wait()
            act_ref[:, 0:FF_CHUNK] = chunk_act(h, 0)
        else:
            h = h_s[cur]
            act_ref[:, 0:FF_CHUNK] = act0_s[cur]
        for j in range(1, FF_CHUNKS):
            if first:
                gate_copy(j).wait()
                up_copy(j).wait()
            act_ref[:, j * FF_CHUNK:(j + 1) * FF_CHUNK] = chunk_act(h, j)
        if first:
            for j in range(FF_CHUNKS):
                down_copy(j).wait()
        y = x_ref[...] + 0.5 * _dot(act_ref[...], wd_v[...])
        h_next = _rms(xnext_ref[...], g_ref[...]).astype(BF16)
        h_s[nxt] = h_next
        act0_s[nxt] = chunk_act(h_next, 0)
        if final_norm:
            y = _rms(y, fg_ref[...])
        o_ref[...] = y

    first_step = step == 0

    @pl.when(first_step)
    def _():
        for j in range(FF_CHUNKS):
            gate_copy(j).start()
            up_copy(j).start()
        for j in range(FF_CHUNKS):
            down_copy(j).start()
        body(True)

    @pl.when(jnp.logical_not(first_step))
    def _():
        body(False)


def _ffn(x, gain, w_gu, w_down, final_gain, *, final_norm):
    t = x.shape[0]
    tm = FFN_TOKEN_TILE
    return pl.pallas_call(
        functools.partial(_ffn_kernel, final_norm=final_norm),
        grid=(t // tm,),
        in_specs=[
            pl.BlockSpec((tm, D_MODEL), lambda i: (i, 0)),
            pl.BlockSpec((tm, D_MODEL), lambda i: (jnp.minimum(i + 1, t // tm - 1), 0)),
            _resident((1, D_MODEL)),
            pl.BlockSpec(memory_space=pl.ANY),
            pl.BlockSpec(memory_space=pl.ANY),
            _resident((1, D_MODEL)),
        ],
        out_specs=pl.BlockSpec((tm, D_MODEL), lambda i: (i, 0)),
        out_shape=jax.ShapeDtypeStruct((t, D_MODEL), F32),
        scratch_shapes=[
            pltpu.VMEM((D_MODEL, 2 * D_FF), F32),
            pltpu.VMEM((D_FF, D_MODEL), F32),
            pltpu.VMEM((tm, D_FF), BF16),
            pltpu.VMEM((2, tm, D_MODEL), BF16),
            pltpu.VMEM((2, tm, FF_CHUNK), BF16),
            pltpu.SemaphoreType.DMA((3 * FF_CHUNKS,)),
        ],
        compiler_params=_params(),
        name="ffn_final" if final_norm else "ffn",
    )(x, x, gain, w_gu, w_down, final_gain)


def _bias_lut_kernel(tab_ref, lut_ref):
    n = lax.broadcasted_iota(jnp.int32, (SWA_HEADS, WINDOW), 1).astype(F32)
    max_exact = REL_BUCKETS // 2
    nf = jnp.maximum(n, 1.0)
    large = max_exact + jnp.floor(jnp.log(nf / max_exact) / math.log(REL_MAX_DIST / max_exact)
                                  * (REL_BUCKETS - max_exact))
    large = jnp.minimum(large, REL_BUCKETS - 1.0)
    bucket = jnp.where(n < max_exact, n, large)
    lut = jnp.broadcast_to(tab_ref[:, 0:1], (SWA_HEADS, WINDOW))
    for b in range(1, REL_BUCKETS):
        lut = jnp.where(bucket >= b, tab_ref[:, b:b + 1], lut)
    lut_ref[...] = lut


def _bias_lut(rel_bias_t):
    return pl.pallas_call(
        _bias_lut_kernel,
        out_shape=jax.ShapeDtypeStruct((SWA_HEADS, WINDOW), F32),
        name="bias_lut",
    )(rel_bias_t)


def _key_head_slabs(t):
    lane = lax.broadcasted_iota(jnp.int32, (t.shape[0], PAIR), 1)
    low_half = lane < HEAD_DIM
    los, his = [], []
    for p in range(SWA_KV_HEADS // 2):
        pair = t[:, p * PAIR:(p + 1) * PAIR]
        swapped = pltpu.roll(pair, HEAD_DIM, axis=1)
        zero = jnp.zeros_like(pair)
        los += [jnp.where(low_half, pair, zero), jnp.where(low_half, swapped, zero)]
        his += [jnp.where(low_half, zero, swapped), jnp.where(low_half, zero, pair)]
    return [a.astype(BF16) for a in los], [a.astype(BF16) for a in his]


def _value_head_slabs_t(t):
    los, his = [], []
    for p in range(SWA_KV_HEADS // 2):
        pair_t = t[:, p * PAIR:(p + 1) * PAIR].T
        zero = jnp.zeros((HEAD_DIM, t.shape[0]), F32)
        for head_t in (pair_t[0:HEAD_DIM], pair_t[HEAD_DIM:PAIR]):
            los.append(jnp.concatenate([head_t, zero], axis=0))
            his.append(jnp.concatenate([zero, head_t], axis=0))
    return [a.astype(BF16) for a in los], [a.astype(BF16) for a in his]


def _mix_kernel(x_ref, posq_ref, posk_ref, lut_ref, sink_ref, g_ref, win_ref, convw_ref, wout_ref, o_ref,
                q_s, klo_s, khi_s, vtlo_s, vthi_s, h_s, cu_s, merged_s, relm_s, bias_s, *, tiles_per_seq):
    tm = x_ref.shape[0]
    nblk = tm // BLOCK
    step = pl.program_id(0)
    first = (step % tiles_per_seq) == 0

    @pl.when(first)
    def _():
        for s in (klo_s, khi_s):
            s[:, 0:BLOCK, :] = jnp.zeros((SWA_KV_HEADS, BLOCK, PAIR), BF16)
        for s in (vtlo_s, vthi_s):
            s[:, 0] = jnp.zeros((SWA_KV_HEADS, PAIR, BLOCK), BF16)
        cu_s[0:SUBLANES, :] = jnp.zeros((SUBLANES, D_MODEL), F32)

    @pl.when(jnp.logical_not(first))
    def _():
        for s in (klo_s, khi_s):
            s[:, 0:BLOCK, :] = s[:, tm:tm + BLOCK, :]
        for s in (vtlo_s, vthi_s):
            s[:, 0] = s[:, nblk]
        cu_s[0:SUBLANES, :] = cu_s[tm:tm + SUBLANES, :]

    @pl.when(step == 0)
    def _():
        relm_s[...] = jnp.full(relm_s.shape, -2, jnp.int32)

    slots = [jnp.where(first, nblk, 0)] + list(range(1, nblk))

    relms, misses = [], []
    for j in range(nblk):
        pos_k = jnp.concatenate(
            [jnp.broadcast_to(posk_ref[0, j:j + 1, c * BLOCK:(c + 1) * BLOCK], (BLOCK, BLOCK)).T
             for c in range(2)], axis=0)
        rel = posq_ref[0, j:j + 1, :] - pos_k
        relm = jnp.where((rel >= 0) & (rel < WINDOW), rel, -1)
        relms.append(relm)
        misses.append(jnp.sum((relm != relm_s[slots[j]]).astype(jnp.int32)))

    h_s[...] = _rms(x_ref[...], g_ref[...]).astype(BF16)
    h = h_s[...]

    k_lo, k_hi = _key_head_slabs(_dot(h, win_ref[:, OFF_K:OFF_K + KV_WIDTH]))
    vt_lo, vt_hi = _value_head_slabs_t(_dot(h, win_ref[:, OFF_V:OFF_V + KV_WIDTH]))
    for g in range(SWA_KV_HEADS):
        klo_s[g, BLOCK:BLOCK + tm, :] = k_lo[g]
        khi_s[g, BLOCK:BLOCK + tm, :] = k_hi[g]
        for j in range(nblk):
            vtlo_s[g, j + 1] = vt_lo[g][:, j * BLOCK:(j + 1) * BLOCK]
            vthi_s[g, j + 1] = vt_hi[g][:, j * BLOCK:(j + 1) * BLOCK]
    q_s[...] = (_dot(h, win_ref[:, OFF_Q:OFF_Q + Q_WIDTH]) * (HEAD_DIM ** -0.5)).astype(BF16)

    for j in range(nblk):
        @pl.when(misses[j] != 0)
        def _(j=j):
            relm_s[slots[j]] = relms[j]
            shown = relms[j] >= 0
            idx = jnp.maximum(relms[j], 0)
            for hd in range(SWA_HEADS):
                lut = jnp.broadcast_to(lut_ref[hd:hd + 1, :], (2 * BLOCK, WINDOW))
                bias_s[slots[j], hd] = jnp.where(shown, jnp.take_along_axis(lut, idx, axis=1), NEG_INF)

    h = h_s[...]

    def qk(blk, g):
        rows = slice(blk * BLOCK, (blk + 1) * BLOCK)
        keys = slice(blk * BLOCK, (blk + 2) * BLOCK)
        c0 = g * 2 * PAIR
        q2 = jnp.concatenate([q_s[rows, c0:c0 + PAIR], q_s[rows, c0 + PAIR:c0 + 2 * PAIR]], axis=0)
        return _dot_nt(klo_s[g, keys, :], q2), _dot_nt(khi_s[g, keys, :], q2)

    def softmax_pv(blk, g, logits_t):
        probs_t = [[None, None], [None, None]]
        inv = [[None, None], [None, None]]
        for half in range(2):
            for pair in range(2):
                head = g * 4 + pair * 2 + half
                lt = logits_t[half][:, pair * BLOCK:(pair + 1) * BLOCK] + bias_s[slots[blk], head]
                sink = sink_ref[head]
                m = jnp.maximum(jnp.max(lt, axis=0, keepdims=True), sink)
                e = jnp.exp(lt - m)
                inv[half][pair] = 1.0 / (jnp.sum(e, axis=0, keepdims=True) + jnp.exp(sink - m))
                probs_t[half][pair] = e.astype(BF16)
        v_lo_t = jnp.concatenate([vtlo_s[g, blk], vtlo_s[g, blk + 1]], axis=1)
        v_hi_t = jnp.concatenate([vthi_s[g, blk], vthi_s[g, blk + 1]], axis=1)
        out_t = (_dot(v_lo_t, jnp.concatenate(probs_t[0], axis=1))
                 + _dot(v_hi_t, jnp.concatenate(probs_t[1], axis=1)))
        scale = jnp.concatenate(
            [jnp.broadcast_to(jnp.concatenate(inv[0], axis=1), (HEAD_DIM, 2 * BLOCK)),
             jnp.broadcast_to(jnp.concatenate(inv[1], axis=1), (HEAD_DIM, 2 * BLOCK))], axis=0)
        return (out_t * scale).T

    units = [(blk, g) for g in range(SWA_KV_HEADS) for blk in range(nblk)]
    logits_next = qk(*units[0])
    for g in range(SWA_KV_HEADS):
        cols = slice(g * 2 * PAIR, (g + 1) * 2 * PAIR)

        def proj(off):
            return _dot(h, win_ref[:, off + g * 2 * PAIR:off + (g + 1) * 2 * PAIR])

        def conv_taps(_):
            cu = proj(OFF_C) * proj(OFF_U)
            cu_s[SUBLANES:SUBLANES + tm, cols] = cu
            return (convw_ref[0:1, cols] * cu_s[SUBLANES - 2:SUBLANES - 2 + tm, cols]
                    + convw_ref[1:2, cols] * cu_s[SUBLANES - 1:SUBLANES - 1 + tm, cols]
                    + convw_ref[2:3, cols] * cu)

        stages = [conv_taps,
                  lambda conv: proj(OFF_B) * conv,
                  lambda conv: jax.nn.sigmoid(proj(OFF_GC)) * conv]
        attn_blocks = []
        conv = None
        for blk in range(nblk):
            logits_t = logits_next
            nxt = g * nblk + blk + 1
            if nxt < len(units):
                logits_next = qk(*units[nxt])
            if blk < len(stages):
                conv = stages[blk](conv)
            out = softmax_pv(blk, g, logits_t)
            attn_blocks.append(jnp.concatenate([out[0:BLOCK], out[BLOCK:2 * BLOCK]], axis=1))
        for stage in stages[nblk:]:
            conv = stage(conv)
        attn = jnp.concatenate(attn_blocks, axis=0)
        merged_s[:, cols] = (jax.nn.sigmoid(proj(OFF_GA)) * attn + conv).astype(BF16)

    o_ref[...] = x_ref[...] + _dot(merged_s[...], wout_ref[...])


def _mix(x, posq, posk, lut, sinks, gain, w_in, conv_w, w_out, *, seq_len):
    t = x.shape[0]
    tm = TOKEN_TILE
    nblk = tm // BLOCK
    k_scratch = pltpu.VMEM((SWA_KV_HEADS, BLOCK + tm, PAIR), BF16)
    vt_scratch = pltpu.VMEM((SWA_KV_HEADS, nblk + 1, PAIR, BLOCK), BF16)
    return pl.pallas_call(
        functools.partial(_mix_kernel, tiles_per_seq=seq_len // tm),
        grid=(t // tm,),
        in_specs=[
            pl.BlockSpec((tm, D_MODEL), lambda i: (i, 0)),
            pl.BlockSpec((1, nblk, BLOCK), lambda i: (i, 0, 0)),
            pl.BlockSpec((1, nblk, 2 * BLOCK), lambda i: (i, 0, 0)),
            _resident((SWA_HEADS, WINDOW)),
            pl.BlockSpec(memory_space=pltpu.SMEM),
            _resident((1, D_MODEL)),
            _resident((D_MODEL, IN_WIDTH)),
            _resident((CONV_K, D_MODEL)),
            _resident((D_MODEL, D_MODEL)),
        ],
        out_specs=pl.BlockSpec((tm, D_MODEL), lambda i: (i, 0)),
        out_shape=jax.ShapeDtypeStruct((t, D_MODEL), F32),
        scratch_shapes=[
            pltpu.VMEM((tm, Q_WIDTH), BF16),
            k_scratch, k_scratch, vt_scratch, vt_scratch,
            pltpu.VMEM((tm, D_MODEL), BF16),
            pltpu.VMEM((SUBLANES + tm, D_MODEL), F32),
            pltpu.VMEM((tm, D_MODEL), BF16),
            pltpu.VMEM((nblk + 1, 2 * BLOCK, BLOCK), jnp.int32),
            pltpu.VMEM((nblk + 1, SWA_HEADS, 2 * BLOCK, BLOCK), F32),
        ],
        compiler_params=_params(),
        name="mix",
    )(x, posq, posk, lut, sinks, gain, w_in, conv_w, w_out)


def _mem_kv_kernel(mem_ref, g_ref, wkv_ref, kv_ref):
    h = _rms(mem_ref[...], g_ref[...]).astype(BF16)
    kv_ref[...] = _dot(h, wkv_ref[...]).astype(BF16)


def _mem_kv(mem, gain, w_kv):
    rows = mem.shape[0]
    tm = TOKEN_TILE
    return pl.pallas_call(
        _mem_kv_kernel,
        grid=(rows // tm,),
        in_specs=[
            pl.BlockSpec((tm, D_MODEL), lambda i: (i, 0)),
            _resident((1, D_MODEL)),
            _resident((D_MODEL, 2 * D_MODEL)),
        ],
        out_specs=pl.BlockSpec((tm, 2 * D_MODEL), lambda i: (i, 0)),
        out_shape=jax.ShapeDtypeStruct((rows, 2 * D_MODEL), BF16),
        compiler_params=_params(),
        name="mem_kv",
    )(mem, gain, w_kv)


def _xattn_kernel(x_ref, g_ref, wq_ref, kv_ref, wo_ref, o_ref):
    x = x_ref[...]
    h = _rms(x, g_ref[...]).astype(BF16)
    def head_cols(hd):
        return slice(hd * MEM_HEAD_DIM, (hd + 1) * MEM_HEAD_DIM)

    def q_proj(hd):
        return (_dot(h, wq_ref[:, head_cols(hd)]) * (MEM_HEAD_DIM ** -0.5)).astype(BF16)

    def logits(hd, q):
        return _dot_nt(q, kv_ref[0, :, head_cols(hd)])

    def context(hd, lg):
        e = jnp.exp(lg - jnp.max(lg, axis=1, keepdims=True))
        p = (e / jnp.sum(e, axis=1, keepdims=True)).astype(BF16)
        v = kv_ref[0, :, D_MODEL + hd * MEM_HEAD_DIM:D_MODEL + (hd + 1) * MEM_HEAD_DIM]
        return _dot(p, v).astype(BF16)

    qs = [q_proj(0), q_proj(1)]
    lgs = [logits(0, qs[0])]
    acc = x
    for hd in range(MEM_HEADS):
        if hd + 2 < MEM_HEADS:
            qs.append(q_proj(hd + 2))
        if hd + 1 < MEM_HEADS:
            lgs.append(logits(hd + 1, qs[hd + 1]))
        acc = acc + _dot(context(hd, lgs[hd]), wo_ref[head_cols(hd), :])
    o_ref[...] = acc


def _xattn(x, gain, w_q, kv, w_o, *, seq_len):
    t = x.shape[0]
    tm = XATTN_TOKEN_TILE
    tiles_per_seq = seq_len // tm
    return pl.pallas_call(
        _xattn_kernel,
        grid=(t // tm,),
        in_specs=[
            pl.BlockSpec((tm, D_MODEL), lambda i: (i, 0)),
            _resident((1, D_MODEL)),
            _resident((D_MODEL, D_MODEL)),
            pl.BlockSpec((1, N_MEM, 2 * D_MODEL), lambda i: (i // tiles_per_seq, 0, 0)),
            _resident((D_MODEL, D_MODEL)),
        ],
        out_specs=pl.BlockSpec((tm, D_MODEL), lambda i: (i, 0)),
        out_shape=jax.ShapeDtypeStruct((t, D_MODEL), F32),
        compiler_params=_params(),
        name="xattn",
    )(x, gain, w_q, kv, w_o)


def kernel(x, mem, positions, rel_bias, ffn1_norm, ffn1_w_gu, ffn1_w_down, mix_norm, w_in, sinks, conv_w, w_out, xattn_norm, mem_norm, xattn_wq, xattn_wkv, xattn_wo, ffn2_norm, ffn2_w_gu, ffn2_w_down, final_norm):
    batch, seq_len, _ = x.shape
    depth = w_in.shape[0]
    tokens = batch * seq_len
    nb = seq_len // BLOCK

    xt = x.reshape(tokens, D_MODEL)
    posq = positions.reshape(tokens // TOKEN_TILE, TOKEN_TILE // BLOCK, BLOCK)
    pos_blocks = positions.reshape(batch, nb, BLOCK)
    prev = jnp.concatenate([jnp.full((batch, 1, BLOCK), POS_PAD, positions.dtype), pos_blocks[:, :-1]], axis=1)
    posk = jnp.concatenate([prev, pos_blocks], axis=2).reshape(tokens // TOKEN_TILE, TOKEN_TILE // BLOCK, 2 * BLOCK)
    lut = _bias_lut(rel_bias.T)
    final_gain = final_norm.reshape(1, D_MODEL)

    for l in range(depth):
        xt = _ffn(xt, ffn1_norm[l].reshape(1, D_MODEL), ffn1_w_gu[l], ffn1_w_down[l],
                  final_gain, final_norm=False)
        xt = _mix(xt, posq, posk, lut, sinks[l], mix_norm[l].reshape(1, D_MODEL),
                  w_in[l], conv_w[l], w_out[l], seq_len=seq_len)
        kv = _mem_kv(mem.reshape(batch * N_MEM, D_MODEL), mem_norm[l].reshape(1, D_MODEL), xattn_wkv[l])
        xt = _xattn(xt, xattn_norm[l].reshape(1, D_MODEL), xattn_wq[l],
                    kv.reshape(batch, N_MEM, 2 * D_MODEL), xattn_wo[l], seq_len=seq_len)
        xt = _ffn(xt, ffn2_norm[l].reshape(1, D_MODEL), ffn2_w_gu[l], ffn2_w_down[l],
                  final_gain, final_norm=(l == depth - 1))
    return xt.reshape(batch, seq_len, D_MODEL)
```

```python
import functools
import math

import jax
import jax.numpy as jnp
from jax import lax
from jax.experimental import pallas as pl
from jax.experimental.pallas import tpu as pltpu

D_MODEL = 1024
D_FF = 2816
N_MEM = 256
MEM_HEADS = 4
MEM_HEAD_DIM = D_MODEL // MEM_HEADS
HEAD_DIM = 64
SWA_HEADS = 16
SWA_KV_HEADS = 4
WINDOW = 128
BLOCK = 128
REL_BUCKETS = 32
REL_MAX_DIST = 128
CONV_K = 3
EPS = 1e-6
NEG_INF = -1e30
POS_PAD = 1 << 30

Q_WIDTH = SWA_HEADS * HEAD_DIM
KV_WIDTH = SWA_KV_HEADS * HEAD_DIM
OFF_Q = 0
OFF_K = OFF_Q + Q_WIDTH
OFF_V = OFF_K + KV_WIDTH
OFF_C = OFF_V + KV_WIDTH
OFF_B = OFF_C + D_MODEL
OFF_U = OFF_B + D_MODEL
OFF_GA = OFF_U + D_MODEL
OFF_GC = OFF_GA + D_MODEL
IN_WIDTH = OFF_GC + D_MODEL

LANES = 128
SUBLANES = 8
MXU_COLS = 256
VMEM_LIMIT_BYTES = 60 * 1024 * 1024

TOKEN_TILE = 512
XATTN_TOKEN_TILE = 1024
FFN_TOKEN_TILE = 512
FF_CHUNK = MXU_COLS
PAIR = 2 * HEAD_DIM

BF16 = jnp.bfloat16
F32 = jnp.float32


def _rms(x, g):
    return x * lax.rsqrt(jnp.mean(x * x, axis=-1, keepdims=True) + EPS) * g


def _dot(a, b):
    return jnp.dot(a.astype(BF16), b.astype(BF16), preferred_element_type=F32)


def _dot_nt(a, b):
    return lax.dot_general(a.astype(BF16), b.astype(BF16), (((1,), (1,)), ((), ())),
                           preferred_element_type=F32)


def _resident(shape):
    return pl.BlockSpec(shape, lambda i: (0,) * len(shape), pipeline_mode=pl.Buffered(1))


def _params():
    return pltpu.CompilerParams(dimension_semantics=("arbitrary",), vmem_limit_bytes=VMEM_LIMIT_BYTES)


FF_CHUNKS = D_FF // FF_CHUNK


def _ffn_kernel(x_ref, xnext_ref, g_ref, wgu_hbm, wd_hbm, fg_ref, o_ref, wgu_v, wd_v, act_ref, h_s, act0_s, sems,
                *, final_norm):
    def gate_copy(j):
        cols = pl.ds(j * FF_CHUNK, FF_CHUNK)
        return pltpu.make_async_copy(wgu_hbm.at[:, cols], wgu_v.at[:, cols], sems.at[j])

    def up_copy(j):
        cols = pl.ds(D_FF + j * FF_CHUNK, FF_CHUNK)
        return pltpu.make_async_copy(wgu_hbm.at[:, cols], wgu_v.at[:, cols], sems.at[FF_CHUNKS + j])

    def down_copy(j):
        rows = pl.ds(j * FF_CHUNK, FF_CHUNK)
        return pltpu.make_async_copy(wd_hbm.at[rows, :], wd_v.at[rows, :], sems.at[2 * FF_CHUNKS + j])

    step = pl.program_id(0)
    cur, nxt = step % 2, (step + 1) % 2

    def chunk_act(h, j):
        lo = j * FF_CHUNK
        gate = _dot(h, wgu_v[:, lo:lo + FF_CHUNK])
        up = _dot(h, wgu_v[:, D_FF + lo:D_FF + lo + FF_CHUNK])
        return (gate * jax.nn.sigmoid(gate) * up).astype(BF16)

    def body(first):
        if first:
            h = _rms(x_ref[...], g_ref[...]).astype(BF16)
            gate_copy(0).wait()
            up_copy(0).wait()
            act_ref[:, 0:FF_CHUNK] = chunk_act(h, 0)
        else:
            h = h_s[cur]
            act_ref[:, 0:FF_CHUNK] = act0_s[cur]
        for j in range(1, FF_CHUNKS):
            if first:
                gate_copy(j).wait()
                up_copy(j).wait()
            act_ref[:, j * FF_CHUNK:(j + 1) * FF_CHUNK] = chunk_act(h, j)
        if first:
            for j in range(FF_CHUNKS):
                down_copy(j).wait()
        y = x_ref[...] + 0.5 * _dot(act_ref[...], wd_v[...])
        h_next = _rms(xnext_ref[...], g_ref[...]).astype(BF16)
        h_s[nxt] = h_next
        act0_s[nxt] = chunk_act(h_next, 0)
        if final_norm:
            y = _rms(y, fg_ref[...])
        o_ref[...] = y

    first_step = step == 0

    @pl.when(first_step)
    def _():
        for j in range(FF_CHUNKS):
            gate_copy(j).start()
            up_copy(j).start()
        for j in range(FF_CHUNKS):
            down_copy(j).start()
        body(True)

    @pl.when(jnp.logical_not(first_step))
    def _():
        body(False)


def _ffn(x, gain, w_gu, w_down, final_gain, *, final_norm):
    t = x.shape[0]
    tm = FFN_TOKEN_TILE
    return pl.pallas_call(
        functools.partial(_ffn_kernel, final_norm=final_norm),
        grid=(t // tm,),
        in_specs=[
            pl.BlockSpec((tm, D_MODEL), lambda i: (i, 0)),
            pl.BlockSpec((tm, D_MODEL), lambda i: (jnp.minimum(i + 1, t // tm - 1), 0)),
            _resident((1, D_MODEL)),
            pl.BlockSpec(memory_space=pl.ANY),
            pl.BlockSpec(memory_space=pl.ANY),
            _resident((1, D_MODEL)),
        ],
        out_specs=pl.BlockSpec((tm, D_MODEL), lambda i: (i, 0)),
        out_shape=jax.ShapeDtypeStruct((t, D_MODEL), F32),
        scratch_shapes=[
            pltpu.VMEM((D_MODEL, 2 * D_FF), F32),
            pltpu.VMEM((D_FF, D_MODEL), F32),
            pltpu.VMEM((tm, D_FF), BF16),
            pltpu.VMEM((2, tm, D_MODEL), BF16),
            pltpu.VMEM((2, tm, FF_CHUNK), BF16),
            pltpu.SemaphoreType.DMA((3 * FF_CHUNKS,)),
        ],
        compiler_params=_params(),
        name="ffn_final" if final_norm else "ffn",
    )(x, x, gain, w_gu, w_down, final_gain)


def _bias_lut_kernel(tab_ref, lut_ref):
    n = lax.broadcasted_iota(jnp.int32, (SWA_HEADS, WINDOW), 1).astype(F32)
    max_exact = REL_BUCKETS // 2
    nf = jnp.maximum(n, 1.0)
    large = max_exact + jnp.floor(jnp.log(nf / max_exact) / math.log(REL_MAX_DIST / max_exact)
                                  * (REL_BUCKETS - max_exact))
    large = jnp.minimum(large, REL_BUCKETS - 1.0)
    bucket = jnp.where(n < max_exact, n, large)
    lut = jnp.broadcast_to(tab_ref[:, 0:1], (SWA_HEADS, WINDOW))
    for b in range(1, REL_BUCKETS):
        lut = jnp.where(bucket >= b, tab_ref[:, b:b + 1], lut)
    lut_ref[...] = lut


def _bias_lut(rel_bias_t):
    return pl.pallas_call(
        _bias_lut_kernel,
        out_shape=jax.ShapeDtypeStruct((SWA_HEADS, WINDOW), F32),
        name="bias_lut",
    )(rel_bias_t)


def _key_head_slabs(t):
    lane = lax.broadcasted_iota(jnp.int32, (t.shape[0], PAIR), 1)
    low_half = lane < HEAD_DIM
    los, his = [], []
    for p in range(SWA_KV_HEADS // 2):
        pair = t[:, p * PAIR:(p + 1) * PAIR]
        swapped = pltpu.roll(pair, HEAD_DIM, axis=1)
        zero = jnp.zeros_like(pair)
        los += [jnp.where(low_half, pair, zero), jnp.where(low_half, swapped, zero)]
        his += [jnp.where(low_half, zero, swapped), jnp.where(low_half, zero, pair)]
    return [a.astype(BF16) for a in los], [a.astype(BF16) for a in his]


def _value_head_slabs_t(t):
    los, his = [], []
    for p in range(SWA_KV_HEADS // 2):
        pair_t = t[:, p * PAIR:(p + 1) * PAIR].T
        zero = jnp.zeros((HEAD_DIM, t.shape[0]), F32)
        for head_t in (pair_t[0:HEAD_DIM], pair_t[HEAD_DIM:PAIR]):
            los.append(jnp.concatenate([head_t, zero], axis=0))
            his.append(jnp.concatenate([zero, head_t], axis=0))
    return [a.astype(BF16) for a in los], [a.astype(BF16) for a in his]


def _mix_kernel(x_ref, posq_ref, posk_ref, lut_ref, sink_ref, g_ref, win_ref, convw_ref, wout_ref, o_ref,
                q_s, klo_s, khi_s, vtlo_s, vthi_s, h_s, cu_s, merged_s, relm_s, bias_s, *, tiles_per_seq):
    tm = x_ref.shape[0]
    nblk = tm // BLOCK
    step = pl.program_id(0)
    first = (step % tiles_per_seq) == 0

    @pl.when(first)
    def _():
        for s in (klo_s, khi_s):
            s[:, 0:BLOCK, :] = jnp.zeros((SWA_KV_HEADS, BLOCK, PAIR), BF16)
        for s in (vtlo_s, vthi_s):
            s[:, 0] = jnp.zeros((SWA_KV_HEADS, PAIR, BLOCK), BF16)
        cu_s[0:SUBLANES, :] = jnp.zeros((SUBLANES, D_MODEL), F32)

    @pl.when(jnp.logical_not(first))
    def _():
        for s in (klo_s, khi_s):
            s[:, 0:BLOCK, :] = s[:, tm:tm + BLOCK, :]
        for s in (vtlo_s, vthi_s):
            s[:, 0] = s[:, nblk]
        cu_s[0:SUBLANES, :] = cu_s[tm:tm + SUBLANES, :]

    @pl.when(step == 0)
    def _():
        relm_s[...] = jnp.full(relm_s.shape, -2, jnp.int32)

    slots = [jnp.where(first, nblk, 0)] + list(range(1, nblk))

    relms, misses = [], []
    for j in range(nblk):
        pos_k = jnp.concatenate(
            [jnp.broadcast_to(posk_ref[0, j:j + 1, c * BLOCK:(c + 1) * BLOCK], (BLOCK, BLOCK)).T
             for c in range(2)], axis=0)
        rel = posq_ref[0, j:j + 1, :] - pos_k
        relm = jnp.where((rel >= 0) & (rel < WINDOW), rel, -1)
        relms.append(relm)
        misses.append(jnp.sum((relm != relm_s[slots[j]]).astype(jnp.int32)))

    h_s[...] = _rms(x_ref[...], g_ref[...]).astype(BF16)
    h = h_s[...]

    k_lo, k_hi = _key_head_slabs(_dot(h, win_ref[:, OFF_K:OFF_K + KV_WIDTH]))
    vt_lo, vt_hi = _value_head_slabs_t(_dot(h, win_ref[:, OFF_V:OFF_V + KV_WIDTH]))
    for g in range(SWA_KV_HEADS):
        klo_s[g, BLOCK:BLOCK + tm, :] = k_lo[g]
        khi_s[g, BLOCK:BLOCK + tm, :] = k_hi[g]
        for j in range(nblk):
            vtlo_s[g, j + 1] = vt_lo[g][:, j * BLOCK:(j + 1) * BLOCK]
            vthi_s[g, j + 1] = vt_hi[g][:, j * BLOCK:(j + 1) * BLOCK]
    q_s[...] = (_dot(h, win_ref[:, OFF_Q:OFF_Q + Q_WIDTH]) * (HEAD_DIM ** -0.5)).astype(BF16)

    for j in range(nblk):
        @pl.when(misses[j] != 0)
        def _(j=j):
            relm_s[slots[j]] = relms[j]
            shown = relms[j] >= 0
            idx = jnp.maximum(relms[j], 0)
            for hd in range(SWA_HEADS):
                lut = jnp.broadcast_to(lut_ref[hd:hd + 1, :], (2 * BLOCK, WINDOW))
                bias_s[slots[j], hd] = jnp.where(shown, jnp.take_along_axis(lut, idx, axis=1), NEG_INF)

    h = h_s[...]

    def qk(blk, g):
        rows = slice(blk * BLOCK, (blk + 1) * BLOCK)
        keys = slice(blk * BLOCK, (blk + 2) * BLOCK)
        c0 = g * 2 * PAIR
        q2 = jnp.concatenate([q_s[rows, c0:c0 + PAIR], q_s[rows, c0 + PAIR:c0 + 2 * PAIR]], axis=0)
        return _dot_nt(klo_s[g, keys, :], q2), _dot_nt(khi_s[g, keys, :], q2)

    def softmax_pv(blk, g, logits_t):
        probs_t = [[None, None], [None, None]]
        inv = [[None, None], [None, None]]
        for half in range(2):
            for pair in range(2):
                head = g * 4 + pair * 2 + half
                lt = logits_t[half][:, pair * BLOCK:(pair + 1) * BLOCK] + bias_s[slots[blk], head]
                sink = sink_ref[head]
                m = jnp.maximum(jnp.max(lt, axis=0, keepdims=True), sink)
                e = jnp.exp(lt - m)
                inv[half][pair] = 1.0 / (jnp.sum(e, axis=0, keepdims=True) + jnp.exp(sink - m))
                probs_t[half][pair] = e.astype(BF16)
        v_lo_t = jnp.concatenate([vtlo_s[g, blk], vtlo_s[g, blk + 1]], axis=1)
        v_hi_t = jnp.concatenate([vthi_s[g, blk], vthi_s[g, blk + 1]], axis=1)
        out_t = (_dot(v_lo_t, jnp.concatenate(probs_t[0], axis=1))
                 + _dot(v_hi_t, jnp.concatenate(probs_t[1], axis=1)))
        scale = jnp.concatenate(
            [jnp.broadcast_to(jnp.concatenate(inv[0], axis=1), (HEAD_DIM, 2 * BLOCK)),
             jnp.broadcast_to(jnp.concatenate(inv[1], axis=1), (HEAD_DIM, 2 * BLOCK))], axis=0)
        return (out_t * scale).T

    units = [(blk, g) for g in range(SWA_KV_HEADS) for blk in range(nblk)]
    logits_next = qk(*units[0])
    for g in range(SWA_KV_HEADS):
        cols = slice(g * 2 * PAIR, (g + 1) * 2 * PAIR)

        def proj(off):
            return _dot(h, win_ref[:, off + g * 2 * PAIR:off + (g + 1) * 2 * PAIR])

        def conv_taps(_):
            cu = proj(OFF_C) * proj(OFF_U)
            cu_s[SUBLANES:SUBLANES + tm, cols] = cu
            return (convw_ref[0:1, cols] * cu_s[SUBLANES - 2:SUBLANES - 2 + tm, cols]
                    + convw_ref[1:2, cols] * cu_s[SUBLANES - 1:SUBLANES - 1 + tm, cols]
                    + convw_ref[2:3, cols] * cu)

        stages = [conv_taps,
                  lambda conv: proj(OFF_B) * conv,
                  lambda conv: jax.nn.sigmoid(proj(OFF_GC)) * conv]
        attn_blocks = []
        conv = None
        for blk in range(nblk):
            logits_t = logits_next
            nxt = g * nblk + blk + 1
            if nxt < len(units):
                logits_next = qk(*units[nxt])
            if blk < len(stages):
                conv = stages[blk](conv)
            out = softmax_pv(blk, g, logits_t)
            attn_blocks.append(jnp.concatenate([out[0:BLOCK], out[BLOCK:2 * BLOCK]], axis=1))
        for stage in stages[nblk:]:
            conv = stage(conv)
        attn = jnp.concatenate(attn_blocks, axis=0)
        merged_s[:, cols] = (jax.nn.sigmoid(proj(OFF_GA)) * attn + conv).astype(BF16)

    o_ref[...] = x_ref[...] + _dot(merged_s[...], wout_ref[...])


def _mix(x, posq, posk, lut, sinks, gain, w_in, conv_w, w_out, *, seq_len):
    t = x.shape[0]
    tm = TOKEN_TILE
    nblk = tm // BLOCK
    k_scratch = pltpu.VMEM((SWA_KV_HEADS, BLOCK + tm, PAIR), BF16)
    vt_scratch = pltpu.VMEM((SWA_KV_HEADS, nblk + 1, PAIR, BLOCK), BF16)
    return pl.pallas_call(
        functools.partial(_mix_kernel, tiles_per_seq=seq_len // tm),
        grid=(t // tm,),
        in_specs=[
            pl.BlockSpec((tm, D_MODEL), lambda i: (i, 0)),
            pl.BlockSpec((1, nblk, BLOCK), lambda i: (i, 0, 0)),
            pl.BlockSpec((1, nblk, 2 * BLOCK), lambda i: (i, 0, 0)),
            _resident((SWA_HEADS, WINDOW)),
            pl.BlockSpec(memory_space=pltpu.SMEM),
            _resident((1, D_MODEL)),
            _resident((D_MODEL, IN_WIDTH)),
            _resident((CONV_K, D_MODEL)),
            _resident((D_MODEL, D_MODEL)),
        ],
        out_specs=pl.BlockSpec((tm, D_MODEL), lambda i: (i, 0)),
        out_shape=jax.ShapeDtypeStruct((t, D_MODEL), F32),
        scratch_shapes=[
            pltpu.VMEM((tm, Q_WIDTH), BF16),
            k_scratch, k_scratch, vt_scratch, vt_scratch,
            pltpu.VMEM((tm, D_MODEL), BF16),
            pltpu.VMEM((SUBLANES + tm, D_MODEL), F32),
            pltpu.VMEM((tm, D_MODEL), BF16),
            pltpu.VMEM((nblk + 1, 2 * BLOCK, BLOCK), jnp.int32),
            pltpu.VMEM((nblk + 1, SWA_HEADS, 2 * BLOCK, BLOCK), F32),
        ],
        compiler_params=_params(),
        name="mix",
    )(x, posq, posk, lut, sinks, gain, w_in, conv_w, w_out)


def _mem_kv_kernel(mem_ref, g_ref, wkv_ref, kv_ref):
    h = _rms(mem_ref[...], g_ref[...]).astype(BF16)
    kv_ref[...] = _dot(h, wkv_ref[...]).astype(BF16)


def _mem_kv(mem, gain, w_kv):
    rows = mem.shape[0]
    tm = TOKEN_TILE
    return pl.pallas_call(
        _mem_kv_kernel,
        grid=(rows // tm,),
        in_specs=[
            pl.BlockSpec((tm, D_MODEL), lambda i: (i, 0)),
            _resident((1, D_MODEL)),
            _resident((D_MODEL, 2 * D_MODEL)),
        ],
        out_specs=pl.BlockSpec((tm, 2 * D_MODEL), lambda i: (i, 0)),
        out_shape=jax.ShapeDtypeStruct((rows, 2 * D_MODEL), BF16),
        compiler_params=_params(),
        name="mem_kv",
    )(mem, gain, w_kv)


def _xattn_kernel(x_ref, xnext_ref, g_ref, wq_ref, kv_ref, wo_ref, o_ref, h_s, q_s):
    step = pl.program_id(0)
    cur, nxt = step % 2, (step + 1) % 2

    def head_cols(hd):
        return slice(hd * MEM_HEAD_DIM, (hd + 1) * MEM_HEAD_DIM)

    def q_proj(h, hd):
        return (_dot(h, wq_ref[:, head_cols(hd)]) * (MEM_HEAD_DIM ** -0.5)).astype(BF16)

    def logits(hd, q):
        return _dot_nt(q, kv_ref[0, :, head_cols(hd)])

    def context(hd, lg):
        e = jnp.exp(lg - jnp.max(lg, axis=1, keepdims=True))
        p = (e / jnp.sum(e, axis=1, keepdims=True)).astype(BF16)
        v = kv_ref[0, :, D_MODEL + hd * MEM_HEAD_DIM:D_MODEL + (hd + 1) * MEM_HEAD_DIM]
        return _dot(p, v).astype(BF16)

    def body(first):
        if first:
            h = _rms(x_ref[...], g_ref[...]).astype(BF16)
            qs = [q_proj(h, 0), q_proj(h, 1)]
        else:
            h = h_s[cur]
            qs = [q_s[cur, 0], q_s[cur, 1]]
        lgs = [logits(0, qs[0])]
        acc = x_ref[...]
        for hd in range(MEM_HEADS):
            if hd + 2 < MEM_HEADS:
                qs.append(q_proj(h, hd + 2))
            if hd + 1 < MEM_HEADS:
                lgs.append(logits(hd + 1, qs[hd + 1]))
            acc = acc + _dot(context(hd, lgs[hd]), wo_ref[head_cols(hd), :])
        h_next = _rms(xnext_ref[...], g_ref[...]).astype(BF16)
        h_s[nxt] = h_next
        q_s[nxt, 0] = q_proj(h_next, 0)
        q_s[nxt, 1] = q_proj(h_next, 1)
        o_ref[...] = acc

    @pl.when(step == 0)
    def _():
        body(True)

    @pl.when(step != 0)
    def _():
        body(False)


def _xattn(x, gain, w_q, kv, w_o, *, seq_len):
    t = x.shape[0]
    tm = XATTN_TOKEN_TILE
    tiles_per_seq = seq_len // tm
    return pl.pallas_call(
        _xattn_kernel,
        grid=(t // tm,),
        in_specs=[
            pl.BlockSpec((tm, D_MODEL), lambda i: (i, 0)),
            pl.BlockSpec((tm, D_MODEL), lambda i: (jnp.minimum(i + 1, t // tm - 1), 0)),
            _resident((1, D_MODEL)),
            _resident((D_MODEL, D_MODEL)),
            pl.BlockSpec((1, N_MEM, 2 * D_MODEL), lambda i: (i // tiles_per_seq, 0, 0)),
            _resident((D_MODEL, D_MODEL)),
        ],
        out_specs=pl.BlockSpec((tm, D_MODEL), lambda i: (i, 0)),
        out_shape=jax.ShapeDtypeStruct((t, D_MODEL), F32),
        scratch_shapes=[
            pltpu.VMEM((2, tm, D_MODEL), BF16),
            pltpu.VMEM((2, 2, tm, MEM_HEAD_DIM), BF16),
        ],
        compiler_params=_params(),
        name="xattn",
    )(x, x, gain, w_q, kv, w_o)


def kernel(x, mem, positions, rel_bias, ffn1_norm, ffn1_w_gu, ffn1_w_down, mix_norm, w_in, sinks, conv_w, w_out, xattn_norm, mem_norm, xattn_wq, xattn_wkv, xattn_wo, ffn2_norm, ffn2_w_gu, ffn2_w_down, final_norm):
    batch, seq_len, _ = x.shape
    depth = w_in.shape[0]
    tokens = batch * seq_len
    nb = seq_len // BLOCK

    xt = x.reshape(tokens, D_MODEL)
    posq = positions.reshape(tokens // TOKEN_TILE, TOKEN_TILE // BLOCK, BLOCK)
    pos_blocks = positions.reshape(batch, nb, BLOCK)
    prev = jnp.concatenate([jnp.full((batch, 1, BLOCK), POS_PAD, positions.dtype), pos_blocks[:, :-1]], axis=1)
    posk = jnp.concatenate([prev, pos_blocks], axis=2).reshape(tokens // TOKEN_TILE, TOKEN_TILE // BLOCK, 2 * BLOCK)
    lut = _bias_lut(rel_bias.T)
    final_gain = final_norm.reshape(1, D_MODEL)

    for l in range(depth):
        xt = _ffn(xt, ffn1_norm[l].reshape(1, D_MODEL), ffn1_w_gu[l], ffn1_w_down[l],
                  final_gain, final_norm=False)
        xt = _mix(xt, posq, posk, lut, sinks[l], mix_norm[l].reshape(1, D_MODEL),
                  w_in[l], conv_w[l], w_out[l], seq_len=seq_len)
        kv = _mem_kv(mem.reshape(batch * N_MEM, D_MODEL), mem_norm[l].reshape(1, D_MODEL), xattn_wkv[l])
        xt = _xattn(xt, xattn_norm[l].reshape(1, D_MODEL), xattn_wq[l],
                    kv.reshape(batch, N_MEM, 2 * D_MODEL), xattn_wo[l], seq_len=seq_len)
        xt = _ffn(xt, ffn2_norm[l].reshape(1, D_MODEL), ffn2_w_gu[l], ffn2_w_down[l],
                  final_gain, final_norm=(l == depth - 1))
    return xt.reshape(batch, seq_len, D_MODEL)
```

```python
import functools
import math

import jax
import jax.numpy as jnp
from jax import lax
from jax.experimental import pallas as pl
from jax.experimental.pallas import tpu as pltpu

D_MODEL = 1024
D_FF = 2816
N_MEM = 256
MEM_HEADS = 4
MEM_HEAD_DIM = D_MODEL // MEM_HEADS
HEAD_DIM = 64
SWA_HEADS = 16
SWA_KV_HEADS = 4
WINDOW = 128
BLOCK = 128
REL_BUCKETS = 32
REL_MAX_DIST = 128
CONV_K = 3
EPS = 1e-6
NEG_INF = -1e30
POS_PAD = 1 << 30

Q_WIDTH = SWA_HEADS * HEAD_DIM
KV_WIDTH = SWA_KV_HEADS * HEAD_DIM
OFF_Q = 0
OFF_K = OFF_Q + Q_WIDTH
OFF_V = OFF_K + KV_WIDTH
OFF_C = OFF_V + KV_WIDTH
OFF_B = OFF_C + D_MODEL
OFF_U = OFF_B + D_MODEL
OFF_GA = OFF_U + D_MODEL
OFF_GC = OFF_GA + D_MODEL
IN_WIDTH = OFF_GC + D_MODEL

SUBLANES = 8
MXU_COLS = 256
V7X_VMEM_BYTES = 64 * 1024 * 1024
VMEM_REQUEST_CAP = V7X_VMEM_BYTES - 4 * 1024 * 1024

TOKEN_TILE = 512
XATTN_TOKEN_TILE = 1024
FFN_TOKEN_TILE = 512
FF_CHUNK = MXU_COLS
PAIR = 2 * HEAD_DIM

BF16 = jnp.bfloat16
F32 = jnp.float32


def _rms(x, g):
    return x * lax.rsqrt(jnp.mean(x * x, axis=-1, keepdims=True) + EPS) * g


def _dot(a, b):
    return jnp.dot(a.astype(BF16), b.astype(BF16), preferred_element_type=F32)


def _dot_nt(a, b):
    return lax.dot_general(a.astype(BF16), b.astype(BF16), (((1,), (1,)), ((), ())),
                           preferred_element_type=F32)


def _resident(shape):
    return pl.BlockSpec(shape, lambda i: (0,) * len(shape), pipeline_mode=pl.Buffered(1))


def _nbytes(shape, dtype):
    return math.prod(shape) * jnp.dtype(dtype).itemsize


def _params(resident_bytes, per_step_bytes, tile_shape):
    request = resident_bytes + 2 * per_step_bytes + 4 * _nbytes(tile_shape, F32)
    return pltpu.CompilerParams(dimension_semantics=("arbitrary",),
                                vmem_limit_bytes=min(VMEM_REQUEST_CAP, request))


FF_CHUNKS = D_FF // FF_CHUNK


def _ffn_kernel(x_ref, xnext_ref, g_ref, wgu_hbm, wd_hbm, fg_ref, o_ref, wgu_v, wd_v, act_ref, h_s, act0_s, sems,
                *, final_norm):
    def gate_copy(j):
        cols = pl.ds(j * FF_CHUNK, FF_CHUNK)
        return pltpu.make_async_copy(wgu_hbm.at[:, cols], wgu_v.at[:, cols], sems.at[j])

    def up_copy(j):
        cols = pl.ds(D_FF + j * FF_CHUNK, FF_CHUNK)
        return pltpu.make_async_copy(wgu_hbm.at[:, cols], wgu_v.at[:, cols], sems.at[FF_CHUNKS + j])

    def down_copy(j):
        rows = pl.ds(j * FF_CHUNK, FF_CHUNK)
        return pltpu.make_async_copy(wd_hbm.at[rows, :], wd_v.at[rows, :], sems.at[2 * FF_CHUNKS + j])

    step = pl.program_id(0)
    cur, nxt = step % 2, (step + 1) % 2

    def chunk_act(h, j):
        lo = j * FF_CHUNK
        gate = _dot(h, wgu_v[:, lo:lo + FF_CHUNK])
        up = _dot(h, wgu_v[:, D_FF + lo:D_FF + lo + FF_CHUNK])
        return (gate * jax.nn.sigmoid(gate) * up).astype(BF16)

    def body(first):
        if first:
            h = _rms(x_ref[...], g_ref[...]).astype(BF16)
            gate_copy(0).wait()
            up_copy(0).wait()
            act_ref[:, 0:FF_CHUNK] = chunk_act(h, 0)
        else:
            h = h_s[cur]
            act_ref[:, 0:FF_CHUNK] = act0_s[cur]
        for j in range(1, FF_CHUNKS):
            if first:
                gate_copy(j).wait()
                up_copy(j).wait()
            act_ref[:, j * FF_CHUNK:(j + 1) * FF_CHUNK] = chunk_act(h, j)
        if first:
            for j in range(FF_CHUNKS):
                down_copy(j).wait()
        y = x_ref[...] + 0.5 * _dot(act_ref[...], wd_v[...])
        h_next = _rms(xnext_ref[...], g_ref[...]).astype(BF16)
        h_s[nxt] = h_next
        act0_s[nxt] = chunk_act(h_next, 0)
        if final_norm:
            y = _rms(y, fg_ref[...])
        o_ref[...] = y

    first_step = step == 0

    @pl.when(first_step)
    def _():
        for j in range(FF_CHUNKS):
            gate_copy(j).start()
            up_copy(j).start()
        for j in range(FF_CHUNKS):
            down_copy(j).start()
        body(True)

    @pl.when(jnp.logical_not(first_step))
    def _():
        body(False)


def _ffn(x, gain, w_gu, w_down, final_gain, *, final_norm):
    t = x.shape[0]
    tm = FFN_TOKEN_TILE
    scratch = [
        pltpu.VMEM((D_MODEL, 2 * D_FF), F32),
        pltpu.VMEM((D_FF, D_MODEL), F32),
        pltpu.VMEM((tm, D_FF), BF16),
        pltpu.VMEM((2, tm, D_MODEL), BF16),
        pltpu.VMEM((2, tm, FF_CHUNK), BF16),
    ]
    return pl.pallas_call(
        functools.partial(_ffn_kernel, final_norm=final_norm),
        grid=(t // tm,),
        in_specs=[
            pl.BlockSpec((tm, D_MODEL), lambda i: (i, 0)),
            pl.BlockSpec((tm, D_MODEL), lambda i: (jnp.minimum(i + 1, t // tm - 1), 0)),
            _resident((1, D_MODEL)),
            pl.BlockSpec(memory_space=pl.ANY),
            pl.BlockSpec(memory_space=pl.ANY),
            _resident((1, D_MODEL)),
        ],
        out_specs=pl.BlockSpec((tm, D_MODEL), lambda i: (i, 0)),
        out_shape=jax.ShapeDtypeStruct((t, D_MODEL), F32),
        scratch_shapes=scratch + [pltpu.SemaphoreType.DMA((3 * FF_CHUNKS,))],
        compiler_params=_params(sum(_nbytes(b.shape, b.dtype) for b in scratch),
                                3 * _nbytes((tm, D_MODEL), F32), (tm, D_MODEL)),
        name="ffn_final" if final_norm else "ffn",
    )(x, x, gain, w_gu, w_down, final_gain)


def _bias_lut_kernel(tab_ref, lut_ref):
    n = lax.broadcasted_iota(jnp.int32, (SWA_HEADS, WINDOW), 1).astype(F32)
    max_exact = REL_BUCKETS // 2
    nf = jnp.maximum(n, 1.0)
    large = max_exact + jnp.floor(jnp.log(nf / max_exact) / math.log(REL_MAX_DIST / max_exact)
                                  * (REL_BUCKETS - max_exact))
    large = jnp.minimum(large, REL_BUCKETS - 1.0)
    bucket = jnp.where(n < max_exact, n, large)
    lut = jnp.broadcast_to(tab_ref[:, 0:1], (SWA_HEADS, WINDOW))
    for b in range(1, REL_BUCKETS):
        lut = jnp.where(bucket >= b, tab_ref[:, b:b + 1], lut)
    lut_ref[...] = lut


def _bias_lut(rel_bias_t):
    return pl.pallas_call(
        _bias_lut_kernel,
        out_shape=jax.ShapeDtypeStruct((SWA_HEADS, WINDOW), F32),
        name="bias_lut",
    )(rel_bias_t)


def _key_head_slabs(t):
    lane = lax.broadcasted_iota(jnp.int32, (t.shape[0], PAIR), 1)
    low_half = lane < HEAD_DIM
    los, his = [], []
    for p in range(SWA_KV_HEADS // 2):
        pair = t[:, p * PAIR:(p + 1) * PAIR]
        swapped = pltpu.roll(pair, HEAD_DIM, axis=1)
        zero = jnp.zeros_like(pair)
        los += [jnp.where(low_half, pair, zero), jnp.where(low_half, swapped, zero)]
        his += [jnp.where(low_half, zero, swapped), jnp.where(low_half, zero, pair)]
    return [a.astype(BF16) for a in los], [a.astype(BF16) for a in his]


def _value_head_slabs_t(t):
    los, his = [], []
    for p in range(SWA_KV_HEADS // 2):
        pair_t = t[:, p * PAIR:(p + 1) * PAIR].T
        zero = jnp.zeros((HEAD_DIM, t.shape[0]), F32)
        for head_t in (pair_t[0:HEAD_DIM], pair_t[HEAD_DIM:PAIR]):
            los.append(jnp.concatenate([head_t, zero], axis=0))
            his.append(jnp.concatenate([zero, head_t], axis=0))
    return [a.astype(BF16) for a in los], [a.astype(BF16) for a in his]


def _mix_kernel(x_ref, posq_ref, posk_ref, lut_ref, sink_ref, g_ref, win_ref, convw_ref, wout_ref, o_ref,
                q_s, klo_s, khi_s, vtlo_s, vthi_s, h_s, cu_s, merged_s, relm_s, bias_s, *, tiles_per_seq):
    tm = x_ref.shape[0]
    nblk = tm // BLOCK
    step = pl.program_id(0)
    first = (step % tiles_per_seq) == 0

    @pl.when(first)
    def _():
        for s in (klo_s, khi_s):
            s[:, 0:BLOCK, :] = jnp.zeros((SWA_KV_HEADS, BLOCK, PAIR), BF16)
        for s in (vtlo_s, vthi_s):
            s[:, 0] = jnp.zeros((SWA_KV_HEADS, PAIR, BLOCK), BF16)
        cu_s[0:SUBLANES, :] = jnp.zeros((SUBLANES, D_MODEL), F32)

    @pl.when(jnp.logical_not(first))
    def _():
        for s in (klo_s, khi_s):
            s[:, 0:BLOCK, :] = s[:, tm:tm + BLOCK, :]
        for s in (vtlo_s, vthi_s):
            s[:, 0] = s[:, nblk]
        cu_s[0:SUBLANES, :] = cu_s[tm:tm + SUBLANES, :]

    @pl.when(step == 0)
    def _():
        relm_s[...] = jnp.full(relm_s.shape, -2, jnp.int32)

    slots = [jnp.where(first, nblk, 0)] + list(range(1, nblk))

    relms, misses = [], []
    for j in range(nblk):
        pos_k = jnp.concatenate(
            [jnp.broadcast_to(posk_ref[0, j:j + 1, c * BLOCK:(c + 1) * BLOCK], (BLOCK, BLOCK)).T
             for c in range(2)], axis=0)
        rel = posq_ref[0, j:j + 1, :] - pos_k
        relm = jnp.where((rel >= 0) & (rel < WINDOW), rel, -1)
        relms.append(relm)
        misses.append(jnp.sum((relm != relm_s[slots[j]]).astype(jnp.int32)))

    h_s[...] = _rms(x_ref[...], g_ref[...]).astype(BF16)
    h = h_s[...]

    k_lo, k_hi = _key_head_slabs(_dot(h, win_ref[:, OFF_K:OFF_K + KV_WIDTH]))
    vt_lo, vt_hi = _value_head_slabs_t(_dot(h, win_ref[:, OFF_V:OFF_V + KV_WIDTH]))
    for g in range(SWA_KV_HEADS):
        klo_s[g, BLOCK:BLOCK + tm, :] = k_lo[g]
        khi_s[g, BLOCK:BLOCK + tm, :] = k_hi[g]
        for j in range(nblk):
            vtlo_s[g, j + 1] = vt_lo[g][:, j * BLOCK:(j + 1) * BLOCK]
            vthi_s[g, j + 1] = vt_hi[g][:, j * BLOCK:(j + 1) * BLOCK]
    q_s[...] = (_dot(h, win_ref[:, OFF_Q:OFF_Q + Q_WIDTH]) * (HEAD_DIM ** -0.5)).astype(BF16)

    @pl.when(sum(misses) != 0)
    def _():
        for j in range(nblk):
            @pl.when(misses[j] != 0)
            def _(j=j):
                relm_s[slots[j]] = relms[j]
                shown = relms[j] >= 0
                idx = jnp.maximum(relms[j], 0)
                for hd in range(SWA_HEADS):
                    lut = jnp.broadcast_to(lut_ref[hd:hd + 1, :], (2 * BLOCK, WINDOW))
                    bias_s[slots[j], hd] = jnp.where(shown, jnp.take_along_axis(lut, idx, axis=1), NEG_INF)

    h = h_s[...]

    def qk(blk, g):
        rows = slice(blk * BLOCK, (blk + 1) * BLOCK)
        keys = slice(blk * BLOCK, (blk + 2) * BLOCK)
        c0 = g * 2 * PAIR
        q2 = jnp.concatenate([q_s[rows, c0:c0 + PAIR], q_s[rows, c0 + PAIR:c0 + 2 * PAIR]], axis=0)
        return _dot_nt(klo_s[g, keys, :], q2), _dot_nt(khi_s[g, keys, :], q2)

    def softmax_pv(blk, g, logits_t):
        probs_t = [[None, None], [None, None]]
        inv = [[None, None], [None, None]]
        for half in range(2):
            for pair in range(2):
                head = g * 4 + pair * 2 + half
                lt = logits_t[half][:, pair * BLOCK:(pair + 1) * BLOCK] + bias_s[slots[blk], head]
                sink = sink_ref[head]
                m = jnp.maximum(jnp.max(lt, axis=0, keepdims=True), sink)
                e = jnp.exp(lt - m)
                inv[half][pair] = 1.0 / (jnp.sum(e, axis=0, keepdims=True) + jnp.exp(sink - m))
                probs_t[half][pair] = e.astype(BF16)
        v_lo_t = jnp.concatenate([vtlo_s[g, blk], vtlo_s[g, blk + 1]], axis=1)
        v_hi_t = jnp.concatenate([vthi_s[g, blk], vthi_s[g, blk + 1]], axis=1)
        out_t = (_dot(v_lo_t, jnp.concatenate(probs_t[0], axis=1))
                 + _dot(v_hi_t, jnp.concatenate(probs_t[1], axis=1)))
        scale = jnp.concatenate(
            [jnp.broadcast_to(jnp.concatenate(inv[0], axis=1), (HEAD_DIM, 2 * BLOCK)),
             jnp.broadcast_to(jnp.concatenate(inv[1], axis=1), (HEAD_DIM, 2 * BLOCK))], axis=0)
        return (out_t * scale).T

    units = [(blk, g) for g in range(SWA_KV_HEADS) for blk in range(nblk)]
    logits_next = qk(*units[0])
    for g in range(SWA_KV_HEADS):
        cols = slice(g * 2 * PAIR, (g + 1) * 2 * PAIR)

        def proj(off):
            return _dot(h, win_ref[:, off + g * 2 * PAIR:off + (g + 1) * 2 * PAIR])

        def conv_taps(_):
            cu = proj(OFF_C) * proj(OFF_U)
            cu_s[SUBLANES:SUBLANES + tm, cols] = cu
            return (convw_ref[0:1, cols] * cu_s[SUBLANES - 2:SUBLANES - 2 + tm, cols]
                    + convw_ref[1:2, cols] * cu_s[SUBLANES - 1:SUBLANES - 1 + tm, cols]
                    + convw_ref[2:3, cols] * cu)

        stages = [conv_taps,
                  lambda conv: proj(OFF_B) * conv,
                  lambda conv: jax.nn.sigmoid(proj(OFF_GC)) * conv]
        attn_blocks = []
        conv = None
        for blk in range(nblk):
            logits_t = logits_next
            nxt = g * nblk + blk + 1
            if nxt < len(units):
                logits_next = qk(*units[nxt])
            if blk < len(stages):
                conv = stages[blk](conv)
            out = softmax_pv(blk, g, logits_t)
            attn_blocks.append(jnp.concatenate([out[0:BLOCK], out[BLOCK:2 * BLOCK]], axis=1))
        for stage in stages[nblk:]:
            conv = stage(conv)
        attn = jnp.concatenate(attn_blocks, axis=0)
        merged_s[:, cols] = (jax.nn.sigmoid(proj(OFF_GA)) * attn + conv).astype(BF16)

    o_ref[...] = x_ref[...] + _dot(merged_s[...], wout_ref[...])


def _mix(x, posq, posk, lut, sinks, gain, w_in, conv_w, w_out, *, seq_len):
    t = x.shape[0]
    tm = TOKEN_TILE
    nblk = tm // BLOCK
    k_scratch = pltpu.VMEM((SWA_KV_HEADS, BLOCK + tm, PAIR), BF16)
    vt_scratch = pltpu.VMEM((SWA_KV_HEADS, nblk + 1, PAIR, BLOCK), BF16)
    scratch = [
        pltpu.VMEM((tm, Q_WIDTH), BF16),
        k_scratch, k_scratch, vt_scratch, vt_scratch,
        pltpu.VMEM((tm, D_MODEL), BF16),
        pltpu.VMEM((SUBLANES + tm, D_MODEL), F32),
        pltpu.VMEM((tm, D_MODEL), BF16),
        pltpu.VMEM((nblk + 1, 2 * BLOCK, BLOCK), jnp.int32),
        pltpu.VMEM((nblk + 1, SWA_HEADS, 2 * BLOCK, BLOCK), F32),
    ]
    return pl.pallas_call(
        functools.partial(_mix_kernel, tiles_per_seq=seq_len // tm),
        grid=(t // tm,),
        in_specs=[
            pl.BlockSpec((tm, D_MODEL), lambda i: (i, 0)),
            pl.BlockSpec((1, nblk, BLOCK), lambda i: (i, 0, 0)),
            pl.BlockSpec((1, nblk, 2 * BLOCK), lambda i: (i, 0, 0)),
            _resident((SWA_HEADS, WINDOW)),
            pl.BlockSpec(memory_space=pltpu.SMEM),
            _resident((1, D_MODEL)),
            _resident((D_MODEL, IN_WIDTH)),
            _resident((CONV_K, D_MODEL)),
            _resident((D_MODEL, D_MODEL)),
        ],
        out_specs=pl.BlockSpec((tm, D_MODEL), lambda i: (i, 0)),
        out_shape=jax.ShapeDtypeStruct((t, D_MODEL), F32),
        scratch_shapes=scratch,
        compiler_params=_params(
            sum(_nbytes(b.shape, b.dtype) for b in scratch) + _nbytes(w_in.shape, F32) + _nbytes(w_out.shape, F32),
            2 * _nbytes((tm, D_MODEL), F32), (tm, D_MODEL)),
        name="mix",
    )(x, posq, posk, lut, sinks, gain, w_in, conv_w, w_out)


def _mem_kv_kernel(mem_ref, g_ref, wkv_ref, kv_ref):
    h = _rms(mem_ref[...], g_ref[...]).astype(BF16)
    kv_ref[...] = _dot(h, wkv_ref[...]).astype(BF16)


def _mem_kv(mem, gain, w_kv):
    rows = mem.shape[0]
    tm = XATTN_TOKEN_TILE
    return pl.pallas_call(
        _mem_kv_kernel,
        grid=(rows // tm,),
        in_specs=[
            pl.BlockSpec((tm, D_MODEL), lambda i: (i, 0)),
            _resident((1, D_MODEL)),
            _resident((D_MODEL, 2 * D_MODEL)),
        ],
        out_specs=pl.BlockSpec((tm, 2 * D_MODEL), lambda i: (i, 0)),
        out_shape=jax.ShapeDtypeStruct((rows, 2 * D_MODEL), BF16),
        compiler_params=_params(_nbytes(w_kv.shape, F32),
                                _nbytes((tm, D_MODEL), F32) + _nbytes((tm, 2 * D_MODEL), BF16), (tm, 2 * D_MODEL)),
        name="mem_kv",
    )(mem, gain, w_kv)


def _xattn_kernel(x_ref, xnext_ref, g_ref, wq_ref, kv_ref, wo_ref, o_ref, h_s, q_s):
    step = pl.program_id(0)
    cur, nxt = step % 2, (step + 1) % 2

    def head_cols(hd):
        return slice(hd * MEM_HEAD_DIM, (hd + 1) * MEM_HEAD_DIM)

    def q_proj(h, hd):
        return (_dot(h, wq_ref[:, head_cols(hd)]) * (MEM_HEAD_DIM ** -0.5)).astype(BF16)

    def logits(hd, q):
        return _dot_nt(q, kv_ref[0, :, head_cols(hd)])

    def context(hd, lg):
        e = jnp.exp(lg - jnp.max(lg, axis=1, keepdims=True))
        p = (e / jnp.sum(e, axis=1, keepdims=True)).astype(BF16)
        v = kv_ref[0, :, D_MODEL + hd * MEM_HEAD_DIM:D_MODEL + (hd + 1) * MEM_HEAD_DIM]
        return _dot(p, v).astype(BF16)

    def body(first):
        if first:
            h = _rms(x_ref[...], g_ref[...]).astype(BF16)
            qs = [q_proj(h, 0), q_proj(h, 1)]
        else:
            h = h_s[cur]
            qs = [q_s[cur, 0], q_s[cur, 1]]
        lgs = [logits(0, qs[0])]
        acc = x_ref[...]
        for hd in range(MEM_HEADS):
            if hd + 2 < MEM_HEADS:
                qs.append(q_proj(h, hd + 2))
            if hd + 1 < MEM_HEADS:
                lgs.append(logits(hd + 1, qs[hd + 1]))
            acc = acc + _dot(context(hd, lgs[hd]), wo_ref[head_cols(hd), :])
        h_next = _rms(xnext_ref[...], g_ref[...]).astype(BF16)
        h_s[nxt] = h_next
        q_s[nxt, 0] = q_proj(h_next, 0)
        q_s[nxt, 1] = q_proj(h_next, 1)
        o_ref[...] = acc

    @pl.when(step == 0)
    def _():
        body(True)

    @pl.when(step != 0)
    def _():
        body(False)


def _xattn(x, gain, w_q, kv, w_o, *, seq_len):
    t = x.shape[0]
    tm = XATTN_TOKEN_TILE
    tiles_per_seq = seq_len // tm
    scratch = [pltpu.VMEM((2, tm, D_MODEL), BF16), pltpu.VMEM((2, 2, tm, MEM_HEAD_DIM), BF16)]
    return pl.pallas_call(
        _xattn_kernel,
        grid=(t // tm,),
        in_specs=[
            pl.BlockSpec((tm, D_MODEL), lambda i: (i, 0)),
            pl.BlockSpec((tm, D_MODEL), lambda i: (jnp.minimum(i + 1, t // tm - 1), 0)),
            _resident((1, D_MODEL)),
            _resident((D_MODEL, D_MODEL)),
            pl.BlockSpec((1, N_MEM, 2 * D_MODEL), lambda i: (i // tiles_per_seq, 0, 0)),
            _resident((D_MODEL, D_MODEL)),
        ],
        out_specs=pl.BlockSpec((tm, D_MODEL), lambda i: (i, 0)),
        out_shape=jax.ShapeDtypeStruct((t, D_MODEL), F32),
        scratch_shapes=scratch,
        compiler_params=_params(
            sum(_nbytes(b.shape, b.dtype) for b in scratch) + _nbytes(w_q.shape, F32) + _nbytes(w_o.shape, F32),
            3 * _nbytes((tm, D_MODEL), F32) + _nbytes((N_MEM, 2 * D_MODEL), BF16), (tm, D_MODEL)),
        name="xattn",
    )(x, x, gain, w_q, kv, w_o)


def kernel(x, mem, positions, rel_bias, ffn1_norm, ffn1_w_gu, ffn1_w_down, mix_norm, w_in, sinks, conv_w, w_out, xattn_norm, mem_norm, xattn_wq, xattn_wkv, xattn_wo, ffn2_norm, ffn2_w_gu, ffn2_w_down, final_norm):
    batch, seq_len, _ = x.shape
    depth = w_in.shape[0]
    tokens = batch * seq_len
    nb = seq_len // BLOCK

    xt = x.reshape(tokens, D_MODEL)
    posq = positions.reshape(tokens // TOKEN_TILE, TOKEN_TILE // BLOCK, BLOCK)
    pos_blocks = positions.reshape(batch, nb, BLOCK)
    prev = jnp.concatenate([jnp.full((batch, 1, BLOCK), POS_PAD, positions.dtype), pos_blocks[:, :-1]], axis=1)
    posk = jnp.concatenate([prev, pos_blocks], axis=2).reshape(tokens // TOKEN_TILE, TOKEN_TILE // BLOCK, 2 * BLOCK)
    lut = _bias_lut(rel_bias.T)
    final_gain = final_norm.reshape(1, D_MODEL)

    for l in range(depth):
        xt = _ffn(xt, ffn1_norm[l].reshape(1, D_MODEL), ffn1_w_gu[l], ffn1_w_down[l],
                  final_gain, final_norm=False)
        xt = _mix(xt, posq, posk, lut, sinks[l], mix_norm[l].reshape(1, D_MODEL),
                  w_in[l], conv_w[l], w_out[l], seq_len=seq_len)
        kv = _mem_kv(mem.reshape(batch * N_MEM, D_MODEL), mem_norm[l].reshape(1, D_MODEL), xattn_wkv[l])
        xt = _xattn(xt, xattn_norm[l].reshape(1, D_MODEL), xattn_wq[l],
                    kv.reshape(batch, N_MEM, 2 * D_MODEL), xattn_wo[l], seq_len=seq_len)
        xt = _ffn(xt, ffn2_norm[l].reshape(1, D_MODEL), ffn2_w_gu[l], ffn2_w_down[l],
                  final_gain, final_norm=(l == depth - 1))
    return xt.reshape(batch, seq_len, D_MODEL)
```

```python
import functools
import math

import jax
import jax.numpy as jnp
from jax import lax
from jax.experimental import pallas as pl
from jax.experimental.pallas import tpu as pltpu

D_MODEL = 1024
D_FF = 2816
N_MEM = 256
MEM_HEADS = 4
MEM_HEAD_DIM = D_MODEL // MEM_HEADS
HEAD_DIM = 64
SWA_HEADS = 16
SWA_KV_HEADS = 4
WINDOW = 128
BLOCK = 128
REL_BUCKETS = 32
REL_MAX_DIST = 128
CONV_K = 3
EPS = 1e-6
NEG_INF = -1e30
POS_PAD = 1 << 30

Q_WIDTH = SWA_HEADS * HEAD_DIM
KV_WIDTH = SWA_KV_HEADS * HEAD_DIM
OFF_Q = 0
OFF_K = OFF_Q + Q_WIDTH
OFF_V = OFF_K + KV_WIDTH
OFF_C = OFF_V + KV_WIDTH
OFF_B = OFF_C + D_MODEL
OFF_U = OFF_B + D_MODEL
OFF_GA = OFF_U + D_MODEL
OFF_GC = OFF_GA + D_MODEL
IN_WIDTH = OFF_GC + D_MODEL

SUBLANES = 8
MXU_COLS = 256
V7X_VMEM_BYTES = 64 * 1024 * 1024
VMEM_REQUEST_CAP = V7X_VMEM_BYTES - 4 * 1024 * 1024

TOKEN_TILE = 512
XATTN_TOKEN_TILE = 1024
FFN_TOKEN_TILE = 1024
FF_CHUNK = MXU_COLS
PAIR = 2 * HEAD_DIM

BF16 = jnp.bfloat16
F32 = jnp.float32


def _rms(x, g):
    return x * lax.rsqrt(jnp.mean(x * x, axis=-1, keepdims=True) + EPS) * g


def _dot(a, b):
    return jnp.dot(a.astype(BF16), b.astype(BF16), preferred_element_type=F32)


def _dot_nt(a, b):
    return lax.dot_general(a.astype(BF16), b.astype(BF16), (((1,), (1,)), ((), ())),
                           preferred_element_type=F32)


def _resident(shape):
    return pl.BlockSpec(shape, lambda i: (0,) * len(shape), pipeline_mode=pl.Buffered(1))


def _nbytes(shape, dtype):
    return math.prod(shape) * jnp.dtype(dtype).itemsize


def _params(resident_bytes, per_step_bytes, tile_shape):
    request = resident_bytes + 2 * per_step_bytes + 4 * _nbytes(tile_shape, F32)
    return pltpu.CompilerParams(dimension_semantics=("arbitrary",),
                                vmem_limit_bytes=min(VMEM_REQUEST_CAP, request))


FF_CHUNKS = D_FF // FF_CHUNK
OUT_CHUNKS = D_MODEL // MXU_COLS


def _ffn_kernel(x_ref, xnext_ref, g_ref, wgu_hbm, wd_hbm, fg_ref, o_ref,
                wgu_v, wd_v, act_ref, h_s, act0_s, gate_stage, up_stage, down_stage, sems, *, final_norm):
    def gate_copy(j):
        return pltpu.make_async_copy(wgu_hbm.at[:, pl.ds(j * FF_CHUNK, FF_CHUNK)], gate_stage, sems.at[0])

    def up_copy(j):
        return pltpu.make_async_copy(wgu_hbm.at[:, pl.ds(D_FF + j * FF_CHUNK, FF_CHUNK)], up_stage, sems.at[1])

    def down_copy(j):
        return pltpu.make_async_copy(wd_hbm.at[pl.ds(j * FF_CHUNK, FF_CHUNK), :], down_stage, sems.at[2])

    def start_chunk(j):
        gate_copy(j).start()
        up_copy(j).start()
        down_copy(j).start()

    def land_chunk(j):
        gate_copy(j).wait()
        wgu_v[j] = gate_stage[...].astype(BF16)
        up_copy(j).wait()
        wgu_v[FF_CHUNKS + j] = up_stage[...].astype(BF16)
        down_copy(j).wait()
        for c in range(OUT_CHUNKS):
            wd_v[c, j * FF_CHUNK:(j + 1) * FF_CHUNK, :] = down_stage[:, c * MXU_COLS:(c + 1) * MXU_COLS].astype(BF16)
        if j + 1 < FF_CHUNKS:
            start_chunk(j + 1)

    step = pl.program_id(0)
    cur, nxt = step % 2, (step + 1) % 2

    def chunk_act(h, j):
        gate = _dot(h, wgu_v[j])
        up = _dot(h, wgu_v[FF_CHUNKS + j])
        return (gate * jax.nn.sigmoid(gate) * up).astype(BF16)

    def body(first):
        if first:
            h = _rms(x_ref[...], g_ref[...]).astype(BF16)
        else:
            h = h_s[...]
            act_ref[:, 0:FF_CHUNK] = act0_s[cur]
        for j in range(0 if first else 1, FF_CHUNKS):
            if first:
                land_chunk(j)
            act_ref[:, j * FF_CHUNK:(j + 1) * FF_CHUNK] = chunk_act(h, j)
        y = x_ref[...] + 0.5 * jnp.concatenate([_dot(act_ref[...], wd_v[c]) for c in range(OUT_CHUNKS)], axis=1)
        h_next = _rms(xnext_ref[...], g_ref[...]).astype(BF16)
        h_s[...] = h_next
        act0_s[nxt] = chunk_act(h_next, 0)
        if final_norm:
            y = _rms(y, fg_ref[...])
        o_ref[...] = y

    first_step = step == 0

    @pl.when(first_step)
    def _():
        start_chunk(0)
        body(True)

    @pl.when(jnp.logical_not(first_step))
    def _():
        body(False)


def _ffn(x, gain, w_gu, w_down, final_gain, *, final_norm):
    t = x.shape[0]
    tm = FFN_TOKEN_TILE
    scratch = [
        pltpu.VMEM((2 * FF_CHUNKS, D_MODEL, FF_CHUNK), BF16),
        pltpu.VMEM((OUT_CHUNKS, D_FF, MXU_COLS), BF16),
        pltpu.VMEM((tm, D_FF), BF16),
        pltpu.VMEM((tm, D_MODEL), BF16),
        pltpu.VMEM((2, tm, FF_CHUNK), BF16),
        pltpu.VMEM((D_MODEL, FF_CHUNK), F32),
        pltpu.VMEM((D_MODEL, FF_CHUNK), F32),
        pltpu.VMEM((FF_CHUNK, D_MODEL), F32),
    ]
    return pl.pallas_call(
        functools.partial(_ffn_kernel, final_norm=final_norm),
        grid=(t // tm,),
        in_specs=[
            pl.BlockSpec((tm, D_MODEL), lambda i: (i, 0)),
            pl.BlockSpec((tm, D_MODEL), lambda i: (jnp.minimum(i + 1, t // tm - 1), 0)),
            _resident((1, D_MODEL)),
            pl.BlockSpec(memory_space=pl.ANY),
            pl.BlockSpec(memory_space=pl.ANY),
            _resident((1, D_MODEL)),
        ],
        out_specs=pl.BlockSpec((tm, D_MODEL), lambda i: (i, 0)),
        out_shape=jax.ShapeDtypeStruct((t, D_MODEL), F32),
        scratch_shapes=scratch + [pltpu.SemaphoreType.DMA((3,))],
        compiler_params=_params(sum(_nbytes(b.shape, b.dtype) for b in scratch),
                                3 * _nbytes((tm, D_MODEL), F32), (tm, D_MODEL)),
        name="ffn_final" if final_norm else "ffn",
    )(x, x, gain, w_gu, w_down, final_gain)


def _bias_lut_kernel(tab_ref, lut_ref):
    n = lax.broadcasted_iota(jnp.int32, (SWA_HEADS, WINDOW), 1).astype(F32)
    max_exact = REL_BUCKETS // 2
    nf = jnp.maximum(n, 1.0)
    large = max_exact + jnp.floor(jnp.log(nf / max_exact) / math.log(REL_MAX_DIST / max_exact)
                                  * (REL_BUCKETS - max_exact))
    large = jnp.minimum(large, REL_BUCKETS - 1.0)
    bucket = jnp.where(n < max_exact, n, large)
    lut = jnp.broadcast_to(tab_ref[:, 0:1], (SWA_HEADS, WINDOW))
    for b in range(1, REL_BUCKETS):
        lut = jnp.where(bucket >= b, tab_ref[:, b:b + 1], lut)
    lut_ref[...] = lut


def _bias_lut(rel_bias_t):
    return pl.pallas_call(
        _bias_lut_kernel,
        out_shape=jax.ShapeDtypeStruct((SWA_HEADS, WINDOW), F32),
        name="bias_lut",
    )(rel_bias_t)


def _key_head_slabs(t):
    lane = lax.broadcasted_iota(jnp.int32, (t.shape[0], PAIR), 1)
    low_half = lane < HEAD_DIM
    los, his = [], []
    for p in range(SWA_KV_HEADS // 2):
        pair = t[:, p * PAIR:(p + 1) * PAIR]
        swapped = pltpu.roll(pair, HEAD_DIM, axis=1)
        zero = jnp.zeros_like(pair)
        los += [jnp.where(low_half, pair, zero), jnp.where(low_half, swapped, zero)]
        his += [jnp.where(low_half, zero, swapped), jnp.where(low_half, zero, pair)]
    return [a.astype(BF16) for a in los], [a.astype(BF16) for a in his]


def _value_head_slabs_t(t):
    los, his = [], []
    for p in range(SWA_KV_HEADS // 2):
        pair_t = t[:, p * PAIR:(p + 1) * PAIR].T
        zero = jnp.zeros((HEAD_DIM, t.shape[0]), F32)
        for head_t in (pair_t[0:HEAD_DIM], pair_t[HEAD_DIM:PAIR]):
            los.append(jnp.concatenate([head_t, zero], axis=0))
            his.append(jnp.concatenate([zero, head_t], axis=0))
    return [a.astype(BF16) for a in los], [a.astype(BF16) for a in his]


def _mix_kernel(x_ref, posq_ref, posk_ref, lut_ref, sink_ref, g_ref, win_ref, convw_ref, wout_ref, o_ref,
                q_s, klo_s, khi_s, vtlo_s, vthi_s, h_s, cu_s, merged_s, relm_s, bias_s, *, tiles_per_seq):
    tm = x_ref.shape[0]
    nblk = tm // BLOCK
    step = pl.program_id(0)
    first = (step % tiles_per_seq) == 0

    @pl.when(first)
    def _():
        for s in (klo_s, khi_s):
            s[:, 0:BLOCK, :] = jnp.zeros((SWA_KV_HEADS, BLOCK, PAIR), BF16)
        for s in (vtlo_s, vthi_s):
            s[:, 0] = jnp.zeros((SWA_KV_HEADS, PAIR, BLOCK), BF16)
        cu_s[0:SUBLANES, :] = jnp.zeros((SUBLANES, D_MODEL), F32)

    @pl.when(jnp.logical_not(first))
    def _():
        for s in (klo_s, khi_s):
            s[:, 0:BLOCK, :] = s[:, tm:tm + BLOCK, :]
        for s in (vtlo_s, vthi_s):
            s[:, 0] = s[:, nblk]
        cu_s[0:SUBLANES, :] = cu_s[tm:tm + SUBLANES, :]

    @pl.when(step == 0)
    def _():
        relm_s[...] = jnp.full(relm_s.shape, -2, jnp.int32)

    slots = [jnp.where(first, nblk, 0)] + list(range(1, nblk))

    relms, misses = [], []
    for j in range(nblk):
        pos_k = jnp.concatenate(
            [jnp.broadcast_to(posk_ref[0, j:j + 1, c * BLOCK:(c + 1) * BLOCK], (BLOCK, BLOCK)).T
             for c in range(2)], axis=0)
        rel = posq_ref[0, j:j + 1, :] - pos_k
        relm = jnp.where((rel >= 0) & (rel < WINDOW), rel, -1)
        relms.append(relm)
        misses.append(jnp.sum((relm != relm_s[slots[j]]).astype(jnp.int32)))

    h_s[...] = _rms(x_ref[...], g_ref[...]).astype(BF16)
    h = h_s[...]

    k_lo, k_hi = _key_head_slabs(_dot(h, win_ref[:, OFF_K:OFF_K + KV_WIDTH]))
    vt_lo, vt_hi = _value_head_slabs_t(_dot(h, win_ref[:, OFF_V:OFF_V + KV_WIDTH]))
    for g in range(SWA_KV_HEADS):
        klo_s[g, BLOCK:BLOCK + tm, :] = k_lo[g]
        khi_s[g, BLOCK:BLOCK + tm, :] = k_hi[g]
        for j in range(nblk):
            vtlo_s[g, j + 1] = vt_lo[g][:, j * BLOCK:(j + 1) * BLOCK]
            vthi_s[g, j + 1] = vt_hi[g][:, j * BLOCK:(j + 1) * BLOCK]
    q_s[...] = (_dot(h, win_ref[:, OFF_Q:OFF_Q + Q_WIDTH]) * (HEAD_DIM ** -0.5)).astype(BF16)

    for j in range(nblk):
        @pl.when(misses[j] != 0)
        def _(j=j):
            relm_s[slots[j]] = relms[j]
            shown = relms[j] >= 0
            idx = jnp.maximum(relms[j], 0)
            for hd in range(SWA_HEADS):
                lut = jnp.broadcast_to(lut_ref[hd:hd + 1, :], (2 * BLOCK, WINDOW))
                bias_s[slots[j], hd] = jnp.where(shown, jnp.take_along_axis(lut, idx, axis=1), NEG_INF)

    h = h_s[...]

    def qk(blk, g):
        rows = slice(blk * BLOCK, (blk + 1) * BLOCK)
        keys = slice(blk * BLOCK, (blk + 2) * BLOCK)
        c0 = g * 2 * PAIR
        q2 = jnp.concatenate([q_s[rows, c0:c0 + PAIR], q_s[rows, c0 + PAIR:c0 + 2 * PAIR]], axis=0)
        return _dot_nt(klo_s[g, keys, :], q2), _dot_nt(khi_s[g, keys, :], q2)

    def softmax_pv(blk, g, logits_t):
        probs_t = [[None, None], [None, None]]
        inv = [[None, None], [None, None]]
        for half in range(2):
            for pair in range(2):
                head = g * 4 + pair * 2 + half
                lt = logits_t[half][:, pair * BLOCK:(pair + 1) * BLOCK] + bias_s[slots[blk], head]
                sink = sink_ref[head]
                m = jnp.maximum(jnp.max(lt, axis=0, keepdims=True), sink)
                e = jnp.exp(lt - m)
                inv[half][pair] = 1.0 / (jnp.sum(e, axis=0, keepdims=True) + jnp.exp(sink - m))
                probs_t[half][pair] = e.astype(BF16)
        v_lo_t = jnp.concatenate([vtlo_s[g, blk], vtlo_s[g, blk + 1]], axis=1)
        v_hi_t = jnp.concatenate([vthi_s[g, blk], vthi_s[g, blk + 1]], axis=1)
        out_t = (_dot(v_lo_t, jnp.concatenate(probs_t[0], axis=1))
                 + _dot(v_hi_t, jnp.concatenate(probs_t[1], axis=1)))
        scale = jnp.concatenate(
            [jnp.broadcast_to(jnp.concatenate(inv[0], axis=1), (HEAD_DIM, 2 * BLOCK)),
             jnp.broadcast_to(jnp.concatenate(inv[1], axis=1), (HEAD_DIM, 2 * BLOCK))], axis=0)
        return (out_t * scale).T

    units = [(blk, g) for g in range(SWA_KV_HEADS) for blk in range(nblk)]
    logits_next = qk(*units[0])
    for g in range(SWA_KV_HEADS):
        cols = slice(g * 2 * PAIR, (g + 1) * 2 * PAIR)

        def proj(off):
            return _dot(h, win_ref[:, off + g * 2 * PAIR:off + (g + 1) * 2 * PAIR])

        def conv_taps(_):
            cu = proj(OFF_C) * proj(OFF_U)
            cu_s[SUBLANES:SUBLANES + tm, cols] = cu
            return (convw_ref[0:1, cols] * cu_s[SUBLANES - 2:SUBLANES - 2 + tm, cols]
                    + convw_ref[1:2, cols] * cu_s[SUBLANES - 1:SUBLANES - 1 + tm, cols]
                    + convw_ref[2:3, cols] * cu)

        stages = [conv_taps,
                  lambda conv: proj(OFF_B) * conv,
                  lambda conv: jax.nn.sigmoid(proj(OFF_GC)) * conv]
        attn_blocks = []
        conv = None
        for blk in range(nblk):
            logits_t = logits_next
            nxt = g * nblk + blk + 1
            if nxt < len(units):
                logits_next = qk(*units[nxt])
            if blk < len(stages):
                conv = stages[blk](conv)
            out = softmax_pv(blk, g, logits_t)
            attn_blocks.append(jnp.concatenate([out[0:BLOCK], out[BLOCK:2 * BLOCK]], axis=1))
        for stage in stages[nblk:]:
            conv = stage(conv)
        attn = jnp.concatenate(attn_blocks, axis=0)
        merged_s[:, cols] = (jax.nn.sigmoid(proj(OFF_GA)) * attn + conv).astype(BF16)

    o_ref[...] = x_ref[...] + _dot(merged_s[...], wout_ref[...])


def _mix(x, posq, posk, lut, sinks, gain, w_in, conv_w, w_out, *, seq_len):
    t = x.shape[0]
    tm = TOKEN_TILE
    nblk = tm // BLOCK
    k_scratch = pltpu.VMEM((SWA_KV_HEADS, BLOCK + tm, PAIR), BF16)
    vt_scratch = pltpu.VMEM((SWA_KV_HEADS, nblk + 1, PAIR, BLOCK), BF16)
    scratch = [
        pltpu.VMEM((tm, Q_WIDTH), BF16),
        k_scratch, k_scratch, vt_scratch, vt_scratch,
        pltpu.VMEM((tm, D_MODEL), BF16),
        pltpu.VMEM((SUBLANES + tm, D_MODEL), F32),
        pltpu.VMEM((tm, D_MODEL), BF16),
        pltpu.VMEM((nblk + 1, 2 * BLOCK, BLOCK), jnp.int32),
        pltpu.VMEM((nblk + 1, SWA_HEADS, 2 * BLOCK, BLOCK), F32),
    ]
    return pl.pallas_call(
        functools.partial(_mix_kernel, tiles_per_seq=seq_len // tm),
        grid=(t // tm,),
        in_specs=[
            pl.BlockSpec((tm, D_MODEL), lambda i: (i, 0)),
            pl.BlockSpec((1, nblk, BLOCK), lambda i: (i, 0, 0)),
            pl.BlockSpec((1, nblk, 2 * BLOCK), lambda i: (i, 0, 0)),
            _resident((SWA_HEADS, WINDOW)),
            pl.BlockSpec(memory_space=pltpu.SMEM),
            _resident((1, D_MODEL)),
            _resident((D_MODEL, IN_WIDTH)),
            _resident((CONV_K, D_MODEL)),
            _resident((D_MODEL, D_MODEL)),
        ],
        out_specs=pl.BlockSpec((tm, D_MODEL), lambda i: (i, 0)),
        out_shape=jax.ShapeDtypeStruct((t, D_MODEL), F32),
        scratch_shapes=scratch,
        compiler_params=_params(
            sum(_nbytes(b.shape, b.dtype) for b in scratch) + _nbytes(w_in.shape, F32) + _nbytes(w_out.shape, F32),
            2 * _nbytes((tm, D_MODEL), F32), (tm, D_MODEL)),
        name="mix",
    )(x, posq, posk, lut, sinks, gain, w_in, conv_w, w_out)


def _mem_kv_kernel(mem_ref, g_ref, wkv_ref, kv_ref):
    h = _rms(mem_ref[...], g_ref[...]).astype(BF16)
    kv_ref[...] = _dot(h, wkv_ref[...]).astype(BF16)


def _mem_kv(mem, gain, w_kv):
    rows = mem.shape[0]
    tm = TOKEN_TILE
    return pl.pallas_call(
        _mem_kv_kernel,
        grid=(rows // tm,),
        in_specs=[
            pl.BlockSpec((tm, D_MODEL), lambda i: (i, 0)),
            _resident((1, D_MODEL)),
            _resident((D_MODEL, 2 * D_MODEL)),
        ],
        out_specs=pl.BlockSpec((tm, 2 * D_MODEL), lambda i: (i, 0)),
        out_shape=jax.ShapeDtypeStruct((rows, 2 * D_MODEL), BF16),
        compiler_params=_params(_nbytes(w_kv.shape, F32),
                                _nbytes((tm, D_MODEL), F32) + _nbytes((tm, 2 * D_MODEL), BF16), (tm, 2 * D_MODEL)),
        name="mem_kv",
    )(mem, gain, w_kv)


def _xattn_kernel(x_ref, xnext_ref, g_ref, wq_ref, kv_ref, wo_ref, o_ref, h_s, q_s):
    step = pl.program_id(0)
    cur, nxt = step % 2, (step + 1) % 2

    def head_cols(hd):
        return slice(hd * MEM_HEAD_DIM, (hd + 1) * MEM_HEAD_DIM)

    def q_proj(h, hd):
        return (_dot(h, wq_ref[:, head_cols(hd)]) * (MEM_HEAD_DIM ** -0.5)).astype(BF16)

    def logits(hd, q):
        return _dot_nt(q, kv_ref[0, :, head_cols(hd)])

    def context(hd, lg):
        e = jnp.exp(lg - jnp.max(lg, axis=1, keepdims=True))
        p = (e / jnp.sum(e, axis=1, keepdims=True)).astype(BF16)
        v = kv_ref[0, :, D_MODEL + hd * MEM_HEAD_DIM:D_MODEL + (hd + 1) * MEM_HEAD_DIM]
        return _dot(p, v).astype(BF16)

    def body(first):
        if first:
            h = _rms(x_ref[...], g_ref[...]).astype(BF16)
            qs = [q_proj(h, 0), q_proj(h, 1)]
        else:
            h = h_s[cur]
            qs = [q_s[cur, 0], q_s[cur, 1]]
        lgs = [logits(0, qs[0])]
        acc = x_ref[...]
        for hd in range(MEM_HEADS):
            if hd + 2 < MEM_HEADS:
                qs.append(q_proj(h, hd + 2))
            if hd + 1 < MEM_HEADS:
                lgs.append(logits(hd + 1, qs[hd + 1]))
            acc = acc + _dot(context(hd, lgs[hd]), wo_ref[head_cols(hd), :])
        h_next = _rms(xnext_ref[...], g_ref[...]).astype(BF16)
        h_s[nxt] = h_next
        q_s[nxt, 0] = q_proj(h_next, 0)
        q_s[nxt, 1] = q_proj(h_next, 1)
        o_ref[...] = acc

    @pl.when(step == 0)
    def _():
        body(True)

    @pl.when(step != 0)
    def _():
        body(False)


def _xattn(x, gain, w_q, kv, w_o, *, seq_len):
    t = x.shape[0]
    tm = XATTN_TOKEN_TILE
    tiles_per_seq = seq_len // tm
    scratch = [pltpu.VMEM((2, tm, D_MODEL), BF16), pltpu.VMEM((2, 2, tm, MEM_HEAD_DIM), BF16)]
    return pl.pallas_call(
        _xattn_kernel,
        grid=(t // tm,),
        in_specs=[
            pl.BlockSpec((tm, D_MODEL), lambda i: (i, 0)),
            pl.BlockSpec((tm, D_MODEL), lambda i: (jnp.minimum(i + 1, t // tm - 1), 0)),
            _resident((1, D_MODEL)),
            _resident((D_MODEL, D_MODEL)),
            pl.BlockSpec((1, N_MEM, 2 * D_MODEL), lambda i: (i // tiles_per_seq, 0, 0)),
            _resident((D_MODEL, D_MODEL)),
        ],
        out_specs=pl.BlockSpec((tm, D_MODEL), lambda i: (i, 0)),
        out_shape=jax.ShapeDtypeStruct((t, D_MODEL), F32),
        scratch_shapes=scratch,
        compiler_params=_params(
            sum(_nbytes(b.shape, b.dtype) for b in scratch) + _nbytes(w_q.shape, F32) + _nbytes(w_o.shape, F32),
            3 * _nbytes((tm, D_MODEL), F32) + _nbytes((N_MEM, 2 * D_MODEL), BF16), (tm, D_MODEL)),
        name="xattn",
    )(x, x, gain, w_q, kv, w_o)


def kernel(x, mem, positions, rel_bias, ffn1_norm, ffn1_w_gu, ffn1_w_down, mix_norm, w_in, sinks, conv_w, w_out, xattn_norm, mem_norm, xattn_wq, xattn_wkv, xattn_wo, ffn2_norm, ffn2_w_gu, ffn2_w_down, final_norm):
    batch, seq_len, _ = x.shape
    depth = w_in.shape[0]
    tokens = batch * seq_len
    nb = seq_len // BLOCK

    xt = x.reshape(tokens, D_MODEL)
    posq = positions.reshape(tokens // TOKEN_TILE, TOKEN_TILE // BLOCK, BLOCK)
    pos_blocks = positions.reshape(batch, nb, BLOCK)
    prev = jnp.concatenate([jnp.full((batch, 1, BLOCK), POS_PAD, positions.dtype), pos_blocks[:, :-1]], axis=1)
    posk = jnp.concatenate([prev, pos_blocks], axis=2).reshape(tokens // TOKEN_TILE, TOKEN_TILE // BLOCK, 2 * BLOCK)
    lut = _bias_lut(rel_bias.T)
    final_gain = final_norm.reshape(1, D_MODEL)

    for l in range(depth):
        xt = _ffn(xt, ffn1_norm[l].reshape(1, D_MODEL), ffn1_w_gu[l], ffn1_w_down[l],
                  final_gain, final_norm=False)
        xt = _mix(xt, posq, posk, lut, sinks[l], mix_norm[l].reshape(1, D_MODEL),
                  w_in[l], conv_w[l], w_out[l], seq_len=seq_len)
        kv = _mem_kv(mem.reshape(batch * N_MEM, D_MODEL), mem_norm[l].reshape(1, D_MODEL), xattn_wkv[l])
        xt = _xattn(xt, xattn_norm[l].reshape(1, D_MODEL), xattn_wq[l],
                    kv.reshape(batch, N_MEM, 2 * D_MODEL), xattn_wo[l], seq_len=seq_len)
        xt = _ffn(xt, ffn2_norm[l].reshape(1, D_MODEL), ffn2_w_gu[l], ffn2_w_down[l],
                  final_gain, final_norm=(l == depth - 1))
    return xt.reshape(batch, seq_len, D_MODEL)
```

```python
import functools
import math

import jax
import jax.numpy as jnp
from jax import lax
from jax.experimental import pallas as pl
from jax.experimental.pallas import tpu as pltpu

D_MODEL = 1024
D_FF = 2816
N_MEM = 256
MEM_HEADS = 4
MEM_HEAD_DIM = D_MODEL // MEM_HEADS
HEAD_DIM = 64
SWA_HEADS = 16
SWA_KV_HEADS = 4
WINDOW = 128
BLOCK = 128
REL_BUCKETS = 32
REL_MAX_DIST = 128
CONV_K = 3
EPS = 1e-6
NEG_INF = -1e30
POS_PAD = 1 << 30

Q_WIDTH = SWA_HEADS * HEAD_DIM
KV_WIDTH = SWA_KV_HEADS * HEAD_DIM
OFF_Q = 0
OFF_K = OFF_Q + Q_WIDTH
OFF_V = OFF_K + KV_WIDTH
OFF_C = OFF_V + KV_WIDTH
OFF_B = OFF_C + D_MODEL
OFF_U = OFF_B + D_MODEL
OFF_GA = OFF_U + D_MODEL
OFF_GC = OFF_GA + D_MODEL
IN_WIDTH = OFF_GC + D_MODEL

SUBLANES = 8
MXU_COLS = 256
V7X_VMEM_BYTES = 64 * 1024 * 1024
VMEM_REQUEST_CAP = V7X_VMEM_BYTES - 4 * 1024 * 1024

TOKEN_TILE = 512
XATTN_TOKEN_TILE = 1024
FFN_TOKEN_TILE = 512
FF_CHUNK = MXU_COLS
PAIR = 2 * HEAD_DIM

BF16 = jnp.bfloat16
F32 = jnp.float32


def _rms(x, g):
    return x * lax.rsqrt(jnp.mean(x * x, axis=-1, keepdims=True) + EPS) * g


def _dot(a, b):
    return jnp.dot(a.astype(BF16), b.astype(BF16), preferred_element_type=F32)


def _dot_nt(a, b):
    return lax.dot_general(a.astype(BF16), b.astype(BF16), (((1,), (1,)), ((), ())),
                           preferred_element_type=F32)


def _resident(shape):
    return pl.BlockSpec(shape, lambda i: (0,) * len(shape), pipeline_mode=pl.Buffered(1))


def _nbytes(shape, dtype):
    return math.prod(shape) * jnp.dtype(dtype).itemsize


def _params(resident_bytes, per_step_bytes, tile_shape, tile_temporaries):
    request = resident_bytes + 2 * per_step_bytes + tile_temporaries * _nbytes(tile_shape, F32)
    return pltpu.CompilerParams(dimension_semantics=("arbitrary",),
                                vmem_limit_bytes=min(VMEM_REQUEST_CAP, request))


FF_CHUNKS = D_FF // FF_CHUNK
OUT_CHUNKS = D_MODEL // MXU_COLS


def _ffn_kernel(x_ref, xnext_ref, g_ref, wgu_hbm, wd_hbm, fg_ref, o_ref, wgu_v, wd_v, act_ref, h_s, act0_s, sems,
                *, final_norm):
    def gate_copy(j):
        return pltpu.make_async_copy(wgu_hbm.at[:, pl.ds(j * FF_CHUNK, FF_CHUNK)], wgu_v.at[j], sems.at[j])

    def up_copy(j):
        return pltpu.make_async_copy(wgu_hbm.at[:, pl.ds(D_FF + j * FF_CHUNK, FF_CHUNK)],
                                     wgu_v.at[FF_CHUNKS + j], sems.at[FF_CHUNKS + j])

    def down_copy(c):
        return pltpu.make_async_copy(wd_hbm.at[:, pl.ds(c * MXU_COLS, MXU_COLS)], wd_v.at[c],
                                     sems.at[2 * FF_CHUNKS + c])

    step = pl.program_id(0)
    cur, nxt = step % 2, (step + 1) % 2

    def chunk_act(h, j):
        gate = _dot(h, wgu_v[j])
        up = _dot(h, wgu_v[FF_CHUNKS + j])
        return (gate * jax.nn.sigmoid(gate) * up).astype(BF16)

    def body(first):
        if first:
            h = _rms(x_ref[...], g_ref[...]).astype(BF16)
            gate_copy(0).wait()
            up_copy(0).wait()
            act_ref[:, 0:FF_CHUNK] = chunk_act(h, 0)
        else:
            h = h_s[cur]
            act_ref[:, 0:FF_CHUNK] = act0_s[cur]
        for j in range(1, FF_CHUNKS):
            if first:
                gate_copy(j).wait()
                up_copy(j).wait()
            act_ref[:, j * FF_CHUNK:(j + 1) * FF_CHUNK] = chunk_act(h, j)
        if first:
            for c in range(OUT_CHUNKS):
                down_copy(c).wait()
        y = x_ref[...] + 0.5 * jnp.concatenate([_dot(act_ref[...], wd_v[c]) for c in range(OUT_CHUNKS)], axis=1)
        h_next = _rms(xnext_ref[...], g_ref[...]).astype(BF16)
        h_s[nxt] = h_next
        act0_s[nxt] = chunk_act(h_next, 0)
        if final_norm:
            y = _rms(y, fg_ref[...])
        o_ref[...] = y

    first_step = step == 0

    @pl.when(first_step)
    def _():
        for j in range(FF_CHUNKS):
            gate_copy(j).start()
            up_copy(j).start()
        for c in range(OUT_CHUNKS):
            down_copy(c).start()
        body(True)

    @pl.when(jnp.logical_not(first_step))
    def _():
        body(False)


def _ffn(x, gain, w_gu, w_down, final_gain, *, final_norm):
    t = x.shape[0]
    tm = FFN_TOKEN_TILE
    scratch = [
        pltpu.VMEM((2 * FF_CHUNKS, D_MODEL, FF_CHUNK), F32),
        pltpu.VMEM((OUT_CHUNKS, D_FF, MXU_COLS), F32),
        pltpu.VMEM((tm, D_FF), BF16),
        pltpu.VMEM((2, tm, D_MODEL), BF16),
        pltpu.VMEM((2, tm, FF_CHUNK), BF16),
    ]
    return pl.pallas_call(
        functools.partial(_ffn_kernel, final_norm=final_norm),
        grid=(t // tm,),
        in_specs=[
            pl.BlockSpec((tm, D_MODEL), lambda i: (i, 0)),
            pl.BlockSpec((tm, D_MODEL), lambda i: (jnp.minimum(i + 1, t // tm - 1), 0)),
            _resident((1, D_MODEL)),
            pl.BlockSpec(memory_space=pl.ANY),
            pl.BlockSpec(memory_space=pl.ANY),
            _resident((1, D_MODEL)),
        ],
        out_specs=pl.BlockSpec((tm, D_MODEL), lambda i: (i, 0)),
        out_shape=jax.ShapeDtypeStruct((t, D_MODEL), F32),
        scratch_shapes=scratch + [pltpu.SemaphoreType.DMA((2 * FF_CHUNKS + OUT_CHUNKS,))],
        compiler_params=_params(sum(_nbytes(b.shape, b.dtype) for b in scratch),
                                3 * _nbytes((tm, D_MODEL), F32), (tm, D_MODEL), 2),
        name="ffn_final" if final_norm else "ffn",
    )(x, x, gain, w_gu, w_down, final_gain)


def _bias_lut_kernel(tab_ref, lut_ref):
    n = lax.broadcasted_iota(jnp.int32, (SWA_HEADS, WINDOW), 1).astype(F32)
    max_exact = REL_BUCKETS // 2
    nf = jnp.maximum(n, 1.0)
    large = max_exact + jnp.floor(jnp.log(nf / max_exact) / math.log(REL_MAX_DIST / max_exact)
                                  * (REL_BUCKETS - max_exact))
    large = jnp.minimum(large, REL_BUCKETS - 1.0)
    bucket = jnp.where(n < max_exact, n, large)
    lut = jnp.broadcast_to(tab_ref[:, 0:1], (SWA_HEADS, WINDOW))
    for b in range(1, REL_BUCKETS):
        lut = jnp.where(bucket >= b, tab_ref[:, b:b + 1], lut)
    lut_ref[...] = lut


def _bias_lut(rel_bias_t):
    return pl.pallas_call(
        _bias_lut_kernel,
        out_shape=jax.ShapeDtypeStruct((SWA_HEADS, WINDOW), F32),
        name="bias_lut",
    )(rel_bias_t)


def _key_head_slabs(t):
    lane = lax.broadcasted_iota(jnp.int32, (t.shape[0], PAIR), 1)
    low_half = lane < HEAD_DIM
    los, his = [], []
    for p in range(SWA_KV_HEADS // 2):
        pair = t[:, p * PAIR:(p + 1) * PAIR]
        swapped = pltpu.roll(pair, HEAD_DIM, axis=1)
        zero = jnp.zeros_like(pair)
        los += [jnp.where(low_half, pair, zero), jnp.where(low_half, swapped, zero)]
        his += [jnp.where(low_half, zero, swapped), jnp.where(low_half, zero, pair)]
    return [a.astype(BF16) for a in los], [a.astype(BF16) for a in his]


def _value_head_slabs_t(t):
    los, his = [], []
    for p in range(SWA_KV_HEADS // 2):
        pair_t = t[:, p * PAIR:(p + 1) * PAIR].T
        zero = jnp.zeros((HEAD_DIM, t.shape[0]), F32)
        for head_t in (pair_t[0:HEAD_DIM], pair_t[HEAD_DIM:PAIR]):
            los.append(jnp.concatenate([head_t, zero], axis=0))
            his.append(jnp.concatenate([zero, head_t], axis=0))
    return [a.astype(BF16) for a in los], [a.astype(BF16) for a in his]


def _mix_kernel(x_ref, posq_ref, posk_ref, lut_ref, sink_ref, g_ref, win_ref, convw_ref, wout_ref, o_ref,
                q_s, klo_s, khi_s, vtlo_s, vthi_s, h_s, cu_s, merged_s, relm_s, bias_s, *, tiles_per_seq):
    tm = x_ref.shape[0]
    nblk = tm // BLOCK
    step = pl.program_id(0)
    first = (step % tiles_per_seq) == 0

    @pl.when(first)
    def _():
        for s in (klo_s, khi_s):
            s[:, 0:BLOCK, :] = jnp.zeros((SWA_KV_HEADS, BLOCK, PAIR), BF16)
        for s in (vtlo_s, vthi_s):
            s[:, 0] = jnp.zeros((SWA_KV_HEADS, PAIR, BLOCK), BF16)
        cu_s[0:SUBLANES, :] = jnp.zeros((SUBLANES, D_MODEL), F32)

    @pl.when(jnp.logical_not(first))
    def _():
        for s in (klo_s, khi_s):
            s[:, 0:BLOCK, :] = s[:, tm:tm + BLOCK, :]
        for s in (vtlo_s, vthi_s):
            s[:, 0] = s[:, nblk]
        cu_s[0:SUBLANES, :] = cu_s[tm:tm + SUBLANES, :]

    @pl.when(step == 0)
    def _():
        relm_s[...] = jnp.full(relm_s.shape, -2, jnp.int32)

    slots = [jnp.where(first, nblk, 0)] + list(range(1, nblk))

    relms, misses = [], []
    for j in range(nblk):
        pos_k = jnp.concatenate(
            [jnp.broadcast_to(posk_ref[0, j:j + 1, c * BLOCK:(c + 1) * BLOCK], (BLOCK, BLOCK)).T
             for c in range(2)], axis=0)
        rel = posq_ref[0, j:j + 1, :] - pos_k
        relm = jnp.where((rel >= 0) & (rel < WINDOW), rel, -1)
        relms.append(relm)
        misses.append(jnp.sum((relm != relm_s[slots[j]]).astype(jnp.int32)))

    h_s[...] = _rms(x_ref[...], g_ref[...]).astype(BF16)
    h = h_s[...]

    k_lo, k_hi = _key_head_slabs(_dot(h, win_ref[:, OFF_K:OFF_K + KV_WIDTH]))
    vt_lo, vt_hi = _value_head_slabs_t(_dot(h, win_ref[:, OFF_V:OFF_V + KV_WIDTH]))
    for g in range(SWA_KV_HEADS):
        klo_s[g, BLOCK:BLOCK + tm, :] = k_lo[g]
        khi_s[g, BLOCK:BLOCK + tm, :] = k_hi[g]
        for j in range(nblk):
            vtlo_s[g, j + 1] = vt_lo[g][:, j * BLOCK:(j + 1) * BLOCK]
            vthi_s[g, j + 1] = vt_hi[g][:, j * BLOCK:(j + 1) * BLOCK]
    q_s[...] = (_dot(h, win_ref[:, OFF_Q:OFF_Q + Q_WIDTH]) * (HEAD_DIM ** -0.5)).astype(BF16)

    for j in range(nblk):
        @pl.when(misses[j] != 0)
        def _(j=j):
            relm_s[slots[j]] = relms[j]
            shown = relms[j] >= 0
            idx = jnp.maximum(relms[j], 0)
            for hd in range(SWA_HEADS):
                lut = jnp.broadcast_to(lut_ref[hd:hd + 1, :], (2 * BLOCK, WINDOW))
                bias_s[slots[j], hd] = jnp.where(shown, jnp.take_along_axis(lut, idx, axis=1), NEG_INF)

    h = h_s[...]

    def qk(blk, g):
        rows = slice(blk * BLOCK, (blk + 1) * BLOCK)
        keys = slice(blk * BLOCK, (blk + 2) * BLOCK)
        c0 = g * 2 * PAIR
        q2 = jnp.concatenate([q_s[rows, c0:c0 + PAIR], q_s[rows, c0 + PAIR:c0 + 2 * PAIR]], axis=0)
        return _dot_nt(klo_s[g, keys, :], q2), _dot_nt(khi_s[g, keys, :], q2)

    def softmax_pv(blk, g, logits_t):
        probs_t = [[None, None], [None, None]]
        inv = [[None, None], [None, None]]
        for half in range(2):
            for pair in range(2):
                head = g * 4 + pair * 2 + half
                lt = logits_t[half][:, pair * BLOCK:(pair + 1) * BLOCK] + bias_s[slots[blk], head]
                sink = sink_ref[head]
                m = jnp.maximum(jnp.max(lt, axis=0, keepdims=True), sink)
                e = jnp.exp(lt - m)
                inv[half][pair] = 1.0 / (jnp.sum(e, axis=0, keepdims=True) + jnp.exp(sink - m))
                probs_t[half][pair] = e.astype(BF16)
        v_lo_t = jnp.concatenate([vtlo_s[g, blk], vtlo_s[g, blk + 1]], axis=1)
        v_hi_t = jnp.concatenate([vthi_s[g, blk], vthi_s[g, blk + 1]], axis=1)
        out_t = (_dot(v_lo_t, jnp.concatenate(probs_t[0], axis=1))
                 + _dot(v_hi_t, jnp.concatenate(probs_t[1], axis=1)))
        scale = jnp.concatenate(
            [jnp.broadcast_to(jnp.concatenate(inv[0], axis=1), (HEAD_DIM, 2 * BLOCK)),
             jnp.broadcast_to(jnp.concatenate(inv[1], axis=1), (HEAD_DIM, 2 * BLOCK))], axis=0)
        return (out_t * scale).T

    units = [(blk, g) for g in range(SWA_KV_HEADS) for blk in range(nblk)]
    logits_next = qk(*units[0])
    for g in range(SWA_KV_HEADS):
        cols = slice(g * 2 * PAIR, (g + 1) * 2 * PAIR)

        def proj(off):
            return _dot(h, win_ref[:, off + g * 2 * PAIR:off + (g + 1) * 2 * PAIR])

        def conv_taps(_):
            cu = proj(OFF_C) * proj(OFF_U)
            cu_s[SUBLANES:SUBLANES + tm, cols] = cu
            return (convw_ref[0:1, cols] * cu_s[SUBLANES - 2:SUBLANES - 2 + tm, cols]
                    + convw_ref[1:2, cols] * cu_s[SUBLANES - 1:SUBLANES - 1 + tm, cols]
                    + convw_ref[2:3, cols] * cu)

        stages = [conv_taps,
                  lambda conv: proj(OFF_B) * conv,
                  lambda conv: jax.nn.sigmoid(proj(OFF_GC)) * conv]
        attn_blocks = []
        conv = None
        for blk in range(nblk):
            logits_t = logits_next
            nxt = g * nblk + blk + 1
            if nxt < len(units):
                logits_next = qk(*units[nxt])
            if blk < len(stages):
                conv = stages[blk](conv)
            out = softmax_pv(blk, g, logits_t)
            attn_blocks.append(jnp.concatenate([out[0:BLOCK], out[BLOCK:2 * BLOCK]], axis=1))
        for stage in stages[nblk:]:
            conv = stage(conv)
        attn = jnp.concatenate(attn_blocks, axis=0)
        merged_s[:, cols] = (jax.nn.sigmoid(proj(OFF_GA)) * attn + conv).astype(BF16)

    o_ref[...] = x_ref[...] + _dot(merged_s[...], wout_ref[...])


def _mix(x, posq, posk, lut, sinks, gain, w_in, conv_w, w_out, *, seq_len):
    t = x.shape[0]
    tm = TOKEN_TILE
    nblk = tm // BLOCK
    k_scratch = pltpu.VMEM((SWA_KV_HEADS, BLOCK + tm, PAIR), BF16)
    vt_scratch = pltpu.VMEM((SWA_KV_HEADS, nblk + 1, PAIR, BLOCK), BF16)
    scratch = [
        pltpu.VMEM((tm, Q_WIDTH), BF16),
        k_scratch, k_scratch, vt_scratch, vt_scratch,
        pltpu.VMEM((tm, D_MODEL), BF16),
        pltpu.VMEM((SUBLANES + tm, D_MODEL), F32),
        pltpu.VMEM((tm, D_MODEL), BF16),
        pltpu.VMEM((nblk + 1, 2 * BLOCK, BLOCK), jnp.int32),
        pltpu.VMEM((nblk + 1, SWA_HEADS, 2 * BLOCK, BLOCK), F32),
    ]
    return pl.pallas_call(
        functools.partial(_mix_kernel, tiles_per_seq=seq_len // tm),
        grid=(t // tm,),
        in_specs=[
            pl.BlockSpec((tm, D_MODEL), lambda i: (i, 0)),
            pl.BlockSpec((1, nblk, BLOCK), lambda i: (i, 0, 0)),
            pl.BlockSpec((1, nblk, 2 * BLOCK), lambda i: (i, 0, 0)),
            _resident((SWA_HEADS, WINDOW)),
            pl.BlockSpec(memory_space=pltpu.SMEM),
            _resident((1, D_MODEL)),
            _resident((D_MODEL, IN_WIDTH)),
            _resident((CONV_K, D_MODEL)),
            _resident((D_MODEL, D_MODEL)),
        ],
        out_specs=pl.BlockSpec((tm, D_MODEL), lambda i: (i, 0)),
        out_shape=jax.ShapeDtypeStruct((t, D_MODEL), F32),
        scratch_shapes=scratch,
        compiler_params=_params(
            sum(_nbytes(b.shape, b.dtype) for b in scratch) + _nbytes(w_in.shape, F32) + _nbytes(w_out.shape, F32),
            2 * _nbytes((tm, D_MODEL), F32), (tm, D_MODEL), 4),
        name="mix",
    )(x, posq, posk, lut, sinks, gain, w_in, conv_w, w_out)


def _mem_kv_kernel(mem_ref, g_ref, wkv_ref, kv_ref):
    h = _rms(mem_ref[...], g_ref[...]).astype(BF16)
    kv_ref[...] = _dot(h, wkv_ref[...]).astype(BF16)


def _mem_kv(mem, gain, w_kv):
    rows = mem.shape[0]
    tm = TOKEN_TILE
    return pl.pallas_call(
        _mem_kv_kernel,
        grid=(rows // tm,),
        in_specs=[
            pl.BlockSpec((tm, D_MODEL), lambda i: (i, 0)),
            _resident((1, D_MODEL)),
            _resident((D_MODEL, 2 * D_MODEL)),
        ],
        out_specs=pl.BlockSpec((tm, 2 * D_MODEL), lambda i: (i, 0)),
        out_shape=jax.ShapeDtypeStruct((rows, 2 * D_MODEL), BF16),
        compiler_params=_params(_nbytes(w_kv.shape, F32),
                                _nbytes((tm, D_MODEL), F32) + _nbytes((tm, 2 * D_MODEL), BF16), (tm, 2 * D_MODEL), 2),
        name="mem_kv",
    )(mem, gain, w_kv)


def _xattn_kernel(x_ref, xnext_ref, g_ref, wq_ref, kv_ref, wo_ref, o_ref, h_s, q_s):
    step = pl.program_id(0)
    cur, nxt = step % 2, (step + 1) % 2

    def head_cols(hd):
        return slice(hd * MEM_HEAD_DIM, (hd + 1) * MEM_HEAD_DIM)

    def q_proj(h, hd):
        return (_dot(h, wq_ref[:, head_cols(hd)]) * (MEM_HEAD_DIM ** -0.5)).astype(BF16)

    def logits(hd, q):
        return _dot_nt(q, kv_ref[0, :, head_cols(hd)])

    def context(hd, lg):
        e = jnp.exp(lg - jnp.max(lg, axis=1, keepdims=True))
        p = (e / jnp.sum(e, axis=1, keepdims=True)).astype(BF16)
        v = kv_ref[0, :, D_MODEL + hd * MEM_HEAD_DIM:D_MODEL + (hd + 1) * MEM_HEAD_DIM]
        return _dot(p, v).astype(BF16)

    def body(first):
        if first:
            h = _rms(x_ref[...], g_ref[...]).astype(BF16)
            qs = [q_proj(h, 0), q_proj(h, 1)]
        else:
            h = h_s[cur]
            qs = [q_s[cur, 0], q_s[cur, 1]]
        lgs = [logits(0, qs[0])]
        acc = x_ref[...]
        for hd in range(MEM_HEADS):
            if hd + 2 < MEM_HEADS:
                qs.append(q_proj(h, hd + 2))
            if hd + 1 < MEM_HEADS:
                lgs.append(logits(hd + 1, qs[hd + 1]))
            acc = acc + _dot(context(hd, lgs[hd]), wo_ref[head_cols(hd), :])
        h_next = _rms(xnext_ref[...], g_ref[...]).astype(BF16)
        h_s[nxt] = h_next
        q_s[nxt, 0] = q_proj(h_next, 0)
        q_s[nxt, 1] = q_proj(h_next, 1)
        o_ref[...] = acc

    @pl.when(step == 0)
    def _():
        body(True)

    @pl.when(step != 0)
    def _():
        body(False)


def _xattn(x, gain, w_q, kv, w_o, *, seq_len):
    t = x.shape[0]
    tm = XATTN_TOKEN_TILE
    tiles_per_seq = seq_len // tm
    scratch = [pltpu.VMEM((2, tm, D_MODEL), BF16), pltpu.VMEM((2, 2, tm, MEM_HEAD_DIM), BF16)]
    return pl.pallas_call(
        _xattn_kernel,
        grid=(t // tm,),
        in_specs=[
            pl.BlockSpec((tm, D_MODEL), lambda i: (i, 0)),
            pl.BlockSpec((tm, D_MODEL), lambda i: (jnp.minimum(i + 1, t // tm - 1), 0)),
            _resident((1, D_MODEL)),
            _resident((D_MODEL, D_MODEL)),
            pl.BlockSpec((1, N_MEM, 2 * D_MODEL), lambda i: (i // tiles_per_seq, 0, 0)),
            _resident((D_MODEL, D_MODEL)),
        ],
        out_specs=pl.BlockSpec((tm, D_MODEL), lambda i: (i, 0)),
        out_shape=jax.ShapeDtypeStruct((t, D_MODEL), F32),
        scratch_shapes=scratch,
        compiler_params=_params(
            sum(_nbytes(b.shape, b.dtype) for b in scratch) + _nbytes(w_q.shape, F32) + _nbytes(w_o.shape, F32),
            3 * _nbytes((tm, D_MODEL), F32) + _nbytes((N_MEM, 2 * D_MODEL), BF16), (tm, D_MODEL), 3),
        name="xattn",
    )(x, x, gain, w_q, kv, w_o)


def kernel(x, mem, positions, rel_bias, ffn1_norm, ffn1_w_gu, ffn1_w_down, mix_norm, w_in, sinks, conv_w, w_out, xattn_norm, mem_norm, xattn_wq, xattn_wkv, xattn_wo, ffn2_norm, ffn2_w_gu, ffn2_w_down, final_norm):
    batch, seq_len, _ = x.shape
    depth = w_in.shape[0]
    tokens = batch * seq_len
    nb = seq_len // BLOCK

    xt = x.reshape(tokens, D_MODEL)
    posq = positions.reshape(tokens // TOKEN_TILE, TOKEN_TILE // BLOCK, BLOCK)
    pos_blocks = positions.reshape(batch, nb, BLOCK)
    prev = jnp.concatenate([jnp.full((batch, 1, BLOCK), POS_PAD, positions.dtype), pos_blocks[:, :-1]], axis=1)
    posk = jnp.concatenate([prev, pos_blocks], axis=2).reshape(tokens // TOKEN_TILE, TOKEN_TILE // BLOCK, 2 * BLOCK)
    lut = _bias_lut(rel_bias.T)
    final_gain = final_norm.reshape(1, D_MODEL)

    for l in range(depth):
        xt = _ffn(xt, ffn1_norm[l].reshape(1, D_MODEL), ffn1_w_gu[l], ffn1_w_down[l],
                  final_gain, final_norm=False)
        kv = _mem_kv(mem.reshape(batch * N_MEM, D_MODEL), mem_norm[l].reshape(1, D_MODEL), xattn_wkv[l])
        xt = _mix(xt, posq, posk, lut, sinks[l], mix_norm[l].reshape(1, D_MODEL),
                  w_in[l], conv_w[l], w_out[l], seq_len=seq_len)
        xt = _xattn(xt, xattn_norm[l].reshape(1, D_MODEL), xattn_wq[l],
                    kv.reshape(batch, N_MEM, 2 * D_MODEL), xattn_wo[l], seq_len=seq_len)
        xt = _ffn(xt, ffn2_norm[l].reshape(1, D_MODEL), ffn2_w_gu[l], ffn2_w_down[l],
                  final_gain, final_norm=(l == depth - 1))
    return xt.reshape(batch, seq_len, D_MODEL)
```

```python
import functools
import math

import jax
import jax.numpy as jnp
from jax import lax
from jax.experimental import pallas as pl
from jax.experimental.pallas import tpu as pltpu

D_MODEL = 1024
D_FF = 2816
N_MEM = 256
MEM_HEADS = 4
MEM_HEAD_DIM = D_MODEL // MEM_HEADS
HEAD_DIM = 64
SWA_HEADS = 16
SWA_KV_HEADS = 4
WINDOW = 128
BLOCK = 128
REL_BUCKETS = 32
REL_MAX_DIST = 128
CONV_K = 3
EPS = 1e-6
NEG_INF = -1e30
POS_PAD = 1 << 30

Q_WIDTH = SWA_HEADS * HEAD_DIM
KV_WIDTH = SWA_KV_HEADS * HEAD_DIM
OFF_Q = 0
OFF_K = OFF_Q + Q_WIDTH
OFF_V = OFF_K + KV_WIDTH
OFF_C = OFF_V + KV_WIDTH
OFF_B = OFF_C + D_MODEL
OFF_U = OFF_B + D_MODEL
OFF_GA = OFF_U + D_MODEL
OFF_GC = OFF_GA + D_MODEL
IN_WIDTH = OFF_GC + D_MODEL

SUBLANES = 8
MXU_COLS = 256
V7X_VMEM_BYTES = 64 * 1024 * 1024
VMEM_REQUEST_CAP = V7X_VMEM_BYTES - 4 * 1024 * 1024

TOKEN_TILE = 512
XATTN_TOKEN_TILE = 1024
FFN_TOKEN_TILE = 1024
FF_CHUNK = MXU_COLS
PAIR = 2 * HEAD_DIM

BF16 = jnp.bfloat16
F32 = jnp.float32


def _rms(x, g):
    return x * lax.rsqrt(jnp.mean(x * x, axis=-1, keepdims=True) + EPS) * g


def _dot(a, b):
    return jnp.dot(a.astype(BF16), b.astype(BF16), preferred_element_type=F32)


def _dot_nt(a, b):
    return lax.dot_general(a.astype(BF16), b.astype(BF16), (((1,), (1,)), ((), ())),
                           preferred_element_type=F32)


def _resident(shape):
    return pl.BlockSpec(shape, lambda i: (0,) * len(shape), pipeline_mode=pl.Buffered(1))


def _nbytes(shape, dtype):
    return math.prod(shape) * jnp.dtype(dtype).itemsize


def _params(resident_bytes, per_step_bytes, tile_shape, tile_temporaries):
    request = resident_bytes + 2 * per_step_bytes + tile_temporaries * _nbytes(tile_shape, F32)
    return pltpu.CompilerParams(dimension_semantics=("arbitrary",),
                                vmem_limit_bytes=min(VMEM_REQUEST_CAP, request))


FF_CHUNKS = D_FF // FF_CHUNK
OUT_CHUNKS = D_MODEL // MXU_COLS


STAGE_SLOTS = 2


def _ffn_kernel(x_ref, xnext_ref, g_ref, wgu_hbm, wd_hbm, fg_ref, o_ref,
                wgu_v, wd_v, act_ref, h_s, act0_s, gate_stage, up_stage, down_stage, sems, *, final_norm):
    def gate_copy(j):
        slot = j % STAGE_SLOTS
        return pltpu.make_async_copy(wgu_hbm.at[:, pl.ds(j * FF_CHUNK, FF_CHUNK)], gate_stage.at[slot],
                                     sems.at[0, slot])

    def up_copy(j):
        slot = j % STAGE_SLOTS
        return pltpu.make_async_copy(wgu_hbm.at[:, pl.ds(D_FF + j * FF_CHUNK, FF_CHUNK)], up_stage.at[slot],
                                     sems.at[1, slot])

    def down_copy(j):
        slot = j % STAGE_SLOTS
        return pltpu.make_async_copy(wd_hbm.at[pl.ds(j * FF_CHUNK, FF_CHUNK), :], down_stage.at[slot],
                                     sems.at[2, slot])

    def start_chunk(j):
        gate_copy(j).start()
        up_copy(j).start()
        down_copy(j).start()

    def land_chunk(j):
        slot = j % STAGE_SLOTS
        gate_copy(j).wait()
        wgu_v[j] = gate_stage[slot].astype(BF16)
        up_copy(j).wait()
        wgu_v[FF_CHUNKS + j] = up_stage[slot].astype(BF16)
        down_copy(j).wait()
        for c in range(OUT_CHUNKS):
            wd_v[c, j * FF_CHUNK:(j + 1) * FF_CHUNK, :] = (
                down_stage[slot, :, c * MXU_COLS:(c + 1) * MXU_COLS].astype(BF16))
        if j + STAGE_SLOTS < FF_CHUNKS:
            start_chunk(j + STAGE_SLOTS)

    step = pl.program_id(0)
    cur, nxt = step % 2, (step + 1) % 2

    def chunk_act(h, j):
        gate = _dot(h, wgu_v[j])
        up = _dot(h, wgu_v[FF_CHUNKS + j])
        return (gate * jax.nn.sigmoid(gate) * up).astype(BF16)

    def body(first):
        if first:
            h = _rms(x_ref[...], g_ref[...]).astype(BF16)
        else:
            h = h_s[...]
            act_ref[:, 0:FF_CHUNK] = act0_s[cur]
        for j in range(0 if first else 1, FF_CHUNKS):
            if first:
                land_chunk(j)
            act_ref[:, j * FF_CHUNK:(j + 1) * FF_CHUNK] = chunk_act(h, j)
        y = x_ref[...] + 0.5 * jnp.concatenate([_dot(act_ref[...], wd_v[c]) for c in range(OUT_CHUNKS)], axis=1)
        h_next = _rms(xnext_ref[...], g_ref[...]).astype(BF16)
        h_s[...] = h_next
        act0_s[nxt] = chunk_act(h_next, 0)
        if final_norm:
            y = _rms(y, fg_ref[...])
        o_ref[...] = y

    first_step = step == 0

    @pl.when(first_step)
    def _():
        for j in range(STAGE_SLOTS):
            start_chunk(j)
        body(True)

    @pl.when(jnp.logical_not(first_step))
    def _():
        body(False)


def _ffn(x, gain, w_gu, w_down, final_gain, *, final_norm):
    t = x.shape[0]
    tm = FFN_TOKEN_TILE
    scratch = [
        pltpu.VMEM((2 * FF_CHUNKS, D_MODEL, FF_CHUNK), BF16),
        pltpu.VMEM((OUT_CHUNKS, D_FF, MXU_COLS), BF16),
        pltpu.VMEM((tm, D_FF), BF16),
        pltpu.VMEM((tm, D_MODEL), BF16),
        pltpu.VMEM((2, tm, FF_CHUNK), BF16),
        pltpu.VMEM((STAGE_SLOTS, D_MODEL, FF_CHUNK), F32),
        pltpu.VMEM((STAGE_SLOTS, D_MODEL, FF_CHUNK), F32),
        pltpu.VMEM((STAGE_SLOTS, FF_CHUNK, D_MODEL), F32),
    ]
    return pl.pallas_call(
        functools.partial(_ffn_kernel, final_norm=final_norm),
        grid=(t // tm,),
        in_specs=[
            pl.BlockSpec((tm, D_MODEL), lambda i: (i, 0)),
            pl.BlockSpec((tm, D_MODEL), lambda i: (jnp.minimum(i + 1, t // tm - 1), 0)),
            _resident((1, D_MODEL)),
            pl.BlockSpec(memory_space=pl.ANY),
            pl.BlockSpec(memory_space=pl.ANY),
            _resident((1, D_MODEL)),
        ],
        out_specs=pl.BlockSpec((tm, D_MODEL), lambda i: (i, 0)),
        out_shape=jax.ShapeDtypeStruct((t, D_MODEL), F32),
        scratch_shapes=scratch + [pltpu.SemaphoreType.DMA((3, STAGE_SLOTS))],
        compiler_params=_params(sum(_nbytes(b.shape, b.dtype) for b in scratch),
                                3 * _nbytes((tm, D_MODEL), F32), (tm, D_MODEL), 2),
        name="ffn_final" if final_norm else "ffn",
    )(x, x, gain, w_gu, w_down, final_gain)


def _bias_lut_kernel(tab_ref, lut_ref):
    n = lax.broadcasted_iota(jnp.int32, (SWA_HEADS, WINDOW), 1).astype(F32)
    max_exact = REL_BUCKETS // 2
    nf = jnp.maximum(n, 1.0)
    large = max_exact + jnp.floor(jnp.log(nf / max_exact) / math.log(REL_MAX_DIST / max_exact)
                                  * (REL_BUCKETS - max_exact))
    large = jnp.minimum(large, REL_BUCKETS - 1.0)
    bucket = jnp.where(n < max_exact, n, large)
    lut = jnp.broadcast_to(tab_ref[:, 0:1], (SWA_HEADS, WINDOW))
    for b in range(1, REL_BUCKETS):
        lut = jnp.where(bucket >= b, tab_ref[:, b:b + 1], lut)
    lut_ref[...] = lut


def _bias_lut(rel_bias_t):
    return pl.pallas_call(
        _bias_lut_kernel,
        out_shape=jax.ShapeDtypeStruct((SWA_HEADS, WINDOW), F32),
        name="bias_lut",
    )(rel_bias_t)


def _key_head_slabs(t):
    lane = lax.broadcasted_iota(jnp.int32, (t.shape[0], PAIR), 1)
    low_half = lane < HEAD_DIM
    los, his = [], []
    for p in range(SWA_KV_HEADS // 2):
        pair = t[:, p * PAIR:(p + 1) * PAIR]
        swapped = pltpu.roll(pair, HEAD_DIM, axis=1)
        zero = jnp.zeros_like(pair)
        los += [jnp.where(low_half, pair, zero), jnp.where(low_half, swapped, zero)]
        his += [jnp.where(low_half, zero, swapped), jnp.where(low_half, zero, pair)]
    return [a.astype(BF16) for a in los], [a.astype(BF16) for a in his]


def _value_head_slabs_t(t):
    los, his = [], []
    for p in range(SWA_KV_HEADS // 2):
        pair_t = t[:, p * PAIR:(p + 1) * PAIR].T
        zero = jnp.zeros((HEAD_DIM, t.shape[0]), F32)
        for head_t in (pair_t[0:HEAD_DIM], pair_t[HEAD_DIM:PAIR]):
            los.append(jnp.concatenate([head_t, zero], axis=0))
            his.append(jnp.concatenate([zero, head_t], axis=0))
    return [a.astype(BF16) for a in los], [a.astype(BF16) for a in his]


def _mix_kernel(x_ref, posq_ref, posk_ref, lut_ref, sink_ref, g_ref, win_ref, convw_ref, wout_ref, o_ref,
                q_s, klo_s, khi_s, vtlo_s, vthi_s, h_s, cu_s, merged_s, relm_s, bias_s, *, tiles_per_seq):
    tm = x_ref.shape[0]
    nblk = tm // BLOCK
    step = pl.program_id(0)
    first = (step % tiles_per_seq) == 0

    @pl.when(first)
    def _():
        for s in (klo_s, khi_s):
            s[:, 0:BLOCK, :] = jnp.zeros((SWA_KV_HEADS, BLOCK, PAIR), BF16)
        for s in (vtlo_s, vthi_s):
            s[:, 0] = jnp.zeros((SWA_KV_HEADS, PAIR, BLOCK), BF16)
        cu_s[0:SUBLANES, :] = jnp.zeros((SUBLANES, D_MODEL), F32)

    @pl.when(jnp.logical_not(first))
    def _():
        for s in (klo_s, khi_s):
            s[:, 0:BLOCK, :] = s[:, tm:tm + BLOCK, :]
        for s in (vtlo_s, vthi_s):
            s[:, 0] = s[:, nblk]
        cu_s[0:SUBLANES, :] = cu_s[tm:tm + SUBLANES, :]

    @pl.when(step == 0)
    def _():
        relm_s[...] = jnp.full(relm_s.shape, -2, jnp.int32)

    slots = [jnp.where(first, nblk, 0)] + list(range(1, nblk))

    relms, misses = [], []
    for j in range(nblk):
        pos_k = jnp.concatenate(
            [jnp.broadcast_to(posk_ref[0, j:j + 1, c * BLOCK:(c + 1) * BLOCK], (BLOCK, BLOCK)).T
             for c in range(2)], axis=0)
        rel = posq_ref[0, j:j + 1, :] - pos_k
        relm = jnp.where((rel >= 0) & (rel < WINDOW), rel, -1)
        relms.append(relm)
        misses.append(jnp.sum((relm != relm_s[slots[j]]).astype(jnp.int32)))

    h_s[...] = _rms(x_ref[...], g_ref[...]).astype(BF16)
    h = h_s[...]

    k_lo, k_hi = _key_head_slabs(_dot(h, win_ref[:, OFF_K:OFF_K + KV_WIDTH]))
    vt_lo, vt_hi = _value_head_slabs_t(_dot(h, win_ref[:, OFF_V:OFF_V + KV_WIDTH]))
    for g in range(SWA_KV_HEADS):
        klo_s[g, BLOCK:BLOCK + tm, :] = k_lo[g]
        khi_s[g, BLOCK:BLOCK + tm, :] = k_hi[g]
        for j in range(nblk):
            vtlo_s[g, j + 1] = vt_lo[g][:, j * BLOCK:(j + 1) * BLOCK]
            vthi_s[g, j + 1] = vt_hi[g][:, j * BLOCK:(j + 1) * BLOCK]
    q_s[...] = (_dot(h, win_ref[:, OFF_Q:OFF_Q + Q_WIDTH]) * (HEAD_DIM ** -0.5)).astype(BF16)

    for j in range(nblk):
        @pl.when(misses[j] != 0)
        def _(j=j):
            relm_s[slots[j]] = relms[j]
            shown = relms[j] >= 0
            idx = jnp.maximum(relms[j], 0)
            for hd in range(SWA_HEADS):
                lut = jnp.broadcast_to(lut_ref[hd:hd + 1, :], (2 * BLOCK, WINDOW))
                bias_s[slots[j], hd] = jnp.where(shown, jnp.take_along_axis(lut, idx, axis=1), NEG_INF)

    h = h_s[...]

    def qk(blk, g):
        rows = slice(blk * BLOCK, (blk + 1) * BLOCK)
        keys = slice(blk * BLOCK, (blk + 2) * BLOCK)
        c0 = g * 2 * PAIR
        q2 = jnp.concatenate([q_s[rows, c0:c0 + PAIR], q_s[rows, c0 + PAIR:c0 + 2 * PAIR]], axis=0)
        return _dot_nt(klo_s[g, keys, :], q2), _dot_nt(khi_s[g, keys, :], q2)

    def softmax_pv(blk, g, logits_t):
        probs_t = [[None, None], [None, None]]
        inv = [[None, None], [None, None]]
        for half in range(2):
            for pair in range(2):
                head = g * 4 + pair * 2 + half
                lt = logits_t[half][:, pair * BLOCK:(pair + 1) * BLOCK] + bias_s[slots[blk], head]
                sink = sink_ref[head]
                m = jnp.maximum(jnp.max(lt, axis=0, keepdims=True), sink)
                e = jnp.exp(lt - m)
                inv[half][pair] = 1.0 / (jnp.sum(e, axis=0, keepdims=True) + jnp.exp(sink - m))
                probs_t[half][pair] = e.astype(BF16)
        v_lo_t = jnp.concatenate([vtlo_s[g, blk], vtlo_s[g, blk + 1]], axis=1)
        v_hi_t = jnp.concatenate([vthi_s[g, blk], vthi_s[g, blk + 1]], axis=1)
        out_t = (_dot(v_lo_t, jnp.concatenate(probs_t[0], axis=1))
                 + _dot(v_hi_t, jnp.concatenate(probs_t[1], axis=1)))
        scale = jnp.concatenate(
            [jnp.broadcast_to(jnp.concatenate(inv[0], axis=1), (HEAD_DIM, 2 * BLOCK)),
             jnp.broadcast_to(jnp.concatenate(inv[1], axis=1), (HEAD_DIM, 2 * BLOCK))], axis=0)
        return (out_t * scale).T

    units = [(blk, g) for g in range(SWA_KV_HEADS) for blk in range(nblk)]
    logits_next = qk(*units[0])
    for g in range(SWA_KV_HEADS):
        cols = slice(g * 2 * PAIR, (g + 1) * 2 * PAIR)

        def proj(off):
            return _dot(h, win_ref[:, off + g * 2 * PAIR:off + (g + 1) * 2 * PAIR])

        def conv_taps(_):
            cu = proj(OFF_C) * proj(OFF_U)
            cu_s[SUBLANES:SUBLANES + tm, cols] = cu
            return (convw_ref[0:1, cols] * cu_s[SUBLANES - 2:SUBLANES - 2 + tm, cols]
                    + convw_ref[1:2, cols] * cu_s[SUBLANES - 1:SUBLANES - 1 + tm, cols]
                    + convw_ref[2:3, cols] * cu)

        stages = [conv_taps,
                  lambda conv: proj(OFF_B) * conv,
                  lambda conv: jax.nn.sigmoid(proj(OFF_GC)) * conv]
        attn_blocks = []
        conv = None
        for blk in range(nblk):
            logits_t = logits_next
            nxt = g * nblk + blk + 1
            if nxt < len(units):
                logits_next = qk(*units[nxt])
            if blk < len(stages):
                conv = stages[blk](conv)
            out = softmax_pv(blk, g, logits_t)
            attn_blocks.append(jnp.concatenate([out[0:BLOCK], out[BLOCK:2 * BLOCK]], axis=1))
        for stage in stages[nblk:]:
            conv = stage(conv)
        attn = jnp.concatenate(attn_blocks, axis=0)
        merged_s[:, cols] = (jax.nn.sigmoid(proj(OFF_GA)) * attn + conv).astype(BF16)

    o_ref[...] = x_ref[...] + _dot(merged_s[...], wout_ref[...])


def _mix(x, posq, posk, lut, sinks, gain, w_in, conv_w, w_out, *, seq_len):
    t = x.shape[0]
    tm = TOKEN_TILE
    nblk = tm // BLOCK
    k_scratch = pltpu.VMEM((SWA_KV_HEADS, BLOCK + tm, PAIR), BF16)
    vt_scratch = pltpu.VMEM((SWA_KV_HEADS, nblk + 1, PAIR, BLOCK), BF16)
    scratch = [
        pltpu.VMEM((tm, Q_WIDTH), BF16),
        k_scratch, k_scratch, vt_scratch, vt_scratch,
        pltpu.VMEM((tm, D_MODEL), BF16),
        pltpu.VMEM((SUBLANES + tm, D_MODEL), F32),
        pltpu.VMEM((tm, D_MODEL), BF16),
        pltpu.VMEM((nblk + 1, 2 * BLOCK, BLOCK), jnp.int32),
        pltpu.VMEM((nblk + 1, SWA_HEADS, 2 * BLOCK, BLOCK), F32),
    ]
    return pl.pallas_call(
        functools.partial(_mix_kernel, tiles_per_seq=seq_len // tm),
        grid=(t // tm,),
        in_specs=[
            pl.BlockSpec((tm, D_MODEL), lambda i: (i, 0)),
            pl.BlockSpec((1, nblk, BLOCK), lambda i: (i, 0, 0)),
            pl.BlockSpec((1, nblk, 2 * BLOCK), lambda i: (i, 0, 0)),
            _resident((SWA_HEADS, WINDOW)),
            pl.BlockSpec(memory_space=pltpu.SMEM),
            _resident((1, D_MODEL)),
            _resident((D_MODEL, IN_WIDTH)),
            _resident((CONV_K, D_MODEL)),
            _resident((D_MODEL, D_MODEL)),
        ],
        out_specs=pl.BlockSpec((tm, D_MODEL), lambda i: (i, 0)),
        out_shape=jax.ShapeDtypeStruct((t, D_MODEL), F32),
        scratch_shapes=scratch,
        compiler_params=_params(
            sum(_nbytes(b.shape, b.dtype) for b in scratch) + _nbytes(w_in.shape, F32) + _nbytes(w_out.shape, F32),
            2 * _nbytes((tm, D_MODEL), F32), (tm, D_MODEL), 4),
        name="mix",
    )(x, posq, posk, lut, sinks, gain, w_in, conv_w, w_out)


def _mem_kv_kernel(mem_ref, g_ref, wkv_ref, kv_ref):
    h = _rms(mem_ref[...], g_ref[...]).astype(BF16)
    kv_ref[...] = _dot(h, wkv_ref[...]).astype(BF16)


def _mem_kv(mem, gain, w_kv):
    rows = mem.shape[0]
    tm = TOKEN_TILE
    return pl.pallas_call(
        _mem_kv_kernel,
        grid=(rows // tm,),
        in_specs=[
            pl.BlockSpec((tm, D_MODEL), lambda i: (i, 0)),
            _resident((1, D_MODEL)),
            _resident((D_MODEL, 2 * D_MODEL)),
        ],
        out_specs=pl.BlockSpec((tm, 2 * D_MODEL), lambda i: (i, 0)),
        out_shape=jax.ShapeDtypeStruct((rows, 2 * D_MODEL), BF16),
        compiler_params=_params(_nbytes(w_kv.shape, F32),
                                _nbytes((tm, D_MODEL), F32) + _nbytes((tm, 2 * D_MODEL), BF16), (tm, 2 * D_MODEL), 2),
        name="mem_kv",
    )(mem, gain, w_kv)


def _xattn_kernel(x_ref, xnext_ref, g_ref, wq_ref, kv_ref, wo_ref, o_ref, h_s, q_s):
    step = pl.program_id(0)
    cur, nxt = step % 2, (step + 1) % 2

    def head_cols(hd):
        return slice(hd * MEM_HEAD_DIM, (hd + 1) * MEM_HEAD_DIM)

    def q_proj(h, hd):
        return (_dot(h, wq_ref[:, head_cols(hd)]) * (MEM_HEAD_DIM ** -0.5)).astype(BF16)

    def logits(hd, q):
        return _dot_nt(q, kv_ref[0, :, head_cols(hd)])

    def context(hd, lg):
        e = jnp.exp(lg - jnp.max(lg, axis=1, keepdims=True))
        p = (e / jnp.sum(e, axis=1, keepdims=True)).astype(BF16)
        v = kv_ref[0, :, D_MODEL + hd * MEM_HEAD_DIM:D_MODEL + (hd + 1) * MEM_HEAD_DIM]
        return _dot(p, v).astype(BF16)

    def body(first):
        if first:
            h = _rms(x_ref[...], g_ref[...]).astype(BF16)
            qs = [q_proj(h, 0), q_proj(h, 1)]
        else:
            h = h_s[cur]
            qs = [q_s[cur, 0], q_s[cur, 1]]
        lgs = [logits(0, qs[0])]
        acc = x_ref[...]
        for hd in range(MEM_HEADS):
            if hd + 2 < MEM_HEADS:
                qs.append(q_proj(h, hd + 2))
            if hd + 1 < MEM_HEADS:
                lgs.append(logits(hd + 1, qs[hd + 1]))
            acc = acc + _dot(context(hd, lgs[hd]), wo_ref[head_cols(hd), :])
        h_next = _rms(xnext_ref[...], g_ref[...]).astype(BF16)
        h_s[nxt] = h_next
        q_s[nxt, 0] = q_proj(h_next, 0)
        q_s[nxt, 1] = q_proj(h_next, 1)
        o_ref[...] = acc

    @pl.when(step == 0)
    def _():
        body(True)

    @pl.when(step != 0)
    def _():
        body(False)


def _xattn(x, gain, w_q, kv, w_o, *, seq_len):
    t = x.shape[0]
    tm = XATTN_TOKEN_TILE
    tiles_per_seq = seq_len // tm
    scratch = [pltpu.VMEM((2, tm, D_MODEL), BF16), pltpu.VMEM((2, 2, tm, MEM_HEAD_DIM), BF16)]
    return pl.pallas_call(
        _xattn_kernel,
        grid=(t // tm,),
        in_specs=[
            pl.BlockSpec((tm, D_MODEL), lambda i: (i, 0)),
            pl.BlockSpec((tm, D_MODEL), lambda i: (jnp.minimum(i + 1, t // tm - 1), 0)),
            _resident((1, D_MODEL)),
            _resident((D_MODEL, D_MODEL)),
            pl.BlockSpec((1, N_MEM, 2 * D_MODEL), lambda i: (i // tiles_per_seq, 0, 0)),
            _resident((D_MODEL, D_MODEL)),
        ],
        out_specs=pl.BlockSpec((tm, D_MODEL), lambda i: (i, 0)),
        out_shape=jax.ShapeDtypeStruct((t, D_MODEL), F32),
        scratch_shapes=scratch,
        compiler_params=_params(
            sum(_nbytes(b.shape, b.dtype) for b in scratch) + _nbytes(w_q.shape, F32) + _nbytes(w_o.shape, F32),
            3 * _nbytes((tm, D_MODEL), F32) + _nbytes((N_MEM, 2 * D_MODEL), BF16), (tm, D_MODEL), 3),
        name="xattn",
    )(x, x, gain, w_q, kv, w_o)


def kernel(x, mem, positions, rel_bias, ffn1_norm, ffn1_w_gu, ffn1_w_down, mix_norm, w_in, sinks, conv_w, w_out, xattn_norm, mem_norm, xattn_wq, xattn_wkv, xattn_wo, ffn2_norm, ffn2_w_gu, ffn2_w_down, final_norm):
    batch, seq_len, _ = x.shape
    depth = w_in.shape[0]
    tokens = batch * seq_len
    nb = seq_len // BLOCK

    xt = x.reshape(tokens, D_MODEL)
    posq = positions.reshape(tokens // TOKEN_TILE, TOKEN_TILE // BLOCK, BLOCK)
    pos_blocks = positions.reshape(batch, nb, BLOCK)
    prev = jnp.concatenate([jnp.full((batch, 1, BLOCK), POS_PAD, positions.dtype), pos_blocks[:, :-1]], axis=1)
    posk = jnp.concatenate([prev, pos_blocks], axis=2).reshape(tokens // TOKEN_TILE, TOKEN_TILE // BLOCK, 2 * BLOCK)
    lut = _bias_lut(rel_bias.T)
    final_gain = final_norm.reshape(1, D_MODEL)

    for l in range(depth):
        xt = _ffn(xt, ffn1_norm[l].reshape(1, D_MODEL), ffn1_w_gu[l], ffn1_w_down[l],
                  final_gain, final_norm=False)
        kv = _mem_kv(mem.reshape(batch * N_MEM, D_MODEL), mem_norm[l].reshape(1, D_MODEL), xattn_wkv[l])
        xt = _mix(xt, posq, posk, lut, sinks[l], mix_norm[l].reshape(1, D_MODEL),
                  w_in[l], conv_w[l], w_out[l], seq_len=seq_len)
        xt = _xattn(xt, xattn_norm[l].reshape(1, D_MODEL), xattn_wq[l],
                    kv.reshape(batch, N_MEM, 2 * D_MODEL), xattn_wo[l], seq_len=seq_len)
        xt = _ffn(xt, ffn2_norm[l].reshape(1, D_MODEL), ffn2_w_gu[l], ffn2_w_down[l],
                  final_gain, final_norm=(l == depth - 1))
    return xt.reshape(batch, seq_len, D_MODEL)
```

```python
import functools
import math

import jax
import jax.numpy as jnp
from jax import lax
from jax.experimental import pallas as pl
from jax.experimental.pallas import tpu as pltpu

D_MODEL = 1024
D_FF = 2816
N_MEM = 256
MEM_HEADS = 4
MEM_HEAD_DIM = D_MODEL // MEM_HEADS
HEAD_DIM = 64
SWA_HEADS = 16
SWA_KV_HEADS = 4
WINDOW = 128
BLOCK = 128
REL_BUCKETS = 32
REL_MAX_DIST = 128
CONV_K = 3
EPS = 1e-6
NEG_INF = -1e30
POS_PAD = 1 << 30

Q_WIDTH = SWA_HEADS * HEAD_DIM
KV_WIDTH = SWA_KV_HEADS * HEAD_DIM
OFF_Q = 0
OFF_K = OFF_Q + Q_WIDTH
OFF_V = OFF_K + KV_WIDTH
OFF_C = OFF_V + KV_WIDTH
OFF_B = OFF_C + D_MODEL
OFF_U = OFF_B + D_MODEL
OFF_GA = OFF_U + D_MODEL
OFF_GC = OFF_GA + D_MODEL
IN_WIDTH = OFF_GC + D_MODEL

SUBLANES = 8
MXU_COLS = 256
V7X_VMEM_BYTES = 64 * 1024 * 1024
VMEM_REQUEST_CAP = V7X_VMEM_BYTES - 4 * 1024 * 1024

TOKEN_TILE = 512
XATTN_TOKEN_TILE = 1024
FFN_TOKEN_TILE = 512
FF_CHUNK = MXU_COLS
PAIR = 2 * HEAD_DIM

BF16 = jnp.bfloat16
F32 = jnp.float32


def _rms(x, g):
    return x * lax.rsqrt(jnp.mean(x * x, axis=-1, keepdims=True) + EPS) * g


def _dot(a, b):
    return jnp.dot(a.astype(BF16), b.astype(BF16), preferred_element_type=F32)


def _dot_nt(a, b):
    return lax.dot_general(a.astype(BF16), b.astype(BF16), (((1,), (1,)), ((), ())),
                           preferred_element_type=F32)


def _resident(shape):
    return pl.BlockSpec(shape, lambda i: (0,) * len(shape), pipeline_mode=pl.Buffered(1))


def _nbytes(shape, dtype):
    return math.prod(shape) * jnp.dtype(dtype).itemsize


def _params(resident_bytes, per_step_bytes, tile_shape, tile_temporaries):
    request = resident_bytes + 2 * per_step_bytes + tile_temporaries * _nbytes(tile_shape, F32)
    return pltpu.CompilerParams(dimension_semantics=("arbitrary",),
                                vmem_limit_bytes=min(VMEM_REQUEST_CAP, request))


FF_CHUNKS = D_FF // FF_CHUNK
OUT_CHUNKS = D_MODEL // MXU_COLS


def _ffn_kernel(x_ref, xnext_ref, g_ref, wgu_hbm, wd_hbm, fg_ref, o_ref, wgu_v, wd_v, act_ref, h_s, act0_s, sems,
                *, final_norm):
    def gate_copy(j):
        return pltpu.make_async_copy(wgu_hbm.at[:, pl.ds(j * FF_CHUNK, FF_CHUNK)], wgu_v.at[j], sems.at[j])

    def up_copy(j):
        return pltpu.make_async_copy(wgu_hbm.at[:, pl.ds(D_FF + j * FF_CHUNK, FF_CHUNK)],
                                     wgu_v.at[FF_CHUNKS + j], sems.at[FF_CHUNKS + j])

    def down_copy(c):
        return pltpu.make_async_copy(wd_hbm.at[:, pl.ds(c * MXU_COLS, MXU_COLS)], wd_v.at[c],
                                     sems.at[2 * FF_CHUNKS + c])

    step = pl.program_id(0)
    cur, nxt = step % 2, (step + 1) % 2

    def chunk_act(h, j):
        gate = _dot(h, wgu_v[j])
        up = _dot(h, wgu_v[FF_CHUNKS + j])
        return (gate * jax.nn.sigmoid(gate) * up).astype(BF16)

    def body(first):
        if first:
            h = _rms(x_ref[...], g_ref[...]).astype(BF16)
            gate_copy(0).wait()
            up_copy(0).wait()
            act_ref[:, 0:FF_CHUNK] = chunk_act(h, 0)
        else:
            h = h_s[cur]
            act_ref[:, 0:FF_CHUNK] = act0_s[cur]
        for j in range(1, FF_CHUNKS):
            if first:
                gate_copy(j).wait()
                up_copy(j).wait()
            act_ref[:, j * FF_CHUNK:(j + 1) * FF_CHUNK] = chunk_act(h, j)
        if first:
            for c in range(OUT_CHUNKS):
                down_copy(c).wait()
        y = x_ref[...] + 0.5 * jnp.concatenate([_dot(act_ref[...], wd_v[c]) for c in range(OUT_CHUNKS)], axis=1)
        h_next = _rms(xnext_ref[...], g_ref[...]).astype(BF16)
        h_s[nxt] = h_next
        act0_s[nxt] = chunk_act(h_next, 0)
        if final_norm:
            y = _rms(y, fg_ref[...])
        o_ref[...] = y

    first_step = step == 0

    @pl.when(first_step)
    def _():
        for j in range(FF_CHUNKS):
            gate_copy(j).start()
            up_copy(j).start()
        for c in range(OUT_CHUNKS):
            down_copy(c).start()
        body(True)

    @pl.when(jnp.logical_not(first_step))
    def _():
        body(False)


def _ffn(x, gain, w_gu, w_down, final_gain, *, final_norm):
    t = x.shape[0]
    tm = FFN_TOKEN_TILE
    scratch = [
        pltpu.VMEM((2 * FF_CHUNKS, D_MODEL, FF_CHUNK), F32),
        pltpu.VMEM((OUT_CHUNKS, D_FF, MXU_COLS), F32),
        pltpu.VMEM((tm, D_FF), BF16),
        pltpu.VMEM((2, tm, D_MODEL), BF16),
        pltpu.VMEM((2, tm, FF_CHUNK), BF16),
    ]
    return pl.pallas_call(
        functools.partial(_ffn_kernel, final_norm=final_norm),
        grid=(t // tm,),
        in_specs=[
            pl.BlockSpec((tm, D_MODEL), lambda i: (i, 0)),
            pl.BlockSpec((tm, D_MODEL), lambda i: (jnp.minimum(i + 1, t // tm - 1), 0)),
            _resident((1, D_MODEL)),
            pl.BlockSpec(memory_space=pl.ANY),
            pl.BlockSpec(memory_space=pl.ANY),
            _resident((1, D_MODEL)),
        ],
        out_specs=pl.BlockSpec((tm, D_MODEL), lambda i: (i, 0)),
        out_shape=jax.ShapeDtypeStruct((t, D_MODEL), F32),
        scratch_shapes=scratch + [pltpu.SemaphoreType.DMA((2 * FF_CHUNKS + OUT_CHUNKS,))],
        compiler_params=_params(sum(_nbytes(b.shape, b.dtype) for b in scratch),
                                3 * _nbytes((tm, D_MODEL), F32), (tm, D_MODEL), 2),
        name="ffn_final" if final_norm else "ffn",
    )(x, x, gain, w_gu, w_down, final_gain)


def _bias_lut_kernel(tab_ref, lut_ref):
    n = lax.broadcasted_iota(jnp.int32, (SWA_HEADS, WINDOW), 1).astype(F32)
    max_exact = REL_BUCKETS // 2
    nf = jnp.maximum(n, 1.0)
    large = max_exact + jnp.floor(jnp.log(nf / max_exact) / math.log(REL_MAX_DIST / max_exact)
                                  * (REL_BUCKETS - max_exact))
    large = jnp.minimum(large, REL_BUCKETS - 1.0)
    bucket = jnp.where(n < max_exact, n, large)
    lut = jnp.broadcast_to(tab_ref[:, 0:1], (SWA_HEADS, WINDOW))
    for b in range(1, REL_BUCKETS):
        lut = jnp.where(bucket >= b, tab_ref[:, b:b + 1], lut)
    lut_ref[...] = lut


def _bias_lut(rel_bias_t):
    return pl.pallas_call(
        _bias_lut_kernel,
        out_shape=jax.ShapeDtypeStruct((SWA_HEADS, WINDOW), F32),
        name="bias_lut",
    )(rel_bias_t)


def _key_head_slabs(t):
    lane = lax.broadcasted_iota(jnp.int32, (t.shape[0], PAIR), 1)
    low_half = lane < HEAD_DIM
    los, his = [], []
    for p in range(SWA_KV_HEADS // 2):
        pair = t[:, p * PAIR:(p + 1) * PAIR]
        swapped = pltpu.roll(pair, HEAD_DIM, axis=1)
        zero = jnp.zeros_like(pair)
        los += [jnp.where(low_half, pair, zero), jnp.where(low_half, swapped, zero)]
        his += [jnp.where(low_half, zero, swapped), jnp.where(low_half, zero, pair)]
    return [a.astype(BF16) for a in los], [a.astype(BF16) for a in his]


def _value_head_slabs_t(t):
    los, his = [], []
    for p in range(SWA_KV_HEADS // 2):
        pair_t = t[:, p * PAIR:(p + 1) * PAIR].T
        zero = jnp.zeros((HEAD_DIM, t.shape[0]), F32)
        for head_t in (pair_t[0:HEAD_DIM], pair_t[HEAD_DIM:PAIR]):
            los.append(jnp.concatenate([head_t, zero], axis=0))
            his.append(jnp.concatenate([zero, head_t], axis=0))
    return [a.astype(BF16) for a in los], [a.astype(BF16) for a in his]


def _mix_kernel(x_ref, posq_ref, posk_ref, posq_next_ref, posk_next_ref, lut_ref, sink_ref, g_ref, win_ref,
                convw_ref, wout_ref, o_ref,
                q_s, klo_s, khi_s, vtlo_s, vthi_s, cu_s, merged_s, relm_s, relm_next_s, bias_s, *, tiles_per_seq):
    tm = x_ref.shape[0]
    nblk = tm // BLOCK
    step = pl.program_id(0)
    first = (step % tiles_per_seq) == 0
    next_first = ((step + 1) % tiles_per_seq) == 0

    @pl.when(first)
    def _():
        for s in (klo_s, khi_s):
            s[:, 0:BLOCK, :] = jnp.zeros((SWA_KV_HEADS, BLOCK, PAIR), BF16)
        for s in (vtlo_s, vthi_s):
            s[:, 0] = jnp.zeros((SWA_KV_HEADS, PAIR, BLOCK), BF16)
        cu_s[0:SUBLANES, :] = jnp.zeros((SUBLANES, D_MODEL), F32)

    @pl.when(jnp.logical_not(first))
    def _():
        for s in (klo_s, khi_s):
            s[:, 0:BLOCK, :] = s[:, tm:tm + BLOCK, :]
        for s in (vtlo_s, vthi_s):
            s[:, 0] = s[:, nblk]
        cu_s[0:SUBLANES, :] = cu_s[tm:tm + SUBLANES, :]

    def masked_rel(pq_ref, pk_ref, j):
        pos_k = jnp.concatenate(
            [jnp.broadcast_to(pk_ref[0, j:j + 1, c * BLOCK:(c + 1) * BLOCK], (BLOCK, BLOCK)).T
             for c in range(2)], axis=0)
        rel = pq_ref[0, j:j + 1, :] - pos_k
        return jnp.where((rel >= 0) & (rel < WINDOW), rel, -1)

    def block_slots(sequence_start):
        return [jnp.where(sequence_start, nblk, 0)] + list(range(1, nblk))

    def rebuild(slot, relm):
        relm_s[slot] = relm
        shown = relm >= 0
        idx = jnp.maximum(relm, 0)
        for hd in range(SWA_HEADS):
            lut = jnp.broadcast_to(lut_ref[hd:hd + 1, :], (2 * BLOCK, WINDOW))
            bias_s[slot, hd] = jnp.where(shown, jnp.take_along_axis(lut, idx, axis=1), NEG_INF)

    slots = block_slots(first)
    next_slots = block_slots(next_first)

    @pl.when(step == 0)
    def _():
        relm_s[...] = jnp.full(relm_s.shape, -2, jnp.int32)
        for j in range(nblk):
            rebuild(slots[j], masked_rel(posq_ref, posk_ref, j))

    next_misses = []
    for j in range(nblk):
        relm = masked_rel(posq_next_ref, posk_next_ref, j)
        relm_next_s[j] = relm
        next_misses.append(jnp.sum((relm != relm_s[next_slots[j]]).astype(jnp.int32)))

    h = _rms(x_ref[...], g_ref[...]).astype(BF16)

    k_lo, k_hi = _key_head_slabs(_dot(h, win_ref[:, OFF_K:OFF_K + KV_WIDTH]))
    vt_lo, vt_hi = _value_head_slabs_t(_dot(h, win_ref[:, OFF_V:OFF_V + KV_WIDTH]))
    for g in range(SWA_KV_HEADS):
        klo_s[g, BLOCK:BLOCK + tm, :] = k_lo[g]
        khi_s[g, BLOCK:BLOCK + tm, :] = k_hi[g]
        for j in range(nblk):
            vtlo_s[g, j + 1] = vt_lo[g][:, j * BLOCK:(j + 1) * BLOCK]
            vthi_s[g, j + 1] = vt_hi[g][:, j * BLOCK:(j + 1) * BLOCK]
    q_s[...] = (_dot(h, win_ref[:, OFF_Q:OFF_Q + Q_WIDTH]) * (HEAD_DIM ** -0.5)).astype(BF16)

    def qk(blk, g):
        rows = slice(blk * BLOCK, (blk + 1) * BLOCK)
        keys = slice(blk * BLOCK, (blk + 2) * BLOCK)
        c0 = g * 2 * PAIR
        q2 = jnp.concatenate([q_s[rows, c0:c0 + PAIR], q_s[rows, c0 + PAIR:c0 + 2 * PAIR]], axis=0)
        return _dot_nt(klo_s[g, keys, :], q2), _dot_nt(khi_s[g, keys, :], q2)

    def softmax_pv(blk, g, logits_t):
        probs_t = [[None, None], [None, None]]
        inv = [[None, None], [None, None]]
        for half in range(2):
            for pair in range(2):
                head = g * 4 + pair * 2 + half
                lt = logits_t[half][:, pair * BLOCK:(pair + 1) * BLOCK] + bias_s[slots[blk], head]
                sink = sink_ref[head]
                m = jnp.maximum(jnp.max(lt, axis=0, keepdims=True), sink)
                e = jnp.exp(lt - m)
                inv[half][pair] = 1.0 / (jnp.sum(e, axis=0, keepdims=True) + jnp.exp(sink - m))
                probs_t[half][pair] = e.astype(BF16)
        v_lo_t = jnp.concatenate([vtlo_s[g, blk], vtlo_s[g, blk + 1]], axis=1)
        v_hi_t = jnp.concatenate([vthi_s[g, blk], vthi_s[g, blk + 1]], axis=1)
        out_t = (_dot(v_lo_t, jnp.concatenate(probs_t[0], axis=1))
                 + _dot(v_hi_t, jnp.concatenate(probs_t[1], axis=1)))
        scale = jnp.concatenate(
            [jnp.broadcast_to(jnp.concatenate(inv[0], axis=1), (HEAD_DIM, 2 * BLOCK)),
             jnp.broadcast_to(jnp.concatenate(inv[1], axis=1), (HEAD_DIM, 2 * BLOCK))], axis=0)
        return (out_t * scale).T

    units = [(blk, g) for g in range(SWA_KV_HEADS) for blk in range(nblk)]
    logits_next = qk(*units[0])
    for g in range(SWA_KV_HEADS):
        cols = slice(g * 2 * PAIR, (g + 1) * 2 * PAIR)

        def proj(off):
            return _dot(h, win_ref[:, off + g * 2 * PAIR:off + (g + 1) * 2 * PAIR])

        def conv_taps(_):
            cu = proj(OFF_C) * proj(OFF_U)
            cu_s[SUBLANES:SUBLANES + tm, cols] = cu
            return (convw_ref[0:1, cols] * cu_s[SUBLANES - 2:SUBLANES - 2 + tm, cols]
                    + convw_ref[1:2, cols] * cu_s[SUBLANES - 1:SUBLANES - 1 + tm, cols]
                    + convw_ref[2:3, cols] * cu)

        stages = [conv_taps,
                  lambda conv: proj(OFF_B) * conv,
                  lambda conv: jax.nn.sigmoid(proj(OFF_GC)) * conv]
        attn_blocks = []
        conv = None
        for blk in range(nblk):
            logits_t = logits_next
            nxt = g * nblk + blk + 1
            if nxt < len(units):
                logits_next = qk(*units[nxt])
            if blk < len(stages):
                conv = stages[blk](conv)
            out = softmax_pv(blk, g, logits_t)
            attn_blocks.append(jnp.concatenate([out[0:BLOCK], out[BLOCK:2 * BLOCK]], axis=1))
        for stage in stages[nblk:]:
            conv = stage(conv)
        attn = jnp.concatenate(attn_blocks, axis=0)
        merged_s[:, cols] = (jax.nn.sigmoid(proj(OFF_GA)) * attn + conv).astype(BF16)

    o_ref[...] = x_ref[...] + _dot(merged_s[...], wout_ref[...])

    for j in range(nblk):
        @pl.when(next_misses[j] != 0)
        def _(j=j):
            rebuild(next_slots[j], relm_next_s[j])


def _mix(x, posq, posk, lut, sinks, gain, w_in, conv_w, w_out, *, seq_len):
    t = x.shape[0]
    tm = TOKEN_TILE
    nblk = tm // BLOCK
    k_scratch = pltpu.VMEM((SWA_KV_HEADS, BLOCK + tm, PAIR), BF16)
    vt_scratch = pltpu.VMEM((SWA_KV_HEADS, nblk + 1, PAIR, BLOCK), BF16)
    scratch = [
        pltpu.VMEM((tm, Q_WIDTH), BF16),
        k_scratch, k_scratch, vt_scratch, vt_scratch,
        pltpu.VMEM((SUBLANES + tm, D_MODEL), F32),
        pltpu.VMEM((tm, D_MODEL), BF16),
        pltpu.VMEM((nblk + 1, 2 * BLOCK, BLOCK), jnp.int32),
        pltpu.VMEM((nblk, 2 * BLOCK, BLOCK), jnp.int32),
        pltpu.VMEM((nblk + 1, SWA_HEADS, 2 * BLOCK, BLOCK), F32),
    ]
    return pl.pallas_call(
        functools.partial(_mix_kernel, tiles_per_seq=seq_len // tm),
        grid=(t // tm,),
        in_specs=[
            pl.BlockSpec((tm, D_MODEL), lambda i: (i, 0)),
            pl.BlockSpec((1, nblk, BLOCK), lambda i: (i, 0, 0)),
            pl.BlockSpec((1, nblk, 2 * BLOCK), lambda i: (i, 0, 0)),
            pl.BlockSpec((1, nblk, BLOCK), lambda i: (jnp.minimum(i + 1, t // tm - 1), 0, 0)),
            pl.BlockSpec((1, nblk, 2 * BLOCK), lambda i: (jnp.minimum(i + 1, t // tm - 1), 0, 0)),
            _resident((SWA_HEADS, WINDOW)),
            pl.BlockSpec(memory_space=pltpu.SMEM),
            _resident((1, D_MODEL)),
            _resident((D_MODEL, IN_WIDTH)),
            _resident((CONV_K, D_MODEL)),
            _resident((D_MODEL, D_MODEL)),
        ],
        out_specs=pl.BlockSpec((tm, D_MODEL), lambda i: (i, 0)),
        out_shape=jax.ShapeDtypeStruct((t, D_MODEL), F32),
        scratch_shapes=scratch,
        compiler_params=_params(
            sum(_nbytes(b.shape, b.dtype) for b in scratch) + _nbytes(w_in.shape, F32) + _nbytes(w_out.shape, F32),
            2 * _nbytes((tm, D_MODEL), F32), (tm, D_MODEL), 4),
        name="mix",
    )(x, posq, posk, posq, posk, lut, sinks, gain, w_in, conv_w, w_out)


def _mem_kv_kernel(mem_ref, g_ref, wkv_ref, kv_ref):
    h = _rms(mem_ref[...], g_ref[...]).astype(BF16)
    kv_ref[...] = _dot(h, wkv_ref[...]).astype(BF16)


def _mem_kv(mem, gain, w_kv):
    rows = mem.shape[0]
    tm = TOKEN_TILE
    return pl.pallas_call(
        _mem_kv_kernel,
        grid=(rows // tm,),
        in_specs=[
            pl.BlockSpec((tm, D_MODEL), lambda i: (i, 0)),
            _resident((1, D_MODEL)),
            _resident((D_MODEL, 2 * D_MODEL)),
        ],
        out_specs=pl.BlockSpec((tm, 2 * D_MODEL), lambda i: (i, 0)),
        out_shape=jax.ShapeDtypeStruct((rows, 2 * D_MODEL), BF16),
        compiler_params=_params(_nbytes(w_kv.shape, F32),
                                _nbytes((tm, D_MODEL), F32) + _nbytes((tm, 2 * D_MODEL), BF16), (tm, 2 * D_MODEL), 2),
        name="mem_kv",
    )(mem, gain, w_kv)


def _xattn_kernel(x_ref, xnext_ref, g_ref, wq_ref, kv_ref, wo_ref, o_ref, h_s, q_s):
    step = pl.program_id(0)
    cur, nxt = step % 2, (step + 1) % 2

    def head_cols(hd):
        return slice(hd * MEM_HEAD_DIM, (hd + 1) * MEM_HEAD_DIM)

    def q_proj(h, hd):
        return (_dot(h, wq_ref[:, head_cols(hd)]) * (MEM_HEAD_DIM ** -0.5)).astype(BF16)

    def logits(hd, q):
        return _dot_nt(q, kv_ref[0, :, head_cols(hd)])

    def context(hd, lg):
        e = jnp.exp(lg - jnp.max(lg, axis=1, keepdims=True))
        p = (e / jnp.sum(e, axis=1, keepdims=True)).astype(BF16)
        v = kv_ref[0, :, D_MODEL + hd * MEM_HEAD_DIM:D_MODEL + (hd + 1) * MEM_HEAD_DIM]
        return _dot(p, v).astype(BF16)

    def body(first):
        if first:
            h = _rms(x_ref[...], g_ref[...]).astype(BF16)
            qs = [q_proj(h, 0), q_proj(h, 1)]
        else:
            h = h_s[cur]
            qs = [q_s[cur, 0], q_s[cur, 1]]
        lgs = [logits(0, qs[0])]
        acc = x_ref[...]
        for hd in range(MEM_HEADS):
            if hd + 2 < MEM_HEADS:
                qs.append(q_proj(h, hd + 2))
            if hd + 1 < MEM_HEADS:
                lgs.append(logits(hd + 1, qs[hd + 1]))
            acc = acc + _dot(context(hd, lgs[hd]), wo_ref[head_cols(hd), :])
        h_next = _rms(xnext_ref[...], g_ref[...]).astype(BF16)
        h_s[nxt] = h_next
        q_s[nxt, 0] = q_proj(h_next, 0)
        q_s[nxt, 1] = q_proj(h_next, 1)
        o_ref[...] = acc

    @pl.when(step == 0)
    def _():
        body(True)

    @pl.when(step != 0)
    def _():
        body(False)


def _xattn(x, gain, w_q, kv, w_o, *, seq_len):
    t = x.shape[0]
    tm = XATTN_TOKEN_TILE
    tiles_per_seq = seq_len // tm
    scratch = [pltpu.VMEM((2, tm, D_MODEL), BF16), pltpu.VMEM((2, 2, tm, MEM_HEAD_DIM), BF16)]
    return pl.pallas_call(
        _xattn_kernel,
        grid=(t // tm,),
        in_specs=[
            pl.BlockSpec((tm, D_MODEL), lambda i: (i, 0)),
            pl.BlockSpec((tm, D_MODEL), lambda i: (jnp.minimum(i + 1, t // tm - 1), 0)),
            _resident((1, D_MODEL)),
            _resident((D_MODEL, D_MODEL)),
            pl.BlockSpec((1, N_MEM, 2 * D_MODEL), lambda i: (i // tiles_per_seq, 0, 0)),
            _resident((D_MODEL, D_MODEL)),
        ],
        out_specs=pl.BlockSpec((tm, D_MODEL), lambda i: (i, 0)),
        out_shape=jax.ShapeDtypeStruct((t, D_MODEL), F32),
        scratch_shapes=scratch,
        compiler_params=_params(
            sum(_nbytes(b.shape, b.dtype) for b in scratch) + _nbytes(w_q.shape, F32) + _nbytes(w_o.shape, F32),
            3 * _nbytes((tm, D_MODEL), F32) + _nbytes((N_MEM, 2 * D_MODEL), BF16), (tm, D_MODEL), 3),
        name="xattn",
    )(x, x, gain, w_q, kv, w_o)


def kernel(x, mem, positions, rel_bias, ffn1_norm, ffn1_w_gu, ffn1_w_down, mix_norm, w_in, sinks, conv_w, w_out, xattn_norm, mem_norm, xattn_wq, xattn_wkv, xattn_wo, ffn2_norm, ffn2_w_gu, ffn2_w_down, final_norm):
    batch, seq_len, _ = x.shape
    depth = w_in.shape[0]
    tokens = batch * seq_len
    nb = seq_len // BLOCK

    xt = x.reshape(tokens, D_MODEL)
    posq = positions.reshape(tokens // TOKEN_TILE, TOKEN_TILE // BLOCK, BLOCK)
    pos_blocks = positions.reshape(batch, nb, BLOCK)
    prev = jnp.concatenate([jnp.full((batch, 1, BLOCK), POS_PAD, positions.dtype), pos_blocks[:, :-1]], axis=1)
    posk = jnp.concatenate([prev, pos_blocks], axis=2).reshape(tokens // TOKEN_TILE, TOKEN_TILE // BLOCK, 2 * BLOCK)
    lut = _bias_lut(rel_bias.T)
    final_gain = final_norm.reshape(1, D_MODEL)

    for l in range(depth):
        xt = _ffn(xt, ffn1_norm[l].reshape(1, D_MODEL), ffn1_w_gu[l], ffn1_w_down[l],
                  final_gain, final_norm=False)
        kv = _mem_kv(mem.reshape(batch * N_MEM, D_MODEL), mem_norm[l].reshape(1, D_MODEL), xattn_wkv[l])
        xt = _mix(xt, posq, posk, lut, sinks[l], mix_norm[l].reshape(1, D_MODEL),
                  w_in[l], conv_w[l], w_out[l], seq_len=seq_len)
        xt = _xattn(xt, xattn_norm[l].reshape(1, D_MODEL), xattn_wq[l],
                    kv.reshape(batch, N_MEM, 2 * D_MODEL), xattn_wo[l], seq_len=seq_len)
        xt = _ffn(xt, ffn2_norm[l].reshape(1, D_MODEL), ffn2_w_gu[l], ffn2_w_down[l],
                  final_gain, final_norm=(l == depth - 1))
    return xt.reshape(batch, seq_len, D_MODEL)
```

```python
import functools
import math

import jax
import jax.numpy as jnp
from jax import lax
from jax.experimental import pallas as pl
from jax.experimental.pallas import tpu as pltpu

D_MODEL = 1024
D_FF = 2816
N_MEM = 256
MEM_HEADS = 4
MEM_HEAD_DIM = D_MODEL // MEM_HEADS
HEAD_DIM = 64
SWA_HEADS = 16
SWA_KV_HEADS = 4
WINDOW = 128
BLOCK = 128
REL_BUCKETS = 32
REL_MAX_DIST = 128
CONV_K = 3
EPS = 1e-6
NEG_INF = -1e30
POS_PAD = 1 << 30

Q_WIDTH = SWA_HEADS * HEAD_DIM
KV_WIDTH = SWA_KV_HEADS * HEAD_DIM
OFF_Q = 0
OFF_K = OFF_Q + Q_WIDTH
OFF_V = OFF_K + KV_WIDTH
OFF_C = OFF_V + KV_WIDTH
OFF_B = OFF_C + D_MODEL
OFF_U = OFF_B + D_MODEL
OFF_GA = OFF_U + D_MODEL
OFF_GC = OFF_GA + D_MODEL
IN_WIDTH = OFF_GC + D_MODEL

SUBLANES = 8
MXU_COLS = 256
V7X_VMEM_BYTES = 64 * 1024 * 1024
VMEM_REQUEST_CAP = V7X_VMEM_BYTES - 4 * 1024 * 1024

TOKEN_TILE = 512
XATTN_TOKEN_TILE = 1024
FFN_TOKEN_TILE = 512
FF_CHUNK = MXU_COLS
PAIR = 2 * HEAD_DIM

BF16 = jnp.bfloat16
F32 = jnp.float32


def _rms(x, g):
    return x * lax.rsqrt(jnp.mean(x * x, axis=-1, keepdims=True) + EPS) * g


def _dot(a, b):
    return jnp.dot(a.astype(BF16), b.astype(BF16), preferred_element_type=F32)


def _dot_nt(a, b):
    return lax.dot_general(a.astype(BF16), b.astype(BF16), (((1,), (1,)), ((), ())),
                           preferred_element_type=F32)


def _resident(shape):
    return pl.BlockSpec(shape, lambda i: (0,) * len(shape), pipeline_mode=pl.Buffered(1))


def _nbytes(shape, dtype):
    return math.prod(shape) * jnp.dtype(dtype).itemsize


def _params(resident_bytes, per_step_bytes, tile_shape, tile_temporaries):
    request = resident_bytes + 2 * per_step_bytes + tile_temporaries * _nbytes(tile_shape, F32)
    return pltpu.CompilerParams(dimension_semantics=("arbitrary",),
                                vmem_limit_bytes=min(VMEM_REQUEST_CAP, request))


FF_CHUNKS = D_FF // FF_CHUNK
OUT_CHUNKS = D_MODEL // MXU_COLS


def _ffn_kernel(x_ref, xnext_ref, g_ref, wgu_hbm, wd_hbm, fg_ref, o_ref, wgu_v, wd_v, act_ref, h_s, act0_s, sems,
                *, final_norm):
    def gate_copy(j):
        return pltpu.make_async_copy(wgu_hbm.at[:, pl.ds(j * FF_CHUNK, FF_CHUNK)], wgu_v.at[j], sems.at[j])

    def up_copy(j):
        return pltpu.make_async_copy(wgu_hbm.at[:, pl.ds(D_FF + j * FF_CHUNK, FF_CHUNK)],
                                     wgu_v.at[FF_CHUNKS + j], sems.at[FF_CHUNKS + j])

    def down_copy(c):
        return pltpu.make_async_copy(wd_hbm.at[:, pl.ds(c * MXU_COLS, MXU_COLS)], wd_v.at[c],
                                     sems.at[2 * FF_CHUNKS + c])

    step = pl.program_id(0)
    cur, nxt = step % 2, (step + 1) % 2

    def chunk_act(h, j):
        gate = _dot(h, wgu_v[j])
        up = _dot(h, wgu_v[FF_CHUNKS + j])
        return (gate * jax.nn.sigmoid(gate) * up).astype(BF16)

    def body(first):
        if first:
            h = _rms(x_ref[...], g_ref[...]).astype(BF16)
            gate_copy(0).wait()
            up_copy(0).wait()
            act_ref[:, 0:FF_CHUNK] = chunk_act(h, 0)
        else:
            h = h_s[cur]
            act_ref[:, 0:FF_CHUNK] = act0_s[cur]
        for j in range(1, FF_CHUNKS):
            if first:
                gate_copy(j).wait()
                up_copy(j).wait()
            act_ref[:, j * FF_CHUNK:(j + 1) * FF_CHUNK] = chunk_act(h, j)
        if first:
            for c in range(OUT_CHUNKS):
                down_copy(c).wait()
        y = x_ref[...] + 0.5 * jnp.concatenate([_dot(act_ref[...], wd_v[c]) for c in range(OUT_CHUNKS)], axis=1)
        h_next = _rms(xnext_ref[...], g_ref[...]).astype(BF16)
        h_s[nxt] = h_next
        act0_s[nxt] = chunk_act(h_next, 0)
        if final_norm:
            y = _rms(y, fg_ref[...])
        o_ref[...] = y

    first_step = step == 0

    @pl.when(first_step)
    def _():
        for j in range(FF_CHUNKS):
            gate_copy(j).start()
            up_copy(j).start()
        for c in range(OUT_CHUNKS):
            down_copy(c).start()
        body(True)

    @pl.when(jnp.logical_not(first_step))
    def _():
        body(False)


def _ffn(x, gain, w_gu, w_down, final_gain, *, final_norm):
    t = x.shape[0]
    tm = FFN_TOKEN_TILE
    scratch = [
        pltpu.VMEM((2 * FF_CHUNKS, D_MODEL, FF_CHUNK), F32),
        pltpu.VMEM((OUT_CHUNKS, D_FF, MXU_COLS), F32),
        pltpu.VMEM((tm, D_FF), BF16),
        pltpu.VMEM((2, tm, D_MODEL), BF16),
        pltpu.VMEM((2, tm, FF_CHUNK), BF16),
    ]
    return pl.pallas_call(
        functools.partial(_ffn_kernel, final_norm=final_norm),
        grid=(t // tm,),
        in_specs=[
            pl.BlockSpec((tm, D_MODEL), lambda i: (i, 0)),
            pl.BlockSpec((tm, D_MODEL), lambda i: (jnp.minimum(i + 1, t // tm - 1), 0)),
            _resident((1, D_MODEL)),
            pl.BlockSpec(memory_space=pl.ANY),
            pl.BlockSpec(memory_space=pl.ANY),
            _resident((1, D_MODEL)),
        ],
        out_specs=pl.BlockSpec((tm, D_MODEL), lambda i: (i, 0)),
        out_shape=jax.ShapeDtypeStruct((t, D_MODEL), F32),
        scratch_shapes=scratch + [pltpu.SemaphoreType.DMA((2 * FF_CHUNKS + OUT_CHUNKS,))],
        compiler_params=_params(sum(_nbytes(b.shape, b.dtype) for b in scratch),
                                3 * _nbytes((tm, D_MODEL), F32), (tm, D_MODEL), 2),
        name="ffn_final" if final_norm else "ffn",
    )(x, x, gain, w_gu, w_down, final_gain)


def _bias_lut_kernel(tab_ref, lut_ref):
    n = lax.broadcasted_iota(jnp.int32, (SWA_HEADS, WINDOW), 1).astype(F32)
    max_exact = REL_BUCKETS // 2
    nf = jnp.maximum(n, 1.0)
    large = max_exact + jnp.floor(jnp.log(nf / max_exact) / math.log(REL_MAX_DIST / max_exact)
                                  * (REL_BUCKETS - max_exact))
    large = jnp.minimum(large, REL_BUCKETS - 1.0)
    bucket = jnp.where(n < max_exact, n, large)
    lut = jnp.broadcast_to(tab_ref[:, 0:1], (SWA_HEADS, WINDOW))
    for b in range(1, REL_BUCKETS):
        lut = jnp.where(bucket >= b, tab_ref[:, b:b + 1], lut)
    lut_ref[...] = lut


def _bias_lut(rel_bias_t):
    return pl.pallas_call(
        _bias_lut_kernel,
        out_shape=jax.ShapeDtypeStruct((SWA_HEADS, WINDOW), F32),
        name="bias_lut",
    )(rel_bias_t)


def _key_head_slabs(t):
    lane = lax.broadcasted_iota(jnp.int32, (t.shape[0], PAIR), 1)
    low_half = lane < HEAD_DIM
    los, his = [], []
    for p in range(SWA_KV_HEADS // 2):
        pair = t[:, p * PAIR:(p + 1) * PAIR]
        swapped = pltpu.roll(pair, HEAD_DIM, axis=1)
        zero = jnp.zeros_like(pair)
        los += [jnp.where(low_half, pair, zero), jnp.where(low_half, swapped, zero)]
        his += [jnp.where(low_half, zero, swapped), jnp.where(low_half, zero, pair)]
    return [a.astype(BF16) for a in los], [a.astype(BF16) for a in his]


def _value_head_slabs_t(t):
    los, his = [], []
    for p in range(SWA_KV_HEADS // 2):
        pair_t = t[:, p * PAIR:(p + 1) * PAIR].T
        zero = jnp.zeros((HEAD_DIM, t.shape[0]), F32)
        for head_t in (pair_t[0:HEAD_DIM], pair_t[HEAD_DIM:PAIR]):
            los.append(jnp.concatenate([head_t, zero], axis=0))
            his.append(jnp.concatenate([zero, head_t], axis=0))
    return [a.astype(BF16) for a in los], [a.astype(BF16) for a in his]


def _mix_kernel(x_ref, posq_ref, posk_ref, lut_ref, sink_ref, g_ref, win_ref, convw_ref, wout_ref, o_ref,
                q_s, klo_s, khi_s, vtlo_s, vthi_s, cu_s, merged_s, relm_s, relm_next_s, bias_s, *, tiles_per_seq):
    tm = x_ref.shape[0]
    nblk = tm // BLOCK
    step = pl.program_id(0)
    first = (step % tiles_per_seq) == 0
    next_first = ((step + 1) % tiles_per_seq) == 0
    next_step = jnp.minimum(step + 1, pl.num_programs(0) - 1)

    @pl.when(first)
    def _():
        for s in (klo_s, khi_s):
            s[:, 0:BLOCK, :] = jnp.zeros((SWA_KV_HEADS, BLOCK, PAIR), BF16)
        for s in (vtlo_s, vthi_s):
            s[:, 0] = jnp.zeros((SWA_KV_HEADS, PAIR, BLOCK), BF16)
        cu_s[0:SUBLANES, :] = jnp.zeros((SUBLANES, D_MODEL), F32)

    @pl.when(jnp.logical_not(first))
    def _():
        for s in (klo_s, khi_s):
            s[:, 0:BLOCK, :] = s[:, tm:tm + BLOCK, :]
        for s in (vtlo_s, vthi_s):
            s[:, 0] = s[:, nblk]
        cu_s[0:SUBLANES, :] = cu_s[tm:tm + SUBLANES, :]

    def masked_rel(tile, j):
        pos_k = jnp.concatenate(
            [jnp.broadcast_to(posk_ref[tile, j:j + 1, c * BLOCK:(c + 1) * BLOCK], (BLOCK, BLOCK)).T
             for c in range(2)], axis=0)
        rel = posq_ref[tile, j:j + 1, :] - pos_k
        return jnp.where((rel >= 0) & (rel < WINDOW), rel, -1)

    def block_slots(sequence_start):
        return [jnp.where(sequence_start, nblk, 0)] + list(range(1, nblk))

    def rebuild(slot, relm):
        relm_s[slot] = relm
        shown = relm >= 0
        idx = jnp.maximum(relm, 0)
        for hd in range(SWA_HEADS):
            lut = jnp.broadcast_to(lut_ref[hd:hd + 1, :], (2 * BLOCK, WINDOW))
            bias_s[slot, hd] = jnp.where(shown, jnp.take_along_axis(lut, idx, axis=1), NEG_INF)

    slots = block_slots(first)
    next_slots = block_slots(next_first)

    @pl.when(step == 0)
    def _():
        relm_s[...] = jnp.full(relm_s.shape, -2, jnp.int32)
        for j in range(nblk):
            rebuild(slots[j], masked_rel(step, j))

    next_misses = []
    for j in range(nblk):
        relm = masked_rel(next_step, j)
        relm_next_s[j] = relm
        next_misses.append(jnp.sum((relm != relm_s[next_slots[j]]).astype(jnp.int32)))

    h = _rms(x_ref[...], g_ref[...]).astype(BF16)

    k_lo, k_hi = _key_head_slabs(_dot(h, win_ref[:, OFF_K:OFF_K + KV_WIDTH]))
    vt_lo, vt_hi = _value_head_slabs_t(_dot(h, win_ref[:, OFF_V:OFF_V + KV_WIDTH]))
    for g in range(SWA_KV_HEADS):
        klo_s[g, BLOCK:BLOCK + tm, :] = k_lo[g]
        khi_s[g, BLOCK:BLOCK + tm, :] = k_hi[g]
        for j in range(nblk):
            vtlo_s[g, j + 1] = vt_lo[g][:, j * BLOCK:(j + 1) * BLOCK]
            vthi_s[g, j + 1] = vt_hi[g][:, j * BLOCK:(j + 1) * BLOCK]
    q_s[...] = (_dot(h, win_ref[:, OFF_Q:OFF_Q + Q_WIDTH]) * (HEAD_DIM ** -0.5)).astype(BF16)

    def qk(blk, g):
        rows = slice(blk * BLOCK, (blk + 1) * BLOCK)
        keys = slice(blk * BLOCK, (blk + 2) * BLOCK)
        c0 = g * 2 * PAIR
        q2 = jnp.concatenate([q_s[rows, c0:c0 + PAIR], q_s[rows, c0 + PAIR:c0 + 2 * PAIR]], axis=0)
        return _dot_nt(klo_s[g, keys, :], q2), _dot_nt(khi_s[g, keys, :], q2)

    def softmax_pv(blk, g, logits_t):
        probs_t = [[None, None], [None, None]]
        inv = [[None, None], [None, None]]
        for half in range(2):
            for pair in range(2):
                head = g * 4 + pair * 2 + half
                lt = logits_t[half][:, pair * BLOCK:(pair + 1) * BLOCK] + bias_s[slots[blk], head]
                sink = sink_ref[head]
                m = jnp.maximum(jnp.max(lt, axis=0, keepdims=True), sink)
                e = jnp.exp(lt - m)
                inv[half][pair] = 1.0 / (jnp.sum(e, axis=0, keepdims=True) + jnp.exp(sink - m))
                probs_t[half][pair] = e.astype(BF16)
        v_lo_t = jnp.concatenate([vtlo_s[g, blk], vtlo_s[g, blk + 1]], axis=1)
        v_hi_t = jnp.concatenate([vthi_s[g, blk], vthi_s[g, blk + 1]], axis=1)
        out_t = (_dot(v_lo_t, jnp.concatenate(probs_t[0], axis=1))
                 + _dot(v_hi_t, jnp.concatenate(probs_t[1], axis=1)))
        scale = jnp.concatenate(
            [jnp.broadcast_to(jnp.concatenate(inv[0], axis=1), (HEAD_DIM, 2 * BLOCK)),
             jnp.broadcast_to(jnp.concatenate(inv[1], axis=1), (HEAD_DIM, 2 * BLOCK))], axis=0)
        return (out_t * scale).T

    units = [(blk, g) for g in range(SWA_KV_HEADS) for blk in range(nblk)]
    logits_next = qk(*units[0])
    for g in range(SWA_KV_HEADS):
        cols = slice(g * 2 * PAIR, (g + 1) * 2 * PAIR)

        def proj(off):
            return _dot(h, win_ref[:, off + g * 2 * PAIR:off + (g + 1) * 2 * PAIR])

        def conv_taps(_):
            cu = proj(OFF_C) * proj(OFF_U)
            cu_s[SUBLANES:SUBLANES + tm, cols] = cu
            return (convw_ref[0:1, cols] * cu_s[SUBLANES - 2:SUBLANES - 2 + tm, cols]
                    + convw_ref[1:2, cols] * cu_s[SUBLANES - 1:SUBLANES - 1 + tm, cols]
                    + convw_ref[2:3, cols] * cu)

        stages = [conv_taps,
                  lambda conv: proj(OFF_B) * conv,
                  lambda conv: jax.nn.sigmoid(proj(OFF_GC)) * conv]
        attn_blocks = []
        conv = None
        for blk in range(nblk):
            logits_t = logits_next
            nxt = g * nblk + blk + 1
            if nxt < len(units):
                logits_next = qk(*units[nxt])
            if blk < len(stages):
                conv = stages[blk](conv)
            out = softmax_pv(blk, g, logits_t)
            attn_blocks.append(jnp.concatenate([out[0:BLOCK], out[BLOCK:2 * BLOCK]], axis=1))
        for stage in stages[nblk:]:
            conv = stage(conv)
        attn = jnp.concatenate(attn_blocks, axis=0)
        merged_s[:, cols] = (jax.nn.sigmoid(proj(OFF_GA)) * attn + conv).astype(BF16)

    o_ref[...] = x_ref[...] + _dot(merged_s[...], wout_ref[...])

    for j in range(nblk):
        @pl.when(next_misses[j] != 0)
        def _(j=j):
            rebuild(next_slots[j], relm_next_s[j])


def _mix(x, posq, posk, lut, sinks, gain, w_in, conv_w, w_out, *, seq_len):
    t = x.shape[0]
    tm = TOKEN_TILE
    nblk = tm // BLOCK
    k_scratch = pltpu.VMEM((SWA_KV_HEADS, BLOCK + tm, PAIR), BF16)
    vt_scratch = pltpu.VMEM((SWA_KV_HEADS, nblk + 1, PAIR, BLOCK), BF16)
    scratch = [
        pltpu.VMEM((tm, Q_WIDTH), BF16),
        k_scratch, k_scratch, vt_scratch, vt_scratch,
        pltpu.VMEM((SUBLANES + tm, D_MODEL), F32),
        pltpu.VMEM((tm, D_MODEL), BF16),
        pltpu.VMEM((nblk + 1, 2 * BLOCK, BLOCK), jnp.int32),
        pltpu.VMEM((nblk, 2 * BLOCK, BLOCK), jnp.int32),
        pltpu.VMEM((nblk + 1, SWA_HEADS, 2 * BLOCK, BLOCK), F32),
    ]
    return pl.pallas_call(
        functools.partial(_mix_kernel, tiles_per_seq=seq_len // tm),
        grid=(t // tm,),
        in_specs=[
            pl.BlockSpec((tm, D_MODEL), lambda i: (i, 0)),
            _resident((t // tm, nblk, BLOCK)),
            _resident((t // tm, nblk, 2 * BLOCK)),
            _resident((SWA_HEADS, WINDOW)),
            pl.BlockSpec(memory_space=pltpu.SMEM),
            _resident((1, D_MODEL)),
            _resident((D_MODEL, IN_WIDTH)),
            _resident((CONV_K, D_MODEL)),
            _resident((D_MODEL, D_MODEL)),
        ],
        out_specs=pl.BlockSpec((tm, D_MODEL), lambda i: (i, 0)),
        out_shape=jax.ShapeDtypeStruct((t, D_MODEL), F32),
        scratch_shapes=scratch,
        compiler_params=_params(
            sum(_nbytes(b.shape, b.dtype) for b in scratch) + _nbytes(w_in.shape, F32) + _nbytes(w_out.shape, F32),
            2 * _nbytes((tm, D_MODEL), F32), (tm, D_MODEL), 4),
        name="mix",
    )(x, posq, posk, lut, sinks, gain, w_in, conv_w, w_out)


def _mem_kv_kernel(mem_ref, g_ref, wkv_ref, kv_ref):
    h = _rms(mem_ref[...], g_ref[...]).astype(BF16)
    kv_ref[...] = _dot(h, wkv_ref[...]).astype(BF16)


def _mem_kv(mem, gain, w_kv):
    rows = mem.shape[0]
    tm = TOKEN_TILE
    return pl.pallas_call(
        _mem_kv_kernel,
        grid=(rows // tm,),
        in_specs=[
            pl.BlockSpec((tm, D_MODEL), lambda i: (i, 0)),
            _resident((1, D_MODEL)),
            _resident((D_MODEL, 2 * D_MODEL)),
        ],
        out_specs=pl.BlockSpec((tm, 2 * D_MODEL), lambda i: (i, 0)),
        out_shape=jax.ShapeDtypeStruct((rows, 2 * D_MODEL), BF16),
        compiler_params=_params(_nbytes(w_kv.shape, F32),
                                _nbytes((tm, D_MODEL), F32) + _nbytes((tm, 2 * D_MODEL), BF16), (tm, 2 * D_MODEL), 2),
        name="mem_kv",
    )(mem, gain, w_kv)


def _xattn_kernel(x_ref, xnext_ref, g_ref, wq_ref, kv_ref, wo_ref, o_ref, h_s, q_s):
    step = pl.program_id(0)
    cur, nxt = step % 2, (step + 1) % 2

    def head_cols(hd):
        return slice(hd * MEM_HEAD_DIM, (hd + 1) * MEM_HEAD_DIM)

    def q_proj(h, hd):
        return (_dot(h, wq_ref[:, head_cols(hd)]) * (MEM_HEAD_DIM ** -0.5)).astype(BF16)

    def logits(hd, q):
        return _dot_nt(q, kv_ref[0, :, head_cols(hd)])

    def context(hd, lg):
        e = jnp.exp(lg - jnp.max(lg, axis=1, keepdims=True))
        p = (e / jnp.sum(e, axis=1, keepdims=True)).astype(BF16)
        v = kv_ref[0, :, D_MODEL + hd * MEM_HEAD_DIM:D_MODEL + (hd + 1) * MEM_HEAD_DIM]
        return _dot(p, v).astype(BF16)

    def body(first):
        if first:
            h = _rms(x_ref[...], g_ref[...]).astype(BF16)
            qs = [q_proj(h, 0), q_proj(h, 1)]
        else:
            h = h_s[cur]
            qs = [q_s[cur, 0], q_s[cur, 1]]
        lgs = [logits(0, qs[0])]
        acc = x_ref[...]
        for hd in range(MEM_HEADS):
            if hd + 2 < MEM_HEADS:
                qs.append(q_proj(h, hd + 2))
            if hd + 1 < MEM_HEADS:
                lgs.append(logits(hd + 1, qs[hd + 1]))
            acc = acc + _dot(context(hd, lgs[hd]), wo_ref[head_cols(hd), :])
        h_next = _rms(xnext_ref[...], g_ref[...]).astype(BF16)
        h_s[nxt] = h_next
        q_s[nxt, 0] = q_proj(h_next, 0)
        q_s[nxt, 1] = q_proj(h_next, 1)
        o_ref[...] = acc

    @pl.when(step == 0)
    def _():
        body(True)

    @pl.when(step != 0)
    def _():
        body(False)


def _xattn(x, gain, w_q, kv, w_o, *, seq_len):
    t = x.shape[0]
    tm = XATTN_TOKEN_TILE
    tiles_per_seq = seq_len // tm
    scratch = [pltpu.VMEM((2, tm, D_MODEL), BF16), pltpu.VMEM((2, 2, tm, MEM_HEAD_DIM), BF16)]
    return pl.pallas_call(
        _xattn_kernel,
        grid=(t // tm,),
        in_specs=[
            pl.BlockSpec((tm, D_MODEL), lambda i: (i, 0)),
            pl.BlockSpec((tm, D_MODEL), lambda i: (jnp.minimum(i + 1, t // tm - 1), 0)),
            _resident((1, D_MODEL)),
            _resident((D_MODEL, D_MODEL)),
            pl.BlockSpec((1, N_MEM, 2 * D_MODEL), lambda i: (i // tiles_per_seq, 0, 0)),
            _resident((D_MODEL, D_MODEL)),
        ],
        out_specs=pl.BlockSpec((tm, D_MODEL), lambda i: (i, 0)),
        out_shape=jax.ShapeDtypeStruct((t, D_MODEL), F32),
        scratch_shapes=scratch,
        compiler_params=_params(
            sum(_nbytes(b.shape, b.dtype) for b in scratch) + _nbytes(w_q.shape, F32) + _nbytes(w_o.shape, F32),
            3 * _nbytes((tm, D_MODEL), F32) + _nbytes((N_MEM, 2 * D_MODEL), BF16), (tm, D_MODEL), 3),
        name="xattn",
    )(x, x, gain, w_q, kv, w_o)


def kernel(x, mem, positions, rel_bias, ffn1_norm, ffn1_w_gu, ffn1_w_down, mix_norm, w_in, sinks, conv_w, w_out, xattn_norm, mem_norm, xattn_wq, xattn_wkv, xattn_wo, ffn2_norm, ffn2_w_gu, ffn2_w_down, final_norm):
    batch, seq_len, _ = x.shape
    depth = w_in.shape[0]
    tokens = batch * seq_len
    nb = seq_len // BLOCK

    xt = x.reshape(tokens, D_MODEL)
    posq = positions.reshape(tokens // TOKEN_TILE, TOKEN_TILE // BLOCK, BLOCK)
    pos_blocks = positions.reshape(batch, nb, BLOCK)
    prev = jnp.concatenate([jnp.full((batch, 1, BLOCK), POS_PAD, positions.dtype), pos_blocks[:, :-1]], axis=1)
    posk = jnp.concatenate([prev, pos_blocks], axis=2).reshape(tokens // TOKEN_TILE, TOKEN_TILE // BLOCK, 2 * BLOCK)
    lut = _bias_lut(rel_bias.T)
    final_gain = final_norm.reshape(1, D_MODEL)

    for l in range(depth):
        xt = _ffn(xt, ffn1_norm[l].reshape(1, D_MODEL), ffn1_w_gu[l], ffn1_w_down[l],
                  final_gain, final_norm=False)
        kv = _mem_kv(mem.reshape(batch * N_MEM, D_MODEL), mem_norm[l].reshape(1, D_MODEL), xattn_wkv[l])
        xt = _mix(xt, posq, posk, lut, sinks[l], mix_norm[l].reshape(1, D_MODEL),
                  w_in[l], conv_w[l], w_out[l], seq_len=seq_len)
        xt = _xattn(xt, xattn_norm[l].reshape(1, D_MODEL), xattn_wq[l],
                    kv.reshape(batch, N_MEM, 2 * D_MODEL), xattn_wo[l], seq_len=seq_len)
        xt = _ffn(xt, ffn2_norm[l].reshape(1, D_MODEL), ffn2_w_gu[l], ffn2_w_down[l],
                  final_gain, final_norm=(l == depth - 1))
    return xt.reshape(batch, seq_len, D_MODEL)
```

```python
import functools
import math

import jax
import jax.numpy as jnp
from jax import lax
from jax.experimental import pallas as pl
from jax.experimental.pallas import tpu as pltpu

D_MODEL = 1024
D_FF = 2816
N_MEM = 256
MEM_HEADS = 4
MEM_HEAD_DIM = D_MODEL // MEM_HEADS
HEAD_DIM = 64
SWA_HEADS = 16
SWA_KV_HEADS = 4
WINDOW = 128
BLOCK = 128
REL_BUCKETS = 32
REL_MAX_DIST = 128
CONV_K = 3
EPS = 1e-6
NEG_INF = -1e30
POS_PAD = 1 << 30

Q_WIDTH = SWA_HEADS * HEAD_DIM
KV_WIDTH = SWA_KV_HEADS * HEAD_DIM
OFF_Q = 0
OFF_K = OFF_Q + Q_WIDTH
OFF_V = OFF_K + KV_WIDTH
OFF_C = OFF_V + KV_WIDTH
OFF_B = OFF_C + D_MODEL
OFF_U = OFF_B + D_MODEL
OFF_GA = OFF_U + D_MODEL
OFF_GC = OFF_GA + D_MODEL
IN_WIDTH = OFF_GC + D_MODEL

SUBLANES = 8
MXU_COLS = 256
V7X_VMEM_BYTES = 64 * 1024 * 1024
VMEM_REQUEST_CAP = V7X_VMEM_BYTES - 4 * 1024 * 1024

TOKEN_TILE = 512
XATTN_TOKEN_TILE = 1024
FFN_TOKEN_TILE = 512
FF_CHUNK = MXU_COLS
PAIR = 2 * HEAD_DIM

BF16 = jnp.bfloat16
F32 = jnp.float32


def _rms(x, g):
    return x * lax.rsqrt(jnp.mean(x * x, axis=-1, keepdims=True) + EPS) * g


def _dot(a, b):
    return jnp.dot(a.astype(BF16), b.astype(BF16), preferred_element_type=F32)


def _dot_nt(a, b):
    return lax.dot_general(a.astype(BF16), b.astype(BF16), (((1,), (1,)), ((), ())),
                           preferred_element_type=F32)


def _resident(shape):
    return pl.BlockSpec(shape, lambda i: (0,) * len(shape), pipeline_mode=pl.Buffered(1))


def _nbytes(shape, dtype):
    return math.prod(shape) * jnp.dtype(dtype).itemsize


def _params(resident_bytes, per_step_bytes, tile_shape, tile_temporaries):
    request = resident_bytes + 2 * per_step_bytes + tile_temporaries * _nbytes(tile_shape, F32)
    return pltpu.CompilerParams(dimension_semantics=("arbitrary",),
                                vmem_limit_bytes=min(VMEM_REQUEST_CAP, request))


FF_CHUNKS = D_FF // FF_CHUNK
OUT_CHUNKS = D_MODEL // MXU_COLS


def _ffn_kernel(x_ref, xnext_ref, g_ref, wgu_hbm, wd_hbm, fg_ref, o_ref, wgu_v, wd_v, act_ref, h_s, act0_s, sems,
                *, final_norm):
    def gate_copy(j):
        return pltpu.make_async_copy(wgu_hbm.at[:, pl.ds(j * FF_CHUNK, FF_CHUNK)], wgu_v.at[j], sems.at[j])

    def up_copy(j):
        return pltpu.make_async_copy(wgu_hbm.at[:, pl.ds(D_FF + j * FF_CHUNK, FF_CHUNK)],
                                     wgu_v.at[FF_CHUNKS + j], sems.at[FF_CHUNKS + j])

    def down_copy(c):
        return pltpu.make_async_copy(wd_hbm.at[:, pl.ds(c * MXU_COLS, MXU_COLS)], wd_v.at[c],
                                     sems.at[2 * FF_CHUNKS + c])

    step = pl.program_id(0)
    cur, nxt = step % 2, (step + 1) % 2

    def chunk_act(h, j):
        gate = _dot(h, wgu_v[j])
        up = _dot(h, wgu_v[FF_CHUNKS + j])
        return (gate * jax.nn.sigmoid(gate) * up).astype(BF16)

    def body(first):
        if first:
            h = _rms(x_ref[...], g_ref[...]).astype(BF16)
            gate_copy(0).wait()
            up_copy(0).wait()
            act_ref[:, 0:FF_CHUNK] = chunk_act(h, 0)
        else:
            h = h_s[cur]
            act_ref[:, 0:FF_CHUNK] = act0_s[cur]
        for j in range(1, FF_CHUNKS):
            if first:
                gate_copy(j).wait()
                up_copy(j).wait()
            act_ref[:, j * FF_CHUNK:(j + 1) * FF_CHUNK] = chunk_act(h, j)
        if first:
            for c in range(OUT_CHUNKS):
                down_copy(c).wait()
        y = x_ref[...] + 0.5 * jnp.concatenate([_dot(act_ref[...], wd_v[c]) for c in range(OUT_CHUNKS)], axis=1)
        h_next = _rms(xnext_ref[...], g_ref[...]).astype(BF16)
        h_s[nxt] = h_next
        act0_s[nxt] = chunk_act(h_next, 0)
        if final_norm:
            y = _rms(y, fg_ref[...])
        o_ref[...] = y

    first_step = step == 0

    @pl.when(first_step)
    def _():
        for j in range(FF_CHUNKS):
            gate_copy(j).start()
            up_copy(j).start()
        for c in range(OUT_CHUNKS):
            down_copy(c).start()
        body(True)

    @pl.when(jnp.logical_not(first_step))
    def _():
        body(False)


def _ffn(x, gain, w_gu, w_down, final_gain, *, final_norm):
    t = x.shape[0]
    tm = FFN_TOKEN_TILE
    scratch = [
        pltpu.VMEM((2 * FF_CHUNKS, D_MODEL, FF_CHUNK), F32),
        pltpu.VMEM((OUT_CHUNKS, D_FF, MXU_COLS), F32),
        pltpu.VMEM((tm, D_FF), BF16),
        pltpu.VMEM((2, tm, D_MODEL), BF16),
        pltpu.VMEM((2, tm, FF_CHUNK), BF16),
    ]
    return pl.pallas_call(
        functools.partial(_ffn_kernel, final_norm=final_norm),
        grid=(t // tm,),
        in_specs=[
            pl.BlockSpec((tm, D_MODEL), lambda i: (i, 0)),
            pl.BlockSpec((tm, D_MODEL), lambda i: (jnp.minimum(i + 1, t // tm - 1), 0)),
            _resident((1, D_MODEL)),
            pl.BlockSpec(memory_space=pl.ANY),
            pl.BlockSpec(memory_space=pl.ANY),
            _resident((1, D_MODEL)),
        ],
        out_specs=pl.BlockSpec((tm, D_MODEL), lambda i: (i, 0)),
        out_shape=jax.ShapeDtypeStruct((t, D_MODEL), F32),
        scratch_shapes=scratch + [pltpu.SemaphoreType.DMA((2 * FF_CHUNKS + OUT_CHUNKS,))],
        compiler_params=_params(sum(_nbytes(b.shape, b.dtype) for b in scratch),
                                3 * _nbytes((tm, D_MODEL), F32), (tm, D_MODEL), 2),
        name="ffn_final" if final_norm else "ffn",
    )(x, x, gain, w_gu, w_down, final_gain)


def _bias_lut(tab_ref):
    n = lax.broadcasted_iota(jnp.int32, (SWA_HEADS, WINDOW), 1).astype(F32)
    max_exact = REL_BUCKETS // 2
    nf = jnp.maximum(n, 1.0)
    large = max_exact + jnp.floor(jnp.log(nf / max_exact) / math.log(REL_MAX_DIST / max_exact)
                                  * (REL_BUCKETS - max_exact))
    large = jnp.minimum(large, REL_BUCKETS - 1.0)
    bucket = jnp.where(n < max_exact, n, large)
    lut = jnp.broadcast_to(tab_ref[:, 0:1], (SWA_HEADS, WINDOW))
    for b in range(1, REL_BUCKETS):
        lut = jnp.where(bucket >= b, tab_ref[:, b:b + 1], lut)
    return lut


def _key_head_slabs(t):
    lane = lax.broadcasted_iota(jnp.int32, (t.shape[0], PAIR), 1)
    low_half = lane < HEAD_DIM
    los, his = [], []
    for p in range(SWA_KV_HEADS // 2):
        pair = t[:, p * PAIR:(p + 1) * PAIR]
        swapped = pltpu.roll(pair, HEAD_DIM, axis=1)
        zero = jnp.zeros_like(pair)
        los += [jnp.where(low_half, pair, zero), jnp.where(low_half, swapped, zero)]
        his += [jnp.where(low_half, zero, swapped), jnp.where(low_half, zero, pair)]
    return [a.astype(BF16) for a in los], [a.astype(BF16) for a in his]


def _value_head_slabs_t(t):
    los, his = [], []
    for p in range(SWA_KV_HEADS // 2):
        pair_t = t[:, p * PAIR:(p + 1) * PAIR].T
        zero = jnp.zeros((HEAD_DIM, t.shape[0]), F32)
        for head_t in (pair_t[0:HEAD_DIM], pair_t[HEAD_DIM:PAIR]):
            los.append(jnp.concatenate([head_t, zero], axis=0))
            his.append(jnp.concatenate([zero, head_t], axis=0))
    return [a.astype(BF16) for a in los], [a.astype(BF16) for a in his]


def _mix_kernel(x_ref, posq_ref, posk_ref, posq_next_ref, posk_next_ref, tab_ref, sink_ref, g_ref, win_ref,
                convw_ref, wout_ref, o_ref,
                q_s, klo_s, khi_s, vtlo_s, vthi_s, cu_s, merged_s, relm_s, relm_next_s, bias_s, lut_s, *, tiles_per_seq):
    tm = x_ref.shape[0]
    nblk = tm // BLOCK
    step = pl.program_id(0)
    first = (step % tiles_per_seq) == 0
    next_first = ((step + 1) % tiles_per_seq) == 0

    @pl.when(first)
    def _():
        for s in (klo_s, khi_s):
            s[:, 0:BLOCK, :] = jnp.zeros((SWA_KV_HEADS, BLOCK, PAIR), BF16)
        for s in (vtlo_s, vthi_s):
            s[:, 0] = jnp.zeros((SWA_KV_HEADS, PAIR, BLOCK), BF16)
        cu_s[0:SUBLANES, :] = jnp.zeros((SUBLANES, D_MODEL), F32)

    @pl.when(jnp.logical_not(first))
    def _():
        for s in (klo_s, khi_s):
            s[:, 0:BLOCK, :] = s[:, tm:tm + BLOCK, :]
        for s in (vtlo_s, vthi_s):
            s[:, 0] = s[:, nblk]
        cu_s[0:SUBLANES, :] = cu_s[tm:tm + SUBLANES, :]

    def masked_rel(pq_ref, pk_ref, j):
        pos_k = jnp.concatenate(
            [jnp.broadcast_to(pk_ref[0, j:j + 1, c * BLOCK:(c + 1) * BLOCK], (BLOCK, BLOCK)).T
             for c in range(2)], axis=0)
        rel = pq_ref[0, j:j + 1, :] - pos_k
        return jnp.where((rel >= 0) & (rel < WINDOW), rel, -1)

    def block_slots(sequence_start):
        return [jnp.where(sequence_start, nblk, 0)] + list(range(1, nblk))

    def rebuild(slot, relm):
        relm_s[slot] = relm
        shown = relm >= 0
        idx = jnp.maximum(relm, 0)
        for hd in range(SWA_HEADS):
            lut = jnp.broadcast_to(lut_s[hd:hd + 1, :], (2 * BLOCK, WINDOW))
            bias_s[slot, hd] = jnp.where(shown, jnp.take_along_axis(lut, idx, axis=1), NEG_INF)

    slots = block_slots(first)
    next_slots = block_slots(next_first)

    @pl.when(step == 0)
    def _():
        relm_s[...] = jnp.full(relm_s.shape, -2, jnp.int32)
        lut_s[...] = _bias_lut(tab_ref)
        for j in range(nblk):
            rebuild(slots[j], masked_rel(posq_ref, posk_ref, j))

    next_misses = []
    for j in range(nblk):
        relm = masked_rel(posq_next_ref, posk_next_ref, j)
        relm_next_s[j] = relm
        next_misses.append(jnp.sum((relm != relm_s[next_slots[j]]).astype(jnp.int32)))

    h = _rms(x_ref[...], g_ref[...]).astype(BF16)

    k_lo, k_hi = _key_head_slabs(_dot(h, win_ref[:, OFF_K:OFF_K + KV_WIDTH]))
    vt_lo, vt_hi = _value_head_slabs_t(_dot(h, win_ref[:, OFF_V:OFF_V + KV_WIDTH]))
    for g in range(SWA_KV_HEADS):
        klo_s[g, BLOCK:BLOCK + tm, :] = k_lo[g]
        khi_s[g, BLOCK:BLOCK + tm, :] = k_hi[g]
        for j in range(nblk):
            vtlo_s[g, j + 1] = vt_lo[g][:, j * BLOCK:(j + 1) * BLOCK]
            vthi_s[g, j + 1] = vt_hi[g][:, j * BLOCK:(j + 1) * BLOCK]
    q_s[...] = (_dot(h, win_ref[:, OFF_Q:OFF_Q + Q_WIDTH]) * (HEAD_DIM ** -0.5)).astype(BF16)

    def qk(blk, g):
        rows = slice(blk * BLOCK, (blk + 1) * BLOCK)
        keys = slice(blk * BLOCK, (blk + 2) * BLOCK)
        c0 = g * 2 * PAIR
        q2 = jnp.concatenate([q_s[rows, c0:c0 + PAIR], q_s[rows, c0 + PAIR:c0 + 2 * PAIR]], axis=0)
        return _dot_nt(klo_s[g, keys, :], q2), _dot_nt(khi_s[g, keys, :], q2)

    def softmax_pv(blk, g, logits_t):
        probs_t = [[None, None], [None, None]]
        inv = [[None, None], [None, None]]
        for half in range(2):
            for pair in range(2):
                head = g * 4 + pair * 2 + half
                lt = logits_t[half][:, pair * BLOCK:(pair + 1) * BLOCK] + bias_s[slots[blk], head]
                sink = sink_ref[head]
                m = jnp.maximum(jnp.max(lt, axis=0, keepdims=True), sink)
                e = jnp.exp(lt - m)
                inv[half][pair] = 1.0 / (jnp.sum(e, axis=0, keepdims=True) + jnp.exp(sink - m))
                probs_t[half][pair] = e.astype(BF16)
        v_lo_t = jnp.concatenate([vtlo_s[g, blk], vtlo_s[g, blk + 1]], axis=1)
        v_hi_t = jnp.concatenate([vthi_s[g, blk], vthi_s[g, blk + 1]], axis=1)
        out_t = (_dot(v_lo_t, jnp.concatenate(probs_t[0], axis=1))
                 + _dot(v_hi_t, jnp.concatenate(probs_t[1], axis=1)))
        scale = jnp.concatenate(
            [jnp.broadcast_to(jnp.concatenate(inv[0], axis=1), (HEAD_DIM, 2 * BLOCK)),
             jnp.broadcast_to(jnp.concatenate(inv[1], axis=1), (HEAD_DIM, 2 * BLOCK))], axis=0)
        return (out_t * scale).T

    units = [(blk, g) for g in range(SWA_KV_HEADS) for blk in range(nblk)]
    logits_next = qk(*units[0])
    for g in range(SWA_KV_HEADS):
        cols = slice(g * 2 * PAIR, (g + 1) * 2 * PAIR)

        def proj(off):
            return _dot(h, win_ref[:, off + g * 2 * PAIR:off + (g + 1) * 2 * PAIR])

        def conv_taps(_):
            cu = proj(OFF_C) * proj(OFF_U)
            cu_s[SUBLANES:SUBLANES + tm, cols] = cu
            return (convw_ref[0:1, cols] * cu_s[SUBLANES - 2:SUBLANES - 2 + tm, cols]
                    + convw_ref[1:2, cols] * cu_s[SUBLANES - 1:SUBLANES - 1 + tm, cols]
                    + convw_ref[2:3, cols] * cu)

        stages = [conv_taps,
                  lambda conv: proj(OFF_B) * conv,
                  lambda conv: jax.nn.sigmoid(proj(OFF_GC)) * conv]
        attn_blocks = []
        conv = None
        for blk in range(nblk):
            logits_t = logits_next
            nxt = g * nblk + blk + 1
            if nxt < len(units):
                logits_next = qk(*units[nxt])
            if blk < len(stages):
                conv = stages[blk](conv)
            out = softmax_pv(blk, g, logits_t)
            attn_blocks.append(jnp.concatenate([out[0:BLOCK], out[BLOCK:2 * BLOCK]], axis=1))
        for stage in stages[nblk:]:
            conv = stage(conv)
        attn = jnp.concatenate(attn_blocks, axis=0)
        merged_s[:, cols] = (jax.nn.sigmoid(proj(OFF_GA)) * attn + conv).astype(BF16)

    o_ref[...] = x_ref[...] + _dot(merged_s[...], wout_ref[...])

    for j in range(nblk):
        @pl.when(next_misses[j] != 0)
        def _(j=j):
            rebuild(next_slots[j], relm_next_s[j])


def _mix(x, posq, posk, rel_bias_t, sinks, gain, w_in, conv_w, w_out, *, seq_len):
    t = x.shape[0]
    tm = TOKEN_TILE
    nblk = tm // BLOCK
    k_scratch = pltpu.VMEM((SWA_KV_HEADS, BLOCK + tm, PAIR), BF16)
    vt_scratch = pltpu.VMEM((SWA_KV_HEADS, nblk + 1, PAIR, BLOCK), BF16)
    scratch = [
        pltpu.VMEM((tm, Q_WIDTH), BF16),
        k_scratch, k_scratch, vt_scratch, vt_scratch,
        pltpu.VMEM((SUBLANES + tm, D_MODEL), F32),
        pltpu.VMEM((tm, D_MODEL), BF16),
        pltpu.VMEM((nblk + 1, 2 * BLOCK, BLOCK), jnp.int32),
        pltpu.VMEM((nblk, 2 * BLOCK, BLOCK), jnp.int32),
        pltpu.VMEM((nblk + 1, SWA_HEADS, 2 * BLOCK, BLOCK), F32),
        pltpu.VMEM((SWA_HEADS, WINDOW), F32),
    ]
    return pl.pallas_call(
        functools.partial(_mix_kernel, tiles_per_seq=seq_len // tm),
        grid=(t // tm,),
        in_specs=[
            pl.BlockSpec((tm, D_MODEL), lambda i: (i, 0)),
            pl.BlockSpec((1, nblk, BLOCK), lambda i: (i, 0, 0)),
            pl.BlockSpec((1, nblk, 2 * BLOCK), lambda i: (i, 0, 0)),
            pl.BlockSpec((1, nblk, BLOCK), lambda i: (jnp.minimum(i + 1, t // tm - 1), 0, 0)),
            pl.BlockSpec((1, nblk, 2 * BLOCK), lambda i: (jnp.minimum(i + 1, t // tm - 1), 0, 0)),
            _resident((SWA_HEADS, REL_BUCKETS)),
            pl.BlockSpec(memory_space=pltpu.SMEM),
            _resident((1, D_MODEL)),
            _resident((D_MODEL, IN_WIDTH)),
            _resident((CONV_K, D_MODEL)),
            _resident((D_MODEL, D_MODEL)),
        ],
        out_specs=pl.BlockSpec((tm, D_MODEL), lambda i: (i, 0)),
        out_shape=jax.ShapeDtypeStruct((t, D_MODEL), F32),
        scratch_shapes=scratch,
        compiler_params=_params(
            sum(_nbytes(b.shape, b.dtype) for b in scratch) + _nbytes(w_in.shape, F32) + _nbytes(w_out.shape, F32),
            2 * _nbytes((tm, D_MODEL), F32), (tm, D_MODEL), 4),
        name="mix",
    )(x, posq, posk, posq, posk, rel_bias_t, sinks, gain, w_in, conv_w, w_out)


def _xattn_kernel(x_ref, xnext_ref, g_ref, wq_ref, mem_ref, mg_ref, wkv_ref, wo_ref, o_ref, h_s, q_s, kv_s,
                  *, tiles_per_seq):
    step = pl.program_id(0)
    cur, nxt = step % 2, (step + 1) % 2

    @pl.when(step % tiles_per_seq == 0)
    def _():
        kv_s[...] = _dot(_rms(mem_ref[...], mg_ref[...]), wkv_ref[...]).astype(BF16)

    def head_cols(hd):
        return slice(hd * MEM_HEAD_DIM, (hd + 1) * MEM_HEAD_DIM)

    def q_proj(h, hd):
        return (_dot(h, wq_ref[:, head_cols(hd)]) * (MEM_HEAD_DIM ** -0.5)).astype(BF16)

    def logits(hd, q):
        return _dot_nt(q, kv_s[:, head_cols(hd)])

    def context(hd, lg):
        e = jnp.exp(lg - jnp.max(lg, axis=1, keepdims=True))
        p = (e / jnp.sum(e, axis=1, keepdims=True)).astype(BF16)
        v = kv_s[:, D_MODEL + hd * MEM_HEAD_DIM:D_MODEL + (hd + 1) * MEM_HEAD_DIM]
        return _dot(p, v).astype(BF16)

    def body(first):
        if first:
            h = _rms(x_ref[...], g_ref[...]).astype(BF16)
            qs = [q_proj(h, 0), q_proj(h, 1)]
        else:
            h = h_s[cur]
            qs = [q_s[cur, 0], q_s[cur, 1]]
        lgs = [logits(0, qs[0])]
        acc = x_ref[...]
        for hd in range(MEM_HEADS):
            if hd + 2 < MEM_HEADS:
                qs.append(q_proj(h, hd + 2))
            if hd + 1 < MEM_HEADS:
                lgs.append(logits(hd + 1, qs[hd + 1]))
            acc = acc + _dot(context(hd, lgs[hd]), wo_ref[head_cols(hd), :])
        h_next = _rms(xnext_ref[...], g_ref[...]).astype(BF16)
        h_s[nxt] = h_next
        q_s[nxt, 0] = q_proj(h_next, 0)
        q_s[nxt, 1] = q_proj(h_next, 1)
        o_ref[...] = acc

    @pl.when(step == 0)
    def _():
        body(True)

    @pl.when(step != 0)
    def _():
        body(False)


def _xattn(x, gain, w_q, mem, mem_gain, w_kv, w_o, *, seq_len):
    t = x.shape[0]
    tm = XATTN_TOKEN_TILE
    tiles_per_seq = seq_len // tm
    scratch = [
        pltpu.VMEM((2, tm, D_MODEL), BF16),
        pltpu.VMEM((2, 2, tm, MEM_HEAD_DIM), BF16),
        pltpu.VMEM((N_MEM, 2 * D_MODEL), BF16),
    ]
    return pl.pallas_call(
        functools.partial(_xattn_kernel, tiles_per_seq=tiles_per_seq),
        grid=(t // tm,),
        in_specs=[
            pl.BlockSpec((tm, D_MODEL), lambda i: (i, 0)),
            pl.BlockSpec((tm, D_MODEL), lambda i: (jnp.minimum(i + 1, t // tm - 1), 0)),
            _resident((1, D_MODEL)),
            _resident((D_MODEL, D_MODEL)),
            pl.BlockSpec((N_MEM, D_MODEL), lambda i: (i // tiles_per_seq, 0)),
            _resident((1, D_MODEL)),
            _resident((D_MODEL, 2 * D_MODEL)),
            _resident((D_MODEL, D_MODEL)),
        ],
        out_specs=pl.BlockSpec((tm, D_MODEL), lambda i: (i, 0)),
        out_shape=jax.ShapeDtypeStruct((t, D_MODEL), F32),
        scratch_shapes=scratch,
        compiler_params=_params(
            sum(_nbytes(b.shape, b.dtype) for b in scratch)
            + _nbytes(w_q.shape, F32) + _nbytes(w_o.shape, F32) + _nbytes(w_kv.shape, F32),
            3 * _nbytes((tm, D_MODEL), F32) + _nbytes((N_MEM, D_MODEL), F32), (tm, D_MODEL), 3),
        name="xattn",
    )(x, x, gain, w_q, mem, mem_gain, w_kv, w_o)


def kernel(x, mem, positions, rel_bias, ffn1_norm, ffn1_w_gu, ffn1_w_down, mix_norm, w_in, sinks, conv_w, w_out, xattn_norm, mem_norm, xattn_wq, xattn_wkv, xattn_wo, ffn2_norm, ffn2_w_gu, ffn2_w_down, final_norm):
    batch, seq_len, _ = x.shape
    depth = w_in.shape[0]
    tokens = batch * seq_len
    nb = seq_len // BLOCK

    xt = x.reshape(tokens, D_MODEL)
    posq = positions.reshape(tokens // TOKEN_TILE, TOKEN_TILE // BLOCK, BLOCK)
    pos_blocks = positions.reshape(batch, nb, BLOCK)
    prev = jnp.concatenate([jnp.full((batch, 1, BLOCK), POS_PAD, positions.dtype), pos_blocks[:, :-1]], axis=1)
    posk = jnp.concatenate([prev, pos_blocks], axis=2).reshape(tokens // TOKEN_TILE, TOKEN_TILE // BLOCK, 2 * BLOCK)
    rel_bias_t = rel_bias.T
    final_gain = final_norm.reshape(1, D_MODEL)

    for l in range(depth):
        xt = _ffn(xt, ffn1_norm[l].reshape(1, D_MODEL), ffn1_w_gu[l], ffn1_w_down[l],
                  final_gain, final_norm=False)
        xt = _mix(xt, posq, posk, rel_bias_t, sinks[l], mix_norm[l].reshape(1, D_MODEL),
                  w_in[l], conv_w[l], w_out[l], seq_len=seq_len)
        xt = _xattn(xt, xattn_norm[l].reshape(1, D_MODEL), xattn_wq[l], mem.reshape(batch * N_MEM, D_MODEL),
                    mem_norm[l].reshape(1, D_MODEL), xattn_wkv[l], xattn_wo[l], seq_len=seq_len)
        xt = _ffn(xt, ffn2_norm[l].reshape(1, D_MODEL), ffn2_w_gu[l], ffn2_w_down[l],
                  final_gain, final_norm=(l == depth - 1))
    return xt.reshape(batch, seq_len, D_MODEL)
```

```python
import functools
import math

import jax
import jax.numpy as jnp
from jax import lax
from jax.experimental import pallas as pl
from jax.experimental.pallas import tpu as pltpu

D_MODEL = 1024
D_FF = 2816
N_MEM = 256
MEM_HEADS = 4
MEM_HEAD_DIM = D_MODEL // MEM_HEADS
HEAD_DIM = 64
SWA_HEADS = 16
SWA_KV_HEADS = 4
WINDOW = 128
BLOCK = 128
REL_BUCKETS = 32
REL_MAX_DIST = 128
CONV_K = 3
EPS = 1e-6
NEG_INF = -1e30
POS_PAD = 1 << 30

Q_WIDTH = SWA_HEADS * HEAD_DIM
KV_WIDTH = SWA_KV_HEADS * HEAD_DIM
OFF_Q = 0
OFF_K = OFF_Q + Q_WIDTH
OFF_V = OFF_K + KV_WIDTH
OFF_C = OFF_V + KV_WIDTH
OFF_B = OFF_C + D_MODEL
OFF_U = OFF_B + D_MODEL
OFF_GA = OFF_U + D_MODEL
OFF_GC = OFF_GA + D_MODEL
IN_WIDTH = OFF_GC + D_MODEL

SUBLANES = 8
MXU_COLS = 256
V7X_VMEM_BYTES = 64 * 1024 * 1024
VMEM_REQUEST_CAP = V7X_VMEM_BYTES - 4 * 1024 * 1024

TOKEN_TILE = 512
XATTN_TOKEN_TILE = 1024
FFN_TOKEN_TILE = 512
FF_CHUNK = MXU_COLS
PAIR = 2 * HEAD_DIM

BF16 = jnp.bfloat16
F32 = jnp.float32


def _rms(x, g):
    return x * lax.rsqrt(jnp.mean(x * x, axis=-1, keepdims=True) + EPS) * g


def _dot(a, b):
    return jnp.dot(a.astype(BF16), b.astype(BF16), preferred_element_type=F32)


def _dot_nt(a, b):
    return lax.dot_general(a.astype(BF16), b.astype(BF16), (((1,), (1,)), ((), ())),
                           preferred_element_type=F32)


def _resident(shape):
    return pl.BlockSpec(shape, lambda i: (0,) * len(shape), pipeline_mode=pl.Buffered(1))


def _nbytes(shape, dtype):
    return math.prod(shape) * jnp.dtype(dtype).itemsize


def _params(resident_bytes, per_step_bytes, tile_shape, tile_temporaries):
    request = resident_bytes + 2 * per_step_bytes + tile_temporaries * _nbytes(tile_shape, F32)
    return pltpu.CompilerParams(dimension_semantics=("arbitrary",),
                                vmem_limit_bytes=min(VMEM_REQUEST_CAP, request))


FF_CHUNKS = D_FF // FF_CHUNK
OUT_CHUNKS = D_MODEL // MXU_COLS


def _ffn_kernel(x_ref, xnext_ref, g_ref, wgu_hbm, wd_hbm, fg_ref, o_ref, wgu_v, wd_v, act_ref, h_s, act0_s, sems,
                *, final_norm):
    def gate_copy(j):
        return pltpu.make_async_copy(wgu_hbm.at[:, pl.ds(j * FF_CHUNK, FF_CHUNK)], wgu_v.at[j], sems.at[j])

    def up_copy(j):
        return pltpu.make_async_copy(wgu_hbm.at[:, pl.ds(D_FF + j * FF_CHUNK, FF_CHUNK)],
                                     wgu_v.at[FF_CHUNKS + j], sems.at[FF_CHUNKS + j])

    def down_copy(c):
        return pltpu.make_async_copy(wd_hbm.at[:, pl.ds(c * MXU_COLS, MXU_COLS)], wd_v.at[c],
                                     sems.at[2 * FF_CHUNKS + c])

    step = pl.program_id(0)
    cur, nxt = step % 2, (step + 1) % 2

    def chunk_act(h, j):
        gate = _dot(h, wgu_v[j])
        up = _dot(h, wgu_v[FF_CHUNKS + j])
        return (gate * jax.nn.sigmoid(gate) * up).astype(BF16)

    def body(first):
        if first:
            h = _rms(x_ref[...], g_ref[...]).astype(BF16)
            gate_copy(0).wait()
            up_copy(0).wait()
            act_ref[:, 0:FF_CHUNK] = chunk_act(h, 0)
        else:
            h = h_s[cur]
            act_ref[:, 0:FF_CHUNK] = act0_s[cur]
        for j in range(1, FF_CHUNKS):
            if first:
                gate_copy(j).wait()
                up_copy(j).wait()
            act_ref[:, j * FF_CHUNK:(j + 1) * FF_CHUNK] = chunk_act(h, j)
        if first:
            for c in range(OUT_CHUNKS):
                down_copy(c).wait()
        y = x_ref[...] + 0.5 * jnp.concatenate([_dot(act_ref[...], wd_v[c]) for c in range(OUT_CHUNKS)], axis=1)
        h_next = _rms(xnext_ref[...], g_ref[...]).astype(BF16)
        h_s[nxt] = h_next
        act0_s[nxt] = chunk_act(h_next, 0)
        if final_norm:
            y = _rms(y, fg_ref[...])
        o_ref[...] = y

    first_step = step == 0

    @pl.when(first_step)
    def _():
        for j in range(FF_CHUNKS):
            gate_copy(j).start()
            up_copy(j).start()
        for c in range(OUT_CHUNKS):
            down_copy(c).start()
        body(True)

    @pl.when(jnp.logical_not(first_step))
    def _():
        body(False)


def _ffn(x, gain, w_gu, w_down, final_gain, *, final_norm):
    t = x.shape[0]
    tm = FFN_TOKEN_TILE
    scratch = [
        pltpu.VMEM((2 * FF_CHUNKS, D_MODEL, FF_CHUNK), F32),
        pltpu.VMEM((OUT_CHUNKS, D_FF, MXU_COLS), F32),
        pltpu.VMEM((tm, D_FF), BF16),
        pltpu.VMEM((2, tm, D_MODEL), BF16),
        pltpu.VMEM((2, tm, FF_CHUNK), BF16),
    ]
    return pl.pallas_call(
        functools.partial(_ffn_kernel, final_norm=final_norm),
        grid=(t // tm,),
        in_specs=[
            pl.BlockSpec((tm, D_MODEL), lambda i: (i, 0)),
            pl.BlockSpec((tm, D_MODEL), lambda i: (jnp.minimum(i + 1, t // tm - 1), 0)),
            _resident((1, D_MODEL)),
            pl.BlockSpec(memory_space=pl.ANY),
            pl.BlockSpec(memory_space=pl.ANY),
            _resident((1, D_MODEL)),
        ],
        out_specs=pl.BlockSpec((tm, D_MODEL), lambda i: (i, 0)),
        out_shape=jax.ShapeDtypeStruct((t, D_MODEL), F32),
        scratch_shapes=scratch + [pltpu.SemaphoreType.DMA((2 * FF_CHUNKS + OUT_CHUNKS,))],
        compiler_params=_params(sum(_nbytes(b.shape, b.dtype) for b in scratch),
                                3 * _nbytes((tm, D_MODEL), F32), (tm, D_MODEL), 2),
        name="ffn_final" if final_norm else "ffn",
    )(x, x, gain, w_gu, w_down, final_gain)


def _bias_lut(tab_ref):
    n = lax.broadcasted_iota(jnp.int32, (SWA_HEADS, WINDOW), 1).astype(F32)
    max_exact = REL_BUCKETS // 2
    nf = jnp.maximum(n, 1.0)
    large = max_exact + jnp.floor(jnp.log(nf / max_exact) / math.log(REL_MAX_DIST / max_exact)
                                  * (REL_BUCKETS - max_exact))
    large = jnp.minimum(large, REL_BUCKETS - 1.0)
    bucket = jnp.where(n < max_exact, n, large)
    lut = jnp.broadcast_to(tab_ref[:, 0:1], (SWA_HEADS, WINDOW))
    for b in range(1, REL_BUCKETS):
        lut = jnp.where(bucket >= b, tab_ref[:, b:b + 1], lut)
    return lut


def _key_head_slabs(t):
    lane = lax.broadcasted_iota(jnp.int32, (t.shape[0], PAIR), 1)
    low_half = lane < HEAD_DIM
    los, his = [], []
    for p in range(SWA_KV_HEADS // 2):
        pair = t[:, p * PAIR:(p + 1) * PAIR]
        swapped = pltpu.roll(pair, HEAD_DIM, axis=1)
        zero = jnp.zeros_like(pair)
        los += [jnp.where(low_half, pair, zero), jnp.where(low_half, swapped, zero)]
        his += [jnp.where(low_half, zero, swapped), jnp.where(low_half, zero, pair)]
    return [a.astype(BF16) for a in los], [a.astype(BF16) for a in his]


def _value_head_slabs_t(t):
    los, his = [], []
    for p in range(SWA_KV_HEADS // 2):
        pair_t = t[:, p * PAIR:(p + 1) * PAIR].T
        zero = jnp.zeros((HEAD_DIM, t.shape[0]), F32)
        for head_t in (pair_t[0:HEAD_DIM], pair_t[HEAD_DIM:PAIR]):
            los.append(jnp.concatenate([head_t, zero], axis=0))
            his.append(jnp.concatenate([zero, head_t], axis=0))
    return [a.astype(BF16) for a in los], [a.astype(BF16) for a in his]


def _mix_kernel(x_ref, posq_ref, posk_ref, posq_next_ref, posk_next_ref, tab_ref, sink_ref, g_ref, win_ref,
                convw_ref, wout_ref, o_ref,
                q_s, klo_s, khi_s, vtlo_s, vthi_s, cu_s, merged_s, relm_s, relm_next_s, bias_s, lut_s, *, tiles_per_seq):
    tm = x_ref.shape[0]
    nblk = tm // BLOCK
    step = pl.program_id(0)
    first = (step % tiles_per_seq) == 0
    next_first = ((step + 1) % tiles_per_seq) == 0

    @pl.when(first)
    def _():
        for s in (klo_s, khi_s):
            s[:, 0:BLOCK, :] = jnp.zeros((SWA_KV_HEADS, BLOCK, PAIR), BF16)
        for s in (vtlo_s, vthi_s):
            s[:, 0] = jnp.zeros((SWA_KV_HEADS, PAIR, BLOCK), BF16)
        cu_s[0:SUBLANES, :] = jnp.zeros((SUBLANES, D_MODEL), F32)

    @pl.when(jnp.logical_not(first))
    def _():
        for s in (klo_s, khi_s):
            s[:, 0:BLOCK, :] = s[:, tm:tm + BLOCK, :]
        for s in (vtlo_s, vthi_s):
            s[:, 0] = s[:, nblk]
        cu_s[0:SUBLANES, :] = cu_s[tm:tm + SUBLANES, :]

    def masked_rel(pq_ref, pk_ref, j):
        pos_k = jnp.concatenate(
            [jnp.broadcast_to(pk_ref[0, j:j + 1, c * BLOCK:(c + 1) * BLOCK], (BLOCK, BLOCK)).T
             for c in range(2)], axis=0)
        rel = pq_ref[0, j:j + 1, :] - pos_k
        return jnp.where((rel >= 0) & (rel < WINDOW), rel, -1)

    def block_slots(sequence_start):
        return [jnp.where(sequence_start, nblk, 0)] + list(range(1, nblk))

    def rebuild(slot, relm):
        relm_s[slot] = relm
        shown = relm >= 0
        idx = jnp.maximum(relm, 0)
        for hd in range(SWA_HEADS):
            lut = jnp.broadcast_to(lut_s[hd:hd + 1, :], (2 * BLOCK, WINDOW))
            bias_s[slot, hd] = jnp.where(shown, jnp.take_along_axis(lut, idx, axis=1), NEG_INF)

    slots = block_slots(first)
    next_slots = block_slots(next_first)

    @pl.when(step == 0)
    def _():
        relm_s[...] = jnp.full(relm_s.shape, -2, jnp.int32)
        lut_s[...] = _bias_lut(tab_ref)
        for j in range(nblk):
            rebuild(slots[j], masked_rel(posq_ref, posk_ref, j))

    next_misses = []
    for j in range(nblk):
        relm = masked_rel(posq_next_ref, posk_next_ref, j)
        relm_next_s[j] = relm
        next_misses.append(jnp.sum((relm != relm_s[next_slots[j]]).astype(jnp.int32)))

    h = _rms(x_ref[...], g_ref[...]).astype(BF16)

    k_lo, k_hi = _key_head_slabs(_dot(h, win_ref[:, OFF_K:OFF_K + KV_WIDTH]))
    vt_lo, vt_hi = _value_head_slabs_t(_dot(h, win_ref[:, OFF_V:OFF_V + KV_WIDTH]))
    for g in range(SWA_KV_HEADS):
        klo_s[g, BLOCK:BLOCK + tm, :] = k_lo[g]
        khi_s[g, BLOCK:BLOCK + tm, :] = k_hi[g]
        for j in range(nblk):
            vtlo_s[g, j + 1] = vt_lo[g][:, j * BLOCK:(j + 1) * BLOCK]
            vthi_s[g, j + 1] = vt_hi[g][:, j * BLOCK:(j + 1) * BLOCK]
    q_s[...] = (_dot(h, win_ref[:, OFF_Q:OFF_Q + Q_WIDTH]) * (HEAD_DIM ** -0.5)).astype(BF16)

    def qk(blk, g):
        rows = slice(blk * BLOCK, (blk + 1) * BLOCK)
        keys = slice(blk * BLOCK, (blk + 2) * BLOCK)
        c0 = g * 2 * PAIR
        q2 = jnp.concatenate([q_s[rows, c0:c0 + PAIR], q_s[rows, c0 + PAIR:c0 + 2 * PAIR]], axis=0)
        return _dot_nt(klo_s[g, keys, :], q2), _dot_nt(khi_s[g, keys, :], q2)

    def softmax_pv(blk, g, logits_t):
        probs_t = [[None, None], [None, None]]
        inv = [[None, None], [None, None]]
        for half in range(2):
            for pair in range(2):
                head = g * 4 + pair * 2 + half
                lt = logits_t[half][:, pair * BLOCK:(pair + 1) * BLOCK] + bias_s[slots[blk], head]
                sink = sink_ref[head]
                m = jnp.maximum(jnp.max(lt, axis=0, keepdims=True), sink)
                e = jnp.exp(lt - m)
                inv[half][pair] = 1.0 / (jnp.sum(e, axis=0, keepdims=True) + jnp.exp(sink - m))
                probs_t[half][pair] = e.astype(BF16)
        v_lo_t = jnp.concatenate([vtlo_s[g, blk], vtlo_s[g, blk + 1]], axis=1)
        v_hi_t = jnp.concatenate([vthi_s[g, blk], vthi_s[g, blk + 1]], axis=1)
        out_t = (_dot(v_lo_t, jnp.concatenate(probs_t[0], axis=1))
                 + _dot(v_hi_t, jnp.concatenate(probs_t[1], axis=1)))
        scale = jnp.concatenate(
            [jnp.broadcast_to(jnp.concatenate(inv[0], axis=1), (HEAD_DIM, 2 * BLOCK)),
             jnp.broadcast_to(jnp.concatenate(inv[1], axis=1), (HEAD_DIM, 2 * BLOCK))], axis=0)
        return (out_t * scale).T

    units = [(blk, g) for g in range(SWA_KV_HEADS) for blk in range(nblk)]
    logits_next = qk(*units[0])
    for g in range(SWA_KV_HEADS):
        cols = slice(g * 2 * PAIR, (g + 1) * 2 * PAIR)

        def proj(off):
            return _dot(h, win_ref[:, off + g * 2 * PAIR:off + (g + 1) * 2 * PAIR])

        def conv_taps(_):
            cu = proj(OFF_C) * proj(OFF_U)
            cu_s[SUBLANES:SUBLANES + tm, cols] = cu
            return (convw_ref[0:1, cols] * cu_s[SUBLANES - 2:SUBLANES - 2 + tm, cols]
                    + convw_ref[1:2, cols] * cu_s[SUBLANES - 1:SUBLANES - 1 + tm, cols]
                    + convw_ref[2:3, cols] * cu)

        stages = [conv_taps,
                  lambda conv: proj(OFF_B) * conv,
                  lambda conv: jax.nn.sigmoid(proj(OFF_GC)) * conv]
        attn_blocks = []
        conv = None
        for blk in range(nblk):
            logits_t = logits_next
            nxt = g * nblk + blk + 1
            if nxt < len(units):
                logits_next = qk(*units[nxt])
            if blk < len(stages):
                conv = stages[blk](conv)
            out = softmax_pv(blk, g, logits_t)
            attn_blocks.append(jnp.concatenate([out[0:BLOCK], out[BLOCK:2 * BLOCK]], axis=1))
        for stage in stages[nblk:]:
            conv = stage(conv)
        attn = jnp.concatenate(attn_blocks, axis=0)
        merged_s[:, cols] = (jax.nn.sigmoid(proj(OFF_GA)) * attn + conv).astype(BF16)

    o_ref[...] = x_ref[...] + _dot(merged_s[...], wout_ref[...])

    for j in range(nblk):
        @pl.when(next_misses[j] != 0)
        def _(j=j):
            rebuild(next_slots[j], relm_next_s[j])


def _mix(x, posq, posk, rel_bias_t, sinks, gain, w_in, conv_w, w_out, *, seq_len):
    t = x.shape[0]
    tm = TOKEN_TILE
    nblk = tm // BLOCK
    k_scratch = pltpu.VMEM((SWA_KV_HEADS, BLOCK + tm, PAIR), BF16)
    vt_scratch = pltpu.VMEM((SWA_KV_HEADS, nblk + 1, PAIR, BLOCK), BF16)
    scratch = [
        pltpu.VMEM((tm, Q_WIDTH), BF16),
        k_scratch, k_scratch, vt_scratch, vt_scratch,
        pltpu.VMEM((SUBLANES + tm, D_MODEL), F32),
        pltpu.VMEM((tm, D_MODEL), BF16),
        pltpu.VMEM((nblk + 1, 2 * BLOCK, BLOCK), jnp.int32),
        pltpu.VMEM((nblk, 2 * BLOCK, BLOCK), jnp.int32),
        pltpu.VMEM((nblk + 1, SWA_HEADS, 2 * BLOCK, BLOCK), F32),
        pltpu.VMEM((SWA_HEADS, WINDOW), F32),
    ]
    return pl.pallas_call(
        functools.partial(_mix_kernel, tiles_per_seq=seq_len // tm),
        grid=(t // tm,),
        in_specs=[
            pl.BlockSpec((tm, D_MODEL), lambda i: (i, 0)),
            pl.BlockSpec((1, nblk, BLOCK), lambda i: (i, 0, 0)),
            pl.BlockSpec((1, nblk, 2 * BLOCK), lambda i: (i, 0, 0)),
            pl.BlockSpec((1, nblk, BLOCK), lambda i: (jnp.minimum(i + 1, t // tm - 1), 0, 0)),
            pl.BlockSpec((1, nblk, 2 * BLOCK), lambda i: (jnp.minimum(i + 1, t // tm - 1), 0, 0)),
            _resident((SWA_HEADS, REL_BUCKETS)),
            pl.BlockSpec(memory_space=pltpu.SMEM),
            _resident((1, D_MODEL)),
            _resident((D_MODEL, IN_WIDTH)),
            _resident((CONV_K, D_MODEL)),
            _resident((D_MODEL, D_MODEL)),
        ],
        out_specs=pl.BlockSpec((tm, D_MODEL), lambda i: (i, 0)),
        out_shape=jax.ShapeDtypeStruct((t, D_MODEL), F32),
        scratch_shapes=scratch,
        compiler_params=_params(
            sum(_nbytes(b.shape, b.dtype) for b in scratch) + _nbytes(w_in.shape, F32) + _nbytes(w_out.shape, F32),
            2 * _nbytes((tm, D_MODEL), F32), (tm, D_MODEL), 4),
        name="mix",
    )(x, posq, posk, posq, posk, rel_bias_t, sinks, gain, w_in, conv_w, w_out)


def _head_cols(hd):
    return slice(hd * MEM_HEAD_DIM, (hd + 1) * MEM_HEAD_DIM)


def _mem_proj_kernel(mem_ref, mg_ref, wkv_ref, wq_ref, wo_ref, qk_ref, vo_ref):
    kv = _dot(_rms(mem_ref[...], mg_ref[...]), wkv_ref[...])
    for hd in range(MEM_HEADS):
        cols = _head_cols(hd)
        k_h = kv[:, cols]
        v_h = kv[:, D_MODEL + hd * MEM_HEAD_DIM:D_MODEL + (hd + 1) * MEM_HEAD_DIM]
        qk_ref[0, :, cols] = (_dot_nt(wq_ref[:, cols], k_h) * (MEM_HEAD_DIM ** -0.5)).astype(BF16)
        vo_ref[0, cols, :] = _dot(v_h, wo_ref[cols, :]).astype(BF16)


def _mem_proj(mem, gain, w_kv, w_q, w_o, *, batch):
    folded = jax.ShapeDtypeStruct((batch, D_MODEL, D_MODEL), BF16)
    return pl.pallas_call(
        _mem_proj_kernel,
        grid=(batch,),
        in_specs=[
            pl.BlockSpec((N_MEM, D_MODEL), lambda i: (i, 0)),
            _resident((1, D_MODEL)),
            _resident((D_MODEL, 2 * D_MODEL)),
            _resident((D_MODEL, D_MODEL)),
            _resident((D_MODEL, D_MODEL)),
        ],
        out_specs=[pl.BlockSpec((1, D_MODEL, D_MODEL), lambda i: (i, 0, 0)),
                   pl.BlockSpec((1, D_MODEL, D_MODEL), lambda i: (i, 0, 0))],
        out_shape=[folded, folded],
        compiler_params=_params(_nbytes(w_kv.shape, F32) + _nbytes(w_q.shape, F32) + _nbytes(w_o.shape, F32),
                                _nbytes((N_MEM, D_MODEL), F32) + 2 * _nbytes((D_MODEL, D_MODEL), BF16),
                                (N_MEM, 2 * D_MODEL), 4),
        name="mem_proj",
    )(mem, gain, w_kv, w_q, w_o)


XATTN_ROW_SPLITS = 2


def _xattn_kernel(x_ref, xnext_ref, g_ref, qk_ref, vo_ref, o_ref, h_s):
    step = pl.program_id(0)
    cur, nxt = step % 2, (step + 1) % 2
    rows_per_split = x_ref.shape[0] // XATTN_ROW_SPLITS

    def probs(lg):
        out = []
        for hd in range(MEM_HEADS):
            seg = lg[:, hd * N_MEM:(hd + 1) * N_MEM]
            e = jnp.exp(seg - jnp.max(seg, axis=1, keepdims=True))
            out.append((e / jnp.sum(e, axis=1, keepdims=True)).astype(BF16))
        return jnp.concatenate(out, axis=1)

    def body(first):
        h = _rms(x_ref[...], g_ref[...]).astype(BF16) if first else h_s[cur]
        splits = [slice(r * rows_per_split, (r + 1) * rows_per_split) for r in range(XATTN_ROW_SPLITS)]
        logits = [_dot(h[rows], qk_ref[0]) for rows in splits]
        for rows, lg in zip(splits, logits):
            o_ref[rows, :] = x_ref[rows, :] + _dot(probs(lg), vo_ref[0])
        h_s[nxt] = _rms(xnext_ref[...], g_ref[...]).astype(BF16)

    @pl.when(step == 0)
    def _():
        body(True)

    @pl.when(step != 0)
    def _():
        body(False)


def _xattn(x, gain, qk, vo, *, seq_len):
    t = x.shape[0]
    tm = XATTN_TOKEN_TILE
    tiles_per_seq = seq_len // tm
    scratch = [pltpu.VMEM((2, tm, D_MODEL), BF16)]
    return pl.pallas_call(
        _xattn_kernel,
        grid=(t // tm,),
        in_specs=[
            pl.BlockSpec((tm, D_MODEL), lambda i: (i, 0)),
            pl.BlockSpec((tm, D_MODEL), lambda i: (jnp.minimum(i + 1, t // tm - 1), 0)),
            _resident((1, D_MODEL)),
            pl.BlockSpec((1, D_MODEL, D_MODEL), lambda i: (i // tiles_per_seq, 0, 0)),
            pl.BlockSpec((1, D_MODEL, D_MODEL), lambda i: (i // tiles_per_seq, 0, 0)),
        ],
        out_specs=pl.BlockSpec((tm, D_MODEL), lambda i: (i, 0)),
        out_shape=jax.ShapeDtypeStruct((t, D_MODEL), F32),
        scratch_shapes=scratch,
        compiler_params=_params(
            sum(_nbytes(b.shape, b.dtype) for b in scratch),
            3 * _nbytes((tm, D_MODEL), F32) + 2 * _nbytes((D_MODEL, D_MODEL), BF16), (tm, D_MODEL), 3),
        name="xattn",
    )(x, x, gain, qk, vo)


def kernel(x, mem, positions, rel_bias, ffn1_norm, ffn1_w_gu, ffn1_w_down, mix_norm, w_in, sinks, conv_w, w_out, xattn_norm, mem_norm, xattn_wq, xattn_wkv, xattn_wo, ffn2_norm, ffn2_w_gu, ffn2_w_down, final_norm):
    batch, seq_len, _ = x.shape
    depth = w_in.shape[0]
    tokens = batch * seq_len
    nb = seq_len // BLOCK

    xt = x.reshape(tokens, D_MODEL)
    posq = positions.reshape(tokens // TOKEN_TILE, TOKEN_TILE // BLOCK, BLOCK)
    pos_blocks = positions.reshape(batch, nb, BLOCK)
    prev = jnp.concatenate([jnp.full((batch, 1, BLOCK), POS_PAD, positions.dtype), pos_blocks[:, :-1]], axis=1)
    posk = jnp.concatenate([prev, pos_blocks], axis=2).reshape(tokens // TOKEN_TILE, TOKEN_TILE // BLOCK, 2 * BLOCK)
    rel_bias_t = rel_bias.T
    final_gain = final_norm.reshape(1, D_MODEL)

    for l in range(depth):
        xt = _ffn(xt, ffn1_norm[l].reshape(1, D_MODEL), ffn1_w_gu[l], ffn1_w_down[l],
                  final_gain, final_norm=False)
        qk, vo = _mem_proj(mem.reshape(batch * N_MEM, D_MODEL), mem_norm[l].reshape(1, D_MODEL),
                           xattn_wkv[l], xattn_wq[l], xattn_wo[l], batch=batch)
        xt = _mix(xt, posq, posk, rel_bias_t, sinks[l], mix_norm[l].reshape(1, D_MODEL),
                  w_in[l], conv_w[l], w_out[l], seq_len=seq_len)
        xt = _xattn(xt, xattn_norm[l].reshape(1, D_MODEL), qk, vo, seq_len=seq_len)
        xt = _ffn(xt, ffn2_norm[l].reshape(1, D_MODEL), ffn2_w_gu[l], ffn2_w_down[l],
                  final_gain, final_norm=(l == depth - 1))
    return xt.reshape(batch, seq_len, D_MODEL)
```

```python
import functools
import math

import jax
import jax.numpy as jnp
from jax import lax
from jax.experimental import pallas as pl
from jax.experimental.pallas import tpu as pltpu

D_MODEL = 1024
D_FF = 2816
N_MEM = 256
MEM_HEADS = 4
MEM_HEAD_DIM = D_MODEL // MEM_HEADS
HEAD_DIM = 64
SWA_HEADS = 16
SWA_KV_HEADS = 4
WINDOW = 128
BLOCK = 128
REL_BUCKETS = 32
REL_MAX_DIST = 128
CONV_K = 3
EPS = 1e-6
NEG_INF = -1e30
POS_PAD = 1 << 30

Q_WIDTH = SWA_HEADS * HEAD_DIM
KV_WIDTH = SWA_KV_HEADS * HEAD_DIM
OFF_Q = 0
OFF_K = OFF_Q + Q_WIDTH
OFF_V = OFF_K + KV_WIDTH
OFF_C = OFF_V + KV_WIDTH
OFF_B = OFF_C + D_MODEL
OFF_U = OFF_B + D_MODEL
OFF_GA = OFF_U + D_MODEL
OFF_GC = OFF_GA + D_MODEL
IN_WIDTH = OFF_GC + D_MODEL

SUBLANES = 8
MXU_COLS = 256
V7X_VMEM_BYTES = 64 * 1024 * 1024
VMEM_REQUEST_CAP = V7X_VMEM_BYTES - 4 * 1024 * 1024

TOKEN_TILE = 512
XATTN_TOKEN_TILE = 1024
FFN_TOKEN_TILE = 512
FF_CHUNK = MXU_COLS
PAIR = 2 * HEAD_DIM

BF16 = jnp.bfloat16
F32 = jnp.float32


def _rms(x, g):
    return x * lax.rsqrt(jnp.mean(x * x, axis=-1, keepdims=True) + EPS) * g


def _dot(a, b):
    return jnp.dot(a.astype(BF16), b.astype(BF16), preferred_element_type=F32)


def _dot_nt(a, b):
    return lax.dot_general(a.astype(BF16), b.astype(BF16), (((1,), (1,)), ((), ())),
                           preferred_element_type=F32)


def _resident(shape):
    return pl.BlockSpec(shape, lambda i: (0,) * len(shape), pipeline_mode=pl.Buffered(1))


def _nbytes(shape, dtype):
    return math.prod(shape) * jnp.dtype(dtype).itemsize


def _params(resident_bytes, per_step_bytes, tile_shape, tile_temporaries):
    request = resident_bytes + 2 * per_step_bytes + tile_temporaries * _nbytes(tile_shape, F32)
    return pltpu.CompilerParams(dimension_semantics=("arbitrary",),
                                vmem_limit_bytes=min(VMEM_REQUEST_CAP, request))


FF_CHUNKS = D_FF // FF_CHUNK
OUT_CHUNKS = D_MODEL // MXU_COLS


def _ffn_kernel(x_ref, xnext_ref, g_ref, wgu_hbm, wd_hbm, fg_ref, o_ref, wgu_v, wd_v, act_ref, h_s, act0_s, sems,
                *, final_norm):
    def gate_copy(j):
        return pltpu.make_async_copy(wgu_hbm.at[:, pl.ds(j * FF_CHUNK, FF_CHUNK)], wgu_v.at[j], sems.at[j])

    def up_copy(j):
        return pltpu.make_async_copy(wgu_hbm.at[:, pl.ds(D_FF + j * FF_CHUNK, FF_CHUNK)],
                                     wgu_v.at[FF_CHUNKS + j], sems.at[FF_CHUNKS + j])

    def down_copy(c):
        return pltpu.make_async_copy(wd_hbm.at[:, pl.ds(c * MXU_COLS, MXU_COLS)], wd_v.at[c],
                                     sems.at[2 * FF_CHUNKS + c])

    step = pl.program_id(0)
    cur, nxt = step % 2, (step + 1) % 2

    def chunk_act(h, j):
        gate = _dot(h, wgu_v[j])
        up = _dot(h, wgu_v[FF_CHUNKS + j])
        return (gate * jax.nn.sigmoid(gate) * up).astype(BF16)

    def body(first):
        if first:
            h = _rms(x_ref[...], g_ref[...]).astype(BF16)
            gate_copy(0).wait()
            up_copy(0).wait()
            act_ref[:, 0:FF_CHUNK] = chunk_act(h, 0)
        else:
            h = h_s[cur]
            act_ref[:, 0:FF_CHUNK] = act0_s[cur]
        for j in range(1, FF_CHUNKS):
            if first:
                gate_copy(j).wait()
                up_copy(j).wait()
            act_ref[:, j * FF_CHUNK:(j + 1) * FF_CHUNK] = chunk_act(h, j)
        if first:
            for c in range(OUT_CHUNKS):
                down_copy(c).wait()
        y = x_ref[...] + 0.5 * jnp.concatenate([_dot(act_ref[...], wd_v[c]) for c in range(OUT_CHUNKS)], axis=1)
        h_next = _rms(xnext_ref[...], g_ref[...]).astype(BF16)
        h_s[nxt] = h_next
        act0_s[nxt] = chunk_act(h_next, 0)
        if final_norm:
            y = _rms(y, fg_ref[...])
        o_ref[...] = y

    first_step = step == 0

    @pl.when(first_step)
    def _():
        for j in range(FF_CHUNKS):
            gate_copy(j).start()
            up_copy(j).start()
        for c in range(OUT_CHUNKS):
            down_copy(c).start()
        body(True)

    @pl.when(jnp.logical_not(first_step))
    def _():
        body(False)


def _ffn(x, gain, w_gu, w_down, final_gain, *, final_norm):
    t = x.shape[0]
    tm = FFN_TOKEN_TILE
    scratch = [
        pltpu.VMEM((2 * FF_CHUNKS, D_MODEL, FF_CHUNK), F32),
        pltpu.VMEM((OUT_CHUNKS, D_FF, MXU_COLS), F32),
        pltpu.VMEM((tm, D_FF), BF16),
        pltpu.VMEM((2, tm, D_MODEL), BF16),
        pltpu.VMEM((2, tm, FF_CHUNK), BF16),
    ]
    return pl.pallas_call(
        functools.partial(_ffn_kernel, final_norm=final_norm),
        grid=(t // tm,),
        in_specs=[
            pl.BlockSpec((tm, D_MODEL), lambda i: (i, 0)),
            pl.BlockSpec((tm, D_MODEL), lambda i: (jnp.minimum(i + 1, t // tm - 1), 0)),
            _resident((1, D_MODEL)),
            pl.BlockSpec(memory_space=pl.ANY),
            pl.BlockSpec(memory_space=pl.ANY),
            _resident((1, D_MODEL)),
        ],
        out_specs=pl.BlockSpec((tm, D_MODEL), lambda i: (i, 0)),
        out_shape=jax.ShapeDtypeStruct((t, D_MODEL), F32),
        scratch_shapes=scratch + [pltpu.SemaphoreType.DMA((2 * FF_CHUNKS + OUT_CHUNKS,))],
        compiler_params=_params(sum(_nbytes(b.shape, b.dtype) for b in scratch),
                                3 * _nbytes((tm, D_MODEL), F32), (tm, D_MODEL), 2),
        name="ffn_final" if final_norm else "ffn",
    )(x, x, gain, w_gu, w_down, final_gain)


def _bias_lut(tab_ref):
    n = lax.broadcasted_iota(jnp.int32, (SWA_HEADS, WINDOW), 1).astype(F32)
    max_exact = REL_BUCKETS // 2
    nf = jnp.maximum(n, 1.0)
    large = max_exact + jnp.floor(jnp.log(nf / max_exact) / math.log(REL_MAX_DIST / max_exact)
                                  * (REL_BUCKETS - max_exact))
    large = jnp.minimum(large, REL_BUCKETS - 1.0)
    bucket = jnp.where(n < max_exact, n, large)
    lut = jnp.broadcast_to(tab_ref[:, 0:1], (SWA_HEADS, WINDOW))
    for b in range(1, REL_BUCKETS):
        lut = jnp.where(bucket >= b, tab_ref[:, b:b + 1], lut)
    return lut


def _key_head_slabs(t):
    lane = lax.broadcasted_iota(jnp.int32, (t.shape[0], PAIR), 1)
    low_half = lane < HEAD_DIM
    los, his = [], []
    for p in range(SWA_KV_HEADS // 2):
        pair = t[:, p * PAIR:(p + 1) * PAIR]
        swapped = pltpu.roll(pair, HEAD_DIM, axis=1)
        zero = jnp.zeros_like(pair)
        los += [jnp.where(low_half, pair, zero), jnp.where(low_half, swapped, zero)]
        his += [jnp.where(low_half, zero, swapped), jnp.where(low_half, zero, pair)]
    return [a.astype(BF16) for a in los], [a.astype(BF16) for a in his]


def _value_head_slabs_t(t):
    los, his = [], []
    for p in range(SWA_KV_HEADS // 2):
        pair_t = t[:, p * PAIR:(p + 1) * PAIR].T
        zero = jnp.zeros((HEAD_DIM, t.shape[0]), F32)
        for head_t in (pair_t[0:HEAD_DIM], pair_t[HEAD_DIM:PAIR]):
            los.append(jnp.concatenate([head_t, zero], axis=0))
            his.append(jnp.concatenate([zero, head_t], axis=0))
    return [a.astype(BF16) for a in los], [a.astype(BF16) for a in his]


def _mix_kernel(x_ref, posq_ref, posk_ref, posq_next_ref, posk_next_ref, tab_ref, sink_ref, g_ref, win_ref,
                convw_ref, wout_ref, o_ref,
                q_s, klo_s, khi_s, vtlo_s, vthi_s, cu_s, merged_s, relm_s, relm_next_s, bias_s, lut_s, *, tiles_per_seq):
    tm = x_ref.shape[0]
    nblk = tm // BLOCK
    step = pl.program_id(0)
    first = (step % tiles_per_seq) == 0
    next_first = ((step + 1) % tiles_per_seq) == 0

    @pl.when(first)
    def _():
        for s in (klo_s, khi_s):
            s[:, 0:BLOCK, :] = jnp.zeros((SWA_KV_HEADS, BLOCK, PAIR), BF16)
        for s in (vtlo_s, vthi_s):
            s[:, 0] = jnp.zeros((SWA_KV_HEADS, PAIR, BLOCK), BF16)
        cu_s[0:SUBLANES, :] = jnp.zeros((SUBLANES, D_MODEL), F32)

    @pl.when(jnp.logical_not(first))
    def _():
        for s in (klo_s, khi_s):
            s[:, 0:BLOCK, :] = s[:, tm:tm + BLOCK, :]
        for s in (vtlo_s, vthi_s):
            s[:, 0] = s[:, nblk]
        cu_s[0:SUBLANES, :] = cu_s[tm:tm + SUBLANES, :]

    def masked_rel(pq_ref, pk_ref, j):
        pos_k = jnp.concatenate(
            [jnp.broadcast_to(pk_ref[0, j:j + 1, c * BLOCK:(c + 1) * BLOCK], (BLOCK, BLOCK)).T
             for c in range(2)], axis=0)
        rel = pq_ref[0, j:j + 1, :] - pos_k
        return jnp.where((rel >= 0) & (rel < WINDOW), rel, -1)

    def block_slots(sequence_start):
        return [jnp.where(sequence_start, nblk, 0)] + list(range(1, nblk))

    def rebuild(slot, relm):
        relm_s[slot] = relm
        shown = relm >= 0
        idx = jnp.maximum(relm, 0)
        for hd in range(SWA_HEADS):
            lut = jnp.broadcast_to(lut_s[hd:hd + 1, :], (2 * BLOCK, WINDOW))
            bias_s[slot, hd] = jnp.where(shown, jnp.take_along_axis(lut, idx, axis=1), NEG_INF)

    slots = block_slots(first)
    next_slots = block_slots(next_first)

    @pl.when(step == 0)
    def _():
        relm_s[...] = jnp.full(relm_s.shape, -2, jnp.int32)
        lut_s[...] = _bias_lut(tab_ref)
        for j in range(nblk):
            rebuild(slots[j], masked_rel(posq_ref, posk_ref, j))

    next_misses = []
    for j in range(nblk):
        relm = masked_rel(posq_next_ref, posk_next_ref, j)
        relm_next_s[j] = relm
        next_misses.append(jnp.sum((relm != relm_s[next_slots[j]]).astype(jnp.int32)))

    h = _rms(x_ref[...], g_ref[...]).astype(BF16)

    k_lo, k_hi = _key_head_slabs(_dot(h, win_ref[:, OFF_K:OFF_K + KV_WIDTH]))
    vt_lo, vt_hi = _value_head_slabs_t(_dot(h, win_ref[:, OFF_V:OFF_V + KV_WIDTH]))
    for g in range(SWA_KV_HEADS):
        klo_s[g, BLOCK:BLOCK + tm, :] = k_lo[g]
        khi_s[g, BLOCK:BLOCK + tm, :] = k_hi[g]
        for j in range(nblk):
            vtlo_s[g, j + 1] = vt_lo[g][:, j * BLOCK:(j + 1) * BLOCK]
            vthi_s[g, j + 1] = vt_hi[g][:, j * BLOCK:(j + 1) * BLOCK]
    q_s[...] = (_dot(h, win_ref[:, OFF_Q:OFF_Q + Q_WIDTH]) * (HEAD_DIM ** -0.5)).astype(BF16)

    def qk(blk, g):
        rows = slice(blk * BLOCK, (blk + 1) * BLOCK)
        keys = slice(blk * BLOCK, (blk + 2) * BLOCK)
        c0 = g * 2 * PAIR
        q2 = jnp.concatenate([q_s[rows, c0:c0 + PAIR], q_s[rows, c0 + PAIR:c0 + 2 * PAIR]], axis=0)
        return _dot_nt(klo_s[g, keys, :], q2), _dot_nt(khi_s[g, keys, :], q2)

    def softmax_pv(blk, g, logits_t):
        probs_t = [[None, None], [None, None]]
        inv = [[None, None], [None, None]]
        for half in range(2):
            for pair in range(2):
                head = g * 4 + pair * 2 + half
                lt = logits_t[half][:, pair * BLOCK:(pair + 1) * BLOCK] + bias_s[slots[blk], head]
                sink = sink_ref[head]
                m = jnp.maximum(jnp.max(lt, axis=0, keepdims=True), sink)
                e = jnp.exp(lt - m)
                inv[half][pair] = 1.0 / (jnp.sum(e, axis=0, keepdims=True) + jnp.exp(sink - m))
                probs_t[half][pair] = e.astype(BF16)
        v_lo_t = jnp.concatenate([vtlo_s[g, blk], vtlo_s[g, blk + 1]], axis=1)
        v_hi_t = jnp.concatenate([vthi_s[g, blk], vthi_s[g, blk + 1]], axis=1)
        out_t = (_dot(v_lo_t, jnp.concatenate(probs_t[0], axis=1))
                 + _dot(v_hi_t, jnp.concatenate(probs_t[1], axis=1)))
        scale = jnp.concatenate(
            [jnp.broadcast_to(jnp.concatenate(inv[0], axis=1), (HEAD_DIM, 2 * BLOCK)),
             jnp.broadcast_to(jnp.concatenate(inv[1], axis=1), (HEAD_DIM, 2 * BLOCK))], axis=0)
        return (out_t * scale).T

    units = [(blk, g) for g in range(SWA_KV_HEADS) for blk in range(nblk)]
    logits_next = qk(*units[0])
    for g in range(SWA_KV_HEADS):
        cols = slice(g * 2 * PAIR, (g + 1) * 2 * PAIR)

        def proj(off):
            return _dot(h, win_ref[:, off + g * 2 * PAIR:off + (g + 1) * 2 * PAIR])

        def conv_taps(_):
            cu = proj(OFF_C) * proj(OFF_U)
            cu_s[SUBLANES:SUBLANES + tm, cols] = cu
            return (convw_ref[0:1, cols] * cu_s[SUBLANES - 2:SUBLANES - 2 + tm, cols]
                    + convw_ref[1:2, cols] * cu_s[SUBLANES - 1:SUBLANES - 1 + tm, cols]
                    + convw_ref[2:3, cols] * cu)

        stages = [conv_taps,
                  lambda conv: proj(OFF_B) * conv,
                  lambda conv: jax.nn.sigmoid(proj(OFF_GC)) * conv]
        attn_blocks = []
        conv = None
        for blk in range(nblk):
            logits_t = logits_next
            nxt = g * nblk + blk + 1
            if nxt < len(units):
                logits_next = qk(*units[nxt])
            if blk < len(stages):
                conv = stages[blk](conv)
            out = softmax_pv(blk, g, logits_t)
            attn_blocks.append(jnp.concatenate([out[0:BLOCK], out[BLOCK:2 * BLOCK]], axis=1))
        for stage in stages[nblk:]:
            conv = stage(conv)
        attn = jnp.concatenate(attn_blocks, axis=0)
        merged_s[:, cols] = (jax.nn.sigmoid(proj(OFF_GA)) * attn + conv).astype(BF16)

    o_ref[...] = x_ref[...] + _dot(merged_s[...], wout_ref[...])

    for j in range(nblk):
        @pl.when(next_misses[j] != 0)
        def _(j=j):
            rebuild(next_slots[j], relm_next_s[j])


def _mix(x, posq, posk, rel_bias_t, sinks, gain, w_in, conv_w, w_out, *, seq_len):
    t = x.shape[0]
    tm = TOKEN_TILE
    nblk = tm // BLOCK
    k_scratch = pltpu.VMEM((SWA_KV_HEADS, BLOCK + tm, PAIR), BF16)
    vt_scratch = pltpu.VMEM((SWA_KV_HEADS, nblk + 1, PAIR, BLOCK), BF16)
    scratch = [
        pltpu.VMEM((tm, Q_WIDTH), BF16),
        k_scratch, k_scratch, vt_scratch, vt_scratch,
        pltpu.VMEM((SUBLANES + tm, D_MODEL), F32),
        pltpu.VMEM((tm, D_MODEL), BF16),
        pltpu.VMEM((nblk + 1, 2 * BLOCK, BLOCK), jnp.int32),
        pltpu.VMEM((nblk, 2 * BLOCK, BLOCK), jnp.int32),
        pltpu.VMEM((nblk + 1, SWA_HEADS, 2 * BLOCK, BLOCK), F32),
        pltpu.VMEM((SWA_HEADS, WINDOW), F32),
    ]
    return pl.pallas_call(
        functools.partial(_mix_kernel, tiles_per_seq=seq_len // tm),
        grid=(t // tm,),
        in_specs=[
            pl.BlockSpec((tm, D_MODEL), lambda i: (i, 0)),
            pl.BlockSpec((1, nblk, BLOCK), lambda i: (i, 0, 0)),
            pl.BlockSpec((1, nblk, 2 * BLOCK), lambda i: (i, 0, 0)),
            pl.BlockSpec((1, nblk, BLOCK), lambda i: (jnp.minimum(i + 1, t // tm - 1), 0, 0)),
            pl.BlockSpec((1, nblk, 2 * BLOCK), lambda i: (jnp.minimum(i + 1, t // tm - 1), 0, 0)),
            _resident((SWA_HEADS, REL_BUCKETS)),
            pl.BlockSpec(memory_space=pltpu.SMEM),
            _resident((1, D_MODEL)),
            _resident((D_MODEL, IN_WIDTH)),
            _resident((CONV_K, D_MODEL)),
            _resident((D_MODEL, D_MODEL)),
        ],
        out_specs=pl.BlockSpec((tm, D_MODEL), lambda i: (i, 0)),
        out_shape=jax.ShapeDtypeStruct((t, D_MODEL), F32),
        scratch_shapes=scratch,
        compiler_params=_params(
            sum(_nbytes(b.shape, b.dtype) for b in scratch) + _nbytes(w_in.shape, F32) + _nbytes(w_out.shape, F32),
            2 * _nbytes((tm, D_MODEL), F32), (tm, D_MODEL), 4),
        name="mix",
    )(x, posq, posk, posq, posk, rel_bias_t, sinks, gain, w_in, conv_w, w_out)


def _mem_proj_kernel(mem_ref, mg_ref, wk_ref, wv_ref, wq_ref, wo_ref, qk_ref, vo_ref, m_s, *, batch):
    @pl.when(pl.program_id(0) == 0)
    def _():
        m_s[...] = _rms(mem_ref[...], mg_ref[...]).astype(BF16)

    k = _dot(m_s[...], wk_ref[...])
    v = _dot(m_s[...], wv_ref[...])
    qk = (_dot_nt(wq_ref[...], k) * (MEM_HEAD_DIM ** -0.5)).astype(BF16)
    vo = _dot(v, wo_ref[...]).astype(BF16)
    for b in range(batch):
        qk_ref[b] = qk[:, b * N_MEM:(b + 1) * N_MEM]
        vo_ref[b] = vo[b * N_MEM:(b + 1) * N_MEM, :]


def _mem_proj(mem, gain, w_kv, w_q, w_o, *, batch):
    dh = MEM_HEAD_DIM
    scratch = [pltpu.VMEM((batch * N_MEM, D_MODEL), BF16)]
    return pl.pallas_call(
        functools.partial(_mem_proj_kernel, batch=batch),
        grid=(MEM_HEADS,),
        in_specs=[
            _resident((batch * N_MEM, D_MODEL)),
            _resident((1, D_MODEL)),
            pl.BlockSpec((D_MODEL, dh), lambda i: (0, i)),
            pl.BlockSpec((D_MODEL, dh), lambda i: (0, MEM_HEADS + i)),
            pl.BlockSpec((D_MODEL, dh), lambda i: (0, i)),
            pl.BlockSpec((dh, D_MODEL), lambda i: (i, 0)),
        ],
        out_specs=[pl.BlockSpec((batch, D_MODEL, N_MEM), lambda i: (0, 0, i)),
                   pl.BlockSpec((batch, N_MEM, D_MODEL), lambda i: (0, i, 0))],
        out_shape=[jax.ShapeDtypeStruct((batch, D_MODEL, MEM_HEADS * N_MEM), BF16),
                   jax.ShapeDtypeStruct((batch, MEM_HEADS * N_MEM, D_MODEL), BF16)],
        scratch_shapes=scratch,
        compiler_params=_params(
            _nbytes((batch * N_MEM, D_MODEL), F32) + sum(_nbytes(b.shape, b.dtype) for b in scratch),
            4 * _nbytes((D_MODEL, dh), F32) + 2 * _nbytes((batch, D_MODEL, N_MEM), BF16),
            (batch * N_MEM, D_MODEL), 4),
        name="mem_proj",
    )(mem, gain, w_kv, w_kv, w_q, w_o)


XATTN_ROW_SPLITS = 4


def _xattn_kernel(x_ref, g_ref, qk_ref, vo_ref, o_ref):
    rows_per_split = x_ref.shape[0] // XATTN_ROW_SPLITS

    def probs(lg):
        out = []
        for hd in range(MEM_HEADS):
            seg = lg[:, hd * N_MEM:(hd + 1) * N_MEM]
            e = jnp.exp(seg - jnp.max(seg, axis=1, keepdims=True))
            out.append((e / jnp.sum(e, axis=1, keepdims=True)).astype(BF16))
        return jnp.concatenate(out, axis=1)

    def logits(r):
        rows = slice(r * rows_per_split, (r + 1) * rows_per_split)
        return rows, _dot(_rms(x_ref[rows, :], g_ref[...]), qk_ref[0])

    pending = logits(0)
    for r in range(XATTN_ROW_SPLITS):
        rows, lg = pending
        if r + 1 < XATTN_ROW_SPLITS:
            pending = logits(r + 1)
        o_ref[rows, :] = x_ref[rows, :] + _dot(probs(lg), vo_ref[0])


def _xattn(x, gain, qk, vo, *, seq_len):
    t = x.shape[0]
    tm = XATTN_TOKEN_TILE
    tiles_per_seq = seq_len // tm
    return pl.pallas_call(
        _xattn_kernel,
        grid=(t // tm,),
        in_specs=[
            pl.BlockSpec((tm, D_MODEL), lambda i: (i, 0)),
            _resident((1, D_MODEL)),
            pl.BlockSpec((1, D_MODEL, D_MODEL), lambda i: (i // tiles_per_seq, 0, 0)),
            pl.BlockSpec((1, D_MODEL, D_MODEL), lambda i: (i // tiles_per_seq, 0, 0)),
        ],
        out_specs=pl.BlockSpec((tm, D_MODEL), lambda i: (i, 0)),
        out_shape=jax.ShapeDtypeStruct((t, D_MODEL), F32),
        compiler_params=_params(
            0, 2 * _nbytes((tm, D_MODEL), F32) + 2 * _nbytes((D_MODEL, D_MODEL), BF16), (tm, D_MODEL), 6),
        name="xattn",
    )(x, gain, qk, vo)


def kernel(x, mem, positions, rel_bias, ffn1_norm, ffn1_w_gu, ffn1_w_down, mix_norm, w_in, sinks, conv_w, w_out, xattn_norm, mem_norm, xattn_wq, xattn_wkv, xattn_wo, ffn2_norm, ffn2_w_gu, ffn2_w_down, final_norm):
    batch, seq_len, _ = x.shape
    depth = w_in.shape[0]
    tokens = batch * seq_len
    nb = seq_len // BLOCK

    xt = x.reshape(tokens, D_MODEL)
    posq = positions.reshape(tokens // TOKEN_TILE, TOKEN_TILE // BLOCK, BLOCK)
    pos_blocks = positions.reshape(batch, nb, BLOCK)
    prev = jnp.concatenate([jnp.full((batch, 1, BLOCK), POS_PAD, positions.dtype), pos_blocks[:, :-1]], axis=1)
    posk = jnp.concatenate([prev, pos_blocks], axis=2).reshape(tokens // TOKEN_TILE, TOKEN_TILE // BLOCK, 2 * BLOCK)
    rel_bias_t = rel_bias.T
    final_gain = final_norm.reshape(1, D_MODEL)

    for l in range(depth):
        xt = _ffn(xt, ffn1_norm[l].reshape(1, D_MODEL), ffn1_w_gu[l], ffn1_w_down[l],
                  final_gain, final_norm=False)
        qk, vo = _mem_proj(mem.reshape(batch * N_MEM, D_MODEL), mem_norm[l].reshape(1, D_MODEL),
                           xattn_wkv[l], xattn_wq[l], xattn_wo[l], batch=batch)
        xt = _mix(xt, posq, posk, rel_bias_t, sinks[l], mix_norm[l].reshape(1, D_MODEL),
                  w_in[l], conv_w[l], w_out[l], seq_len=seq_len)
        xt = _xattn(xt, xattn_norm[l].reshape(1, D_MODEL), qk, vo, seq_len=seq_len)
        xt = _ffn(xt, ffn2_norm[l].reshape(1, D_MODEL), ffn2_w_gu[l], ffn2_w_down[l],
                  final_gain, final_norm=(l == depth - 1))
    return xt.reshape(batch, seq_len, D_MODEL)
```

```python
import functools
import math

import jax
import jax.numpy as jnp
from jax import lax
from jax.experimental import pallas as pl
from jax.experimental.pallas import tpu as pltpu

D_MODEL = 1024
D_FF = 2816
N_MEM = 256
MEM_HEADS = 4
MEM_HEAD_DIM = D_MODEL // MEM_HEADS
HEAD_DIM = 64
SWA_HEADS = 16
SWA_KV_HEADS = 4
WINDOW = 128
BLOCK = 128
REL_BUCKETS = 32
REL_MAX_DIST = 128
CONV_K = 3
EPS = 1e-6
NEG_INF = -1e30
POS_PAD = 1 << 30

Q_WIDTH = SWA_HEADS * HEAD_DIM
KV_WIDTH = SWA_KV_HEADS * HEAD_DIM
OFF_Q = 0
OFF_K = OFF_Q + Q_WIDTH
OFF_V = OFF_K + KV_WIDTH
OFF_C = OFF_V + KV_WIDTH
OFF_B = OFF_C + D_MODEL
OFF_U = OFF_B + D_MODEL
OFF_GA = OFF_U + D_MODEL
OFF_GC = OFF_GA + D_MODEL
IN_WIDTH = OFF_GC + D_MODEL

SUBLANES = 8
MXU_COLS = 256
V7X_VMEM_BYTES = 64 * 1024 * 1024
VMEM_REQUEST_CAP = V7X_VMEM_BYTES - 4 * 1024 * 1024

TOKEN_TILE = 512
XATTN_TOKEN_TILE = 1024
FFN_TOKEN_TILE = 512
FF_CHUNK = MXU_COLS
PAIR = 2 * HEAD_DIM

BF16 = jnp.bfloat16
F32 = jnp.float32


def _rms(x, g):
    return x * lax.rsqrt(jnp.mean(x * x, axis=-1, keepdims=True) + EPS) * g


def _dot(a, b):
    return jnp.dot(a.astype(BF16), b.astype(BF16), preferred_element_type=F32)


def _dot_nt(a, b):
    return lax.dot_general(a.astype(BF16), b.astype(BF16), (((1,), (1,)), ((), ())),
                           preferred_element_type=F32)


def _resident(shape):
    return pl.BlockSpec(shape, lambda i: (0,) * len(shape), pipeline_mode=pl.Buffered(1))


def _nbytes(shape, dtype):
    return math.prod(shape) * jnp.dtype(dtype).itemsize


def _params(resident_bytes, per_step_bytes, tile_shape, tile_temporaries):
    request = resident_bytes + 2 * per_step_bytes + tile_temporaries * _nbytes(tile_shape, F32)
    return pltpu.CompilerParams(dimension_semantics=("arbitrary",),
                                vmem_limit_bytes=min(VMEM_REQUEST_CAP, request))


FF_CHUNKS = D_FF // FF_CHUNK
OUT_CHUNKS = D_MODEL // MXU_COLS


def _ffn_kernel(x_ref, xnext_ref, g_ref, wgu_hbm, wd_hbm, fg_ref, o_ref, wgu_v, wd_v, act_ref, h_s, act0_s, sems,
                *, final_norm):
    def gate_copy(j):
        return pltpu.make_async_copy(wgu_hbm.at[:, pl.ds(j * FF_CHUNK, FF_CHUNK)], wgu_v.at[j], sems.at[j])

    def up_copy(j):
        return pltpu.make_async_copy(wgu_hbm.at[:, pl.ds(D_FF + j * FF_CHUNK, FF_CHUNK)],
                                     wgu_v.at[FF_CHUNKS + j], sems.at[FF_CHUNKS + j])

    def down_copy(c):
        return pltpu.make_async_copy(wd_hbm.at[:, pl.ds(c * MXU_COLS, MXU_COLS)], wd_v.at[c],
                                     sems.at[2 * FF_CHUNKS + c])

    step = pl.program_id(0)
    cur, nxt = step % 2, (step + 1) % 2

    def chunk_act(h, j):
        gate = _dot(h, wgu_v[j])
        up = _dot(h, wgu_v[FF_CHUNKS + j])
        return (gate * jax.nn.sigmoid(gate) * up).astype(BF16)

    def body(first):
        if first:
            h = _rms(x_ref[...], g_ref[...]).astype(BF16)
            gate_copy(0).wait()
            up_copy(0).wait()
            act_ref[:, 0:FF_CHUNK] = chunk_act(h, 0)
        else:
            h = h_s[cur]
            act_ref[:, 0:FF_CHUNK] = act0_s[cur]
        for j in range(1, FF_CHUNKS):
            if first:
                gate_copy(j).wait()
                up_copy(j).wait()
            act_ref[:, j * FF_CHUNK:(j + 1) * FF_CHUNK] = chunk_act(h, j)
        if first:
            for c in range(OUT_CHUNKS):
                down_copy(c).wait()
        y = x_ref[...] + 0.5 * jnp.concatenate([_dot(act_ref[...], wd_v[c]) for c in range(OUT_CHUNKS)], axis=1)
        h_next = _rms(xnext_ref[...], g_ref[...]).astype(BF16)
        h_s[nxt] = h_next
        act0_s[nxt] = chunk_act(h_next, 0)
        if final_norm:
            y = _rms(y, fg_ref[...])
        o_ref[...] = y

    first_step = step == 0

    @pl.when(first_step)
    def _():
        for j in range(FF_CHUNKS):
            gate_copy(j).start()
            up_copy(j).start()
        for c in range(OUT_CHUNKS):
            down_copy(c).start()
        body(True)

    @pl.when(jnp.logical_not(first_step))
    def _():
        body(False)


def _ffn(x, gain, w_gu, w_down, final_gain, *, final_norm):
    t = x.shape[0]
    tm = FFN_TOKEN_TILE
    scratch = [
        pltpu.VMEM((2 * FF_CHUNKS, D_MODEL, FF_CHUNK), F32),
        pltpu.VMEM((OUT_CHUNKS, D_FF, MXU_COLS), F32),
        pltpu.VMEM((tm, D_FF), BF16),
        pltpu.VMEM((2, tm, D_MODEL), BF16),
        pltpu.VMEM((2, tm, FF_CHUNK), BF16),
    ]
    return pl.pallas_call(
        functools.partial(_ffn_kernel, final_norm=final_norm),
        grid=(t // tm,),
        in_specs=[
            pl.BlockSpec((tm, D_MODEL), lambda i: (i, 0)),
            pl.BlockSpec((tm, D_MODEL), lambda i: (jnp.minimum(i + 1, t // tm - 1), 0)),
            _resident((1, D_MODEL)),
            pl.BlockSpec(memory_space=pl.ANY),
            pl.BlockSpec(memory_space=pl.ANY),
            _resident((1, D_MODEL)),
        ],
        out_specs=pl.BlockSpec((tm, D_MODEL), lambda i: (i, 0)),
        out_shape=jax.ShapeDtypeStruct((t, D_MODEL), F32),
        scratch_shapes=scratch + [pltpu.SemaphoreType.DMA((2 * FF_CHUNKS + OUT_CHUNKS,))],
        compiler_params=_params(sum(_nbytes(b.shape, b.dtype) for b in scratch),
                                3 * _nbytes((tm, D_MODEL), F32), (tm, D_MODEL), 2),
        name="ffn_final" if final_norm else "ffn",
    )(x, x, gain, w_gu, w_down, final_gain)


def _bias_lut(tab_ref):
    n = lax.broadcasted_iota(jnp.int32, (SWA_HEADS, WINDOW), 1).astype(F32)
    max_exact = REL_BUCKETS // 2
    nf = jnp.maximum(n, 1.0)
    large = max_exact + jnp.floor(jnp.log(nf / max_exact) / math.log(REL_MAX_DIST / max_exact)
                                  * (REL_BUCKETS - max_exact))
    large = jnp.minimum(large, REL_BUCKETS - 1.0)
    bucket = jnp.where(n < max_exact, n, large)
    lut = jnp.broadcast_to(tab_ref[:, 0:1], (SWA_HEADS, WINDOW))
    for b in range(1, REL_BUCKETS):
        lut = jnp.where(bucket >= b, tab_ref[:, b:b + 1], lut)
    return lut


def _key_head_slabs(t):
    lane = lax.broadcasted_iota(jnp.int32, (t.shape[0], PAIR), 1)
    low_half = lane < HEAD_DIM
    los, his = [], []
    for p in range(SWA_KV_HEADS // 2):
        pair = t[:, p * PAIR:(p + 1) * PAIR]
        swapped = pltpu.roll(pair, HEAD_DIM, axis=1)
        zero = jnp.zeros_like(pair)
        los += [jnp.where(low_half, pair, zero), jnp.where(low_half, swapped, zero)]
        his += [jnp.where(low_half, zero, swapped), jnp.where(low_half, zero, pair)]
    return [a.astype(BF16) for a in los], [a.astype(BF16) for a in his]


def _value_head_slabs_t(t):
    los, his = [], []
    for p in range(SWA_KV_HEADS // 2):
        pair_t = t[:, p * PAIR:(p + 1) * PAIR].T
        zero = jnp.zeros((HEAD_DIM, t.shape[0]), F32)
        for head_t in (pair_t[0:HEAD_DIM], pair_t[HEAD_DIM:PAIR]):
            los.append(jnp.concatenate([head_t, zero], axis=0))
            his.append(jnp.concatenate([zero, head_t], axis=0))
    return [a.astype(BF16) for a in los], [a.astype(BF16) for a in his]


def _mix_kernel(x_ref, posq_ref, posk_ref, posq_next_ref, posk_next_ref, tab_ref, sink_ref, g_ref, win_ref,
                convw_ref, wout_ref, o_ref,
                q_s, klo_s, khi_s, vtlo_s, vthi_s, cu_s, merged_s, relm_s, relm_next_s, bias_s, lut_s, *, tiles_per_seq):
    tm = x_ref.shape[0]
    nblk = tm // BLOCK
    step = pl.program_id(0)
    first = (step % tiles_per_seq) == 0
    next_first = ((step + 1) % tiles_per_seq) == 0

    @pl.when(first)
    def _():
        for s in (klo_s, khi_s):
            s[:, 0:BLOCK, :] = jnp.zeros((SWA_KV_HEADS, BLOCK, PAIR), BF16)
        for s in (vtlo_s, vthi_s):
            s[:, 0] = jnp.zeros((SWA_KV_HEADS, PAIR, BLOCK), BF16)
        cu_s[0:SUBLANES, :] = jnp.zeros((SUBLANES, D_MODEL), F32)

    @pl.when(jnp.logical_not(first))
    def _():
        for s in (klo_s, khi_s):
            s[:, 0:BLOCK, :] = s[:, tm:tm + BLOCK, :]
        for s in (vtlo_s, vthi_s):
            s[:, 0] = s[:, nblk]
        cu_s[0:SUBLANES, :] = cu_s[tm:tm + SUBLANES, :]

    def masked_rel(pq_ref, pk_ref, j):
        pos_k = jnp.concatenate(
            [jnp.broadcast_to(pk_ref[0, j:j + 1, c * BLOCK:(c + 1) * BLOCK], (BLOCK, BLOCK)).T
             for c in range(2)], axis=0)
        rel = pq_ref[0, j:j + 1, :] - pos_k
        return jnp.where((rel >= 0) & (rel < WINDOW), rel, -1)

    def block_slots(sequence_start):
        return [jnp.where(sequence_start, nblk, 0)] + list(range(1, nblk))

    def rebuild(slot, relm):
        relm_s[slot] = relm
        shown = relm >= 0
        idx = jnp.maximum(relm, 0)
        for hd in range(SWA_HEADS):
            lut = jnp.broadcast_to(lut_s[hd:hd + 1, :], (2 * BLOCK, WINDOW))
            bias_s[slot, hd] = jnp.where(shown, jnp.take_along_axis(lut, idx, axis=1), NEG_INF)

    slots = block_slots(first)
    next_slots = block_slots(next_first)

    @pl.when(step == 0)
    def _():
        relm_s[...] = jnp.full(relm_s.shape, -2, jnp.int32)
        lut_s[...] = _bias_lut(tab_ref)
        for j in range(nblk):
            rebuild(slots[j], masked_rel(posq_ref, posk_ref, j))

    next_misses = []
    for j in range(nblk):
        relm = masked_rel(posq_next_ref, posk_next_ref, j)
        relm_next_s[j] = relm
        next_misses.append(jnp.sum((relm != relm_s[next_slots[j]]).astype(jnp.int32)))

    h = _rms(x_ref[...], g_ref[...]).astype(BF16)

    k_lo, k_hi = _key_head_slabs(_dot(h, win_ref[:, OFF_K:OFF_K + KV_WIDTH]))
    vt_lo, vt_hi = _value_head_slabs_t(_dot(h, win_ref[:, OFF_V:OFF_V + KV_WIDTH]))
    for g in range(SWA_KV_HEADS):
        klo_s[g, BLOCK:BLOCK + tm, :] = k_lo[g]
        khi_s[g, BLOCK:BLOCK + tm, :] = k_hi[g]
        for j in range(nblk):
            vtlo_s[g, j + 1] = vt_lo[g][:, j * BLOCK:(j + 1) * BLOCK]
            vthi_s[g, j + 1] = vt_hi[g][:, j * BLOCK:(j + 1) * BLOCK]
    q_s[...] = (_dot(h, win_ref[:, OFF_Q:OFF_Q + Q_WIDTH]) * (HEAD_DIM ** -0.5)).astype(BF16)

    def qk(blk, g):
        rows = slice(blk * BLOCK, (blk + 1) * BLOCK)
        keys = slice(blk * BLOCK, (blk + 2) * BLOCK)
        c0 = g * 2 * PAIR
        q2 = jnp.concatenate([q_s[rows, c0:c0 + PAIR], q_s[rows, c0 + PAIR:c0 + 2 * PAIR]], axis=0)
        return _dot_nt(klo_s[g, keys, :], q2), _dot_nt(khi_s[g, keys, :], q2)

    def softmax_pv(blk, g, logits_t):
        probs_t = [[None, None], [None, None]]
        inv = [[None, None], [None, None]]
        for half in range(2):
            for pair in range(2):
                head = g * 4 + pair * 2 + half
                lt = logits_t[half][:, pair * BLOCK:(pair + 1) * BLOCK] + bias_s[slots[blk], head]
                sink = sink_ref[head]
                m = jnp.maximum(jnp.max(lt, axis=0, keepdims=True), sink)
                e = jnp.exp(lt - m)
                inv[half][pair] = 1.0 / (jnp.sum(e, axis=0, keepdims=True) + jnp.exp(sink - m))
                probs_t[half][pair] = e.astype(BF16)
        v_lo_t = jnp.concatenate([vtlo_s[g, blk], vtlo_s[g, blk + 1]], axis=1)
        v_hi_t = jnp.concatenate([vthi_s[g, blk], vthi_s[g, blk + 1]], axis=1)
        out_t = (_dot(v_lo_t, jnp.concatenate(probs_t[0], axis=1))
                 + _dot(v_hi_t, jnp.concatenate(probs_t[1], axis=1)))
        scale = jnp.concatenate(
            [jnp.broadcast_to(jnp.concatenate(inv[0], axis=1), (HEAD_DIM, 2 * BLOCK)),
             jnp.broadcast_to(jnp.concatenate(inv[1], axis=1), (HEAD_DIM, 2 * BLOCK))], axis=0)
        return (out_t * scale).T

    units = [(blk, g) for g in range(SWA_KV_HEADS) for blk in range(nblk)]
    logits_next = qk(*units[0])
    for g in range(SWA_KV_HEADS):
        cols = slice(g * 2 * PAIR, (g + 1) * 2 * PAIR)

        def proj(off):
            return _dot(h, win_ref[:, off + g * 2 * PAIR:off + (g + 1) * 2 * PAIR])

        def conv_taps(_):
            cu = proj(OFF_C) * proj(OFF_U)
            cu_s[SUBLANES:SUBLANES + tm, cols] = cu
            return (convw_ref[0:1, cols] * cu_s[SUBLANES - 2:SUBLANES - 2 + tm, cols]
                    + convw_ref[1:2, cols] * cu_s[SUBLANES - 1:SUBLANES - 1 + tm, cols]
                    + convw_ref[2:3, cols] * cu)

        stages = [conv_taps,
                  lambda conv: proj(OFF_B) * conv,
                  lambda conv: jax.nn.sigmoid(proj(OFF_GC)) * conv]
        attn_blocks = []
        conv = None
        for blk in range(nblk):
            logits_t = logits_next
            nxt = g * nblk + blk + 1
            if nxt < len(units):
                logits_next = qk(*units[nxt])
            if blk < len(stages):
                conv = stages[blk](conv)
            out = softmax_pv(blk, g, logits_t)
            attn_blocks.append(jnp.concatenate([out[0:BLOCK], out[BLOCK:2 * BLOCK]], axis=1))
        for stage in stages[nblk:]:
            conv = stage(conv)
        attn = jnp.concatenate(attn_blocks, axis=0)
        merged_s[:, cols] = (jax.nn.sigmoid(proj(OFF_GA)) * attn + conv).astype(BF16)

    o_ref[...] = x_ref[...] + _dot(merged_s[...], wout_ref[...])

    for j in range(nblk):
        @pl.when(next_misses[j] != 0)
        def _(j=j):
            rebuild(next_slots[j], relm_next_s[j])


def _mix(x, posq, posk, rel_bias_t, sinks, gain, w_in, conv_w, w_out, *, seq_len):
    t = x.shape[0]
    tm = TOKEN_TILE
    nblk = tm // BLOCK
    k_scratch = pltpu.VMEM((SWA_KV_HEADS, BLOCK + tm, PAIR), BF16)
    vt_scratch = pltpu.VMEM((SWA_KV_HEADS, nblk + 1, PAIR, BLOCK), BF16)
    scratch = [
        pltpu.VMEM((tm, Q_WIDTH), BF16),
        k_scratch, k_scratch, vt_scratch, vt_scratch,
        pltpu.VMEM((SUBLANES + tm, D_MODEL), F32),
        pltpu.VMEM((tm, D_MODEL), BF16),
        pltpu.VMEM((nblk + 1, 2 * BLOCK, BLOCK), jnp.int32),
        pltpu.VMEM((nblk, 2 * BLOCK, BLOCK), jnp.int32),
        pltpu.VMEM((nblk + 1, SWA_HEADS, 2 * BLOCK, BLOCK), F32),
        pltpu.VMEM((SWA_HEADS, WINDOW), F32),
    ]
    return pl.pallas_call(
        functools.partial(_mix_kernel, tiles_per_seq=seq_len // tm),
        grid=(t // tm,),
        in_specs=[
            pl.BlockSpec((tm, D_MODEL), lambda i: (i, 0)),
            pl.BlockSpec((1, nblk, BLOCK), lambda i: (i, 0, 0)),
            pl.BlockSpec((1, nblk, 2 * BLOCK), lambda i: (i, 0, 0)),
            pl.BlockSpec((1, nblk, BLOCK), lambda i: (jnp.minimum(i + 1, t // tm - 1), 0, 0)),
            pl.BlockSpec((1, nblk, 2 * BLOCK), lambda i: (jnp.minimum(i + 1, t // tm - 1), 0, 0)),
            _resident((SWA_HEADS, REL_BUCKETS)),
            pl.BlockSpec(memory_space=pltpu.SMEM),
            _resident((1, D_MODEL)),
            _resident((D_MODEL, IN_WIDTH)),
            _resident((CONV_K, D_MODEL)),
            _resident((D_MODEL, D_MODEL)),
        ],
        out_specs=pl.BlockSpec((tm, D_MODEL), lambda i: (i, 0)),
        out_shape=jax.ShapeDtypeStruct((t, D_MODEL), F32),
        scratch_shapes=scratch,
        compiler_params=_params(
            sum(_nbytes(b.shape, b.dtype) for b in scratch) + _nbytes(w_in.shape, F32) + _nbytes(w_out.shape, F32),
            2 * _nbytes((tm, D_MODEL), F32), (tm, D_MODEL), 4),
        name="mix",
    )(x, posq, posk, posq, posk, rel_bias_t, sinks, gain, w_in, conv_w, w_out)


def _mem_proj_kernel(mem_ref, mg_ref, wk_ref, wv_ref, wq_ref, wo_ref, qk_ref, vo_ref, m_s, *, batch):
    @pl.when(pl.program_id(0) == 0)
    def _():
        m_s[...] = _rms(mem_ref[...], mg_ref[...]).astype(BF16)

    k = _dot(m_s[...], wk_ref[...])
    v = _dot(m_s[...], wv_ref[...])
    qk = (_dot_nt(wq_ref[...], k) * (MEM_HEAD_DIM ** -0.5)).astype(BF16)
    vo = _dot(v, wo_ref[...]).astype(BF16)
    for b in range(batch):
        qk_ref[b] = qk[:, b * N_MEM:(b + 1) * N_MEM]
        vo_ref[b] = vo[b * N_MEM:(b + 1) * N_MEM, :]


def _mem_proj(mem, gain, w_kv, w_q, w_o, *, batch):
    dh = MEM_HEAD_DIM
    scratch = [pltpu.VMEM((batch * N_MEM, D_MODEL), BF16)]
    return pl.pallas_call(
        functools.partial(_mem_proj_kernel, batch=batch),
        grid=(MEM_HEADS,),
        in_specs=[
            _resident((batch * N_MEM, D_MODEL)),
            _resident((1, D_MODEL)),
            pl.BlockSpec((D_MODEL, dh), lambda i: (0, i)),
            pl.BlockSpec((D_MODEL, dh), lambda i: (0, MEM_HEADS + i)),
            pl.BlockSpec((D_MODEL, dh), lambda i: (0, i)),
            pl.BlockSpec((dh, D_MODEL), lambda i: (i, 0)),
        ],
        out_specs=[pl.BlockSpec((batch, D_MODEL, N_MEM), lambda i: (0, 0, i)),
                   pl.BlockSpec((batch, N_MEM, D_MODEL), lambda i: (0, i, 0))],
        out_shape=[jax.ShapeDtypeStruct((batch, D_MODEL, MEM_HEADS * N_MEM), BF16),
                   jax.ShapeDtypeStruct((batch, MEM_HEADS * N_MEM, D_MODEL), BF16)],
        scratch_shapes=scratch,
        compiler_params=_params(
            _nbytes((batch * N_MEM, D_MODEL), F32) + sum(_nbytes(b.shape, b.dtype) for b in scratch),
            4 * _nbytes((D_MODEL, dh), F32) + 2 * _nbytes((batch, D_MODEL, N_MEM), BF16),
            (batch * N_MEM, D_MODEL), 10),
        name="mem_proj",
    )(mem, gain, w_kv, w_kv, w_q, w_o)


XATTN_ROW_SPLITS = 4


def _xattn_kernel(x_ref, g_ref, qk_ref, vo_ref, o_ref):
    rows_per_split = x_ref.shape[0] // XATTN_ROW_SPLITS

    def probs(lg):
        out = []
        for hd in range(MEM_HEADS):
            seg = lg[:, hd * N_MEM:(hd + 1) * N_MEM]
            e = jnp.exp(seg - jnp.max(seg, axis=1, keepdims=True))
            out.append((e / jnp.sum(e, axis=1, keepdims=True)).astype(BF16))
        return jnp.concatenate(out, axis=1)

    def logits(r):
        rows = slice(r * rows_per_split, (r + 1) * rows_per_split)
        return rows, _dot(_rms(x_ref[rows, :], g_ref[...]), qk_ref[0])

    pending = logits(0)
    for r in range(XATTN_ROW_SPLITS):
        rows, lg = pending
        if r + 1 < XATTN_ROW_SPLITS:
            pending = logits(r + 1)
        o_ref[rows, :] = x_ref[rows, :] + _dot(probs(lg), vo_ref[0])


def _xattn(x, gain, qk, vo, *, seq_len):
    t = x.shape[0]
    tm = XATTN_TOKEN_TILE
    tiles_per_seq = seq_len // tm
    return pl.pallas_call(
        _xattn_kernel,
        grid=(t // tm,),
        in_specs=[
            pl.BlockSpec((tm, D_MODEL), lambda i: (i, 0)),
            _resident((1, D_MODEL)),
            pl.BlockSpec((1, D_MODEL, D_MODEL), lambda i: (i // tiles_per_seq, 0, 0)),
            pl.BlockSpec((1, D_MODEL, D_MODEL), lambda i: (i // tiles_per_seq, 0, 0)),
        ],
        out_specs=pl.BlockSpec((tm, D_MODEL), lambda i: (i, 0)),
        out_shape=jax.ShapeDtypeStruct((t, D_MODEL), F32),
        compiler_params=_params(
            0, 2 * _nbytes((tm, D_MODEL), F32) + 2 * _nbytes((D_MODEL, D_MODEL), BF16), (tm, D_MODEL), 6),
        name="xattn",
    )(x, gain, qk, vo)


def kernel(x, mem, positions, rel_bias, ffn1_norm, ffn1_w_gu, ffn1_w_down, mix_norm, w_in, sinks, conv_w, w_out, xattn_norm, mem_norm, xattn_wq, xattn_wkv, xattn_wo, ffn2_norm, ffn2_w_gu, ffn2_w_down, final_norm):
    batch, seq_len, _ = x.shape
    depth = w_in.shape[0]
    tokens = batch * seq_len
    nb = seq_len // BLOCK

    xt = x.reshape(tokens, D_MODEL)
    posq = positions.reshape(tokens // TOKEN_TILE, TOKEN_TILE // BLOCK, BLOCK)
    pos_blocks = positions.reshape(batch, nb, BLOCK)
    prev = jnp.concatenate([jnp.full((batch, 1, BLOCK), POS_PAD, positions.dtype), pos_blocks[:, :-1]], axis=1)
    posk = jnp.concatenate([prev, pos_blocks], axis=2).reshape(tokens // TOKEN_TILE, TOKEN_TILE // BLOCK, 2 * BLOCK)
    rel_bias_t = rel_bias.T
    final_gain = final_norm.reshape(1, D_MODEL)

    for l in range(depth):
        xt = _ffn(xt, ffn1_norm[l].reshape(1, D_MODEL), ffn1_w_gu[l], ffn1_w_down[l],
                  final_gain, final_norm=False)
        qk, vo = _mem_proj(mem.reshape(batch * N_MEM, D_MODEL), mem_norm[l].reshape(1, D_MODEL),
                           xattn_wkv[l], xattn_wq[l], xattn_wo[l], batch=batch)
        xt = _mix(xt, posq, posk, rel_bias_t, sinks[l], mix_norm[l].reshape(1, D_MODEL),
                  w_in[l], conv_w[l], w_out[l], seq_len=seq_len)
        xt = _xattn(xt, xattn_norm[l].reshape(1, D_MODEL), qk, vo, seq_len=seq_len)
        xt = _ffn(xt, ffn2_norm[l].reshape(1, D_MODEL), ffn2_w_gu[l], ffn2_w_down[l],
                  final_gain, final_norm=(l == depth - 1))
    return xt.reshape(batch, seq_len, D_MODEL)
```

```python
import functools
import math

import jax
import jax.numpy as jnp
from jax import lax
from jax.experimental import pallas as pl
from jax.experimental.pallas import tpu as pltpu

D_MODEL = 1024
D_FF = 2816
N_MEM = 256
MEM_HEADS = 4
MEM_HEAD_DIM = D_MODEL // MEM_HEADS
HEAD_DIM = 64
SWA_HEADS = 16
SWA_KV_HEADS = 4
WINDOW = 128
BLOCK = 128
REL_BUCKETS = 32
REL_MAX_DIST = 128
CONV_K = 3
EPS = 1e-6
NEG_INF = -1e30
POS_PAD = 1 << 30

Q_WIDTH = SWA_HEADS * HEAD_DIM
KV_WIDTH = SWA_KV_HEADS * HEAD_DIM
OFF_Q = 0
OFF_K = OFF_Q + Q_WIDTH
OFF_V = OFF_K + KV_WIDTH
OFF_C = OFF_V + KV_WIDTH
OFF_B = OFF_C + D_MODEL
OFF_U = OFF_B + D_MODEL
OFF_GA = OFF_U + D_MODEL
OFF_GC = OFF_GA + D_MODEL
IN_WIDTH = OFF_GC + D_MODEL

SUBLANES = 8
MXU_COLS = 256
V7X_VMEM_BYTES = 64 * 1024 * 1024
VMEM_REQUEST_CAP = V7X_VMEM_BYTES - 4 * 1024 * 1024

TOKEN_TILE = 512
XATTN_TOKEN_TILE = 1024
FFN_TOKEN_TILE = 512
FF_CHUNK = MXU_COLS
PAIR = 2 * HEAD_DIM

BF16 = jnp.bfloat16
F32 = jnp.float32


def _rms(x, g):
    return x * lax.rsqrt(jnp.mean(x * x, axis=-1, keepdims=True) + EPS) * g


def _dot(a, b):
    return jnp.dot(a.astype(BF16), b.astype(BF16), preferred_element_type=F32)


def _dot_nt(a, b):
    return lax.dot_general(a.astype(BF16), b.astype(BF16), (((1,), (1,)), ((), ())),
                           preferred_element_type=F32)


def _resident(shape):
    return pl.BlockSpec(shape, lambda i: (0,) * len(shape), pipeline_mode=pl.Buffered(1))


def _nbytes(shape, dtype):
    return math.prod(shape) * jnp.dtype(dtype).itemsize


def _params(resident_bytes, per_step_bytes, tile_shape, tile_temporaries):
    request = resident_bytes + 2 * per_step_bytes + tile_temporaries * _nbytes(tile_shape, F32)
    return pltpu.CompilerParams(dimension_semantics=("arbitrary",),
                                vmem_limit_bytes=min(VMEM_REQUEST_CAP, request))


FF_CHUNKS = D_FF // FF_CHUNK
OUT_CHUNKS = D_MODEL // MXU_COLS


def _ffn_kernel(x_ref, xnext_ref, g_ref, wgu_hbm, wd_hbm, fg_ref, o_ref, wgu_v, wd_v, act_ref, h_s, act0_s, sems,
                *, final_norm):
    def gate_copy(j):
        return pltpu.make_async_copy(wgu_hbm.at[:, pl.ds(j * FF_CHUNK, FF_CHUNK)], wgu_v.at[j], sems.at[j])

    def up_copy(j):
        return pltpu.make_async_copy(wgu_hbm.at[:, pl.ds(D_FF + j * FF_CHUNK, FF_CHUNK)],
                                     wgu_v.at[FF_CHUNKS + j], sems.at[FF_CHUNKS + j])

    def down_copy(c):
        return pltpu.make_async_copy(wd_hbm.at[:, pl.ds(c * MXU_COLS, MXU_COLS)], wd_v.at[c],
                                     sems.at[2 * FF_CHUNKS + c])

    step = pl.program_id(0)
    cur, nxt = step % 2, (step + 1) % 2

    def chunk_act(h, j):
        gate = _dot(h, wgu_v[j])
        up = _dot(h, wgu_v[FF_CHUNKS + j])
        return (gate * jax.nn.sigmoid(gate) * up).astype(BF16)

    def body(first):
        if first:
            h = _rms(x_ref[...], g_ref[...]).astype(BF16)
            gate_copy(0).wait()
            up_copy(0).wait()
            act_ref[:, 0:FF_CHUNK] = chunk_act(h, 0)
        else:
            h = h_s[cur]
            act_ref[:, 0:FF_CHUNK] = act0_s[cur]
        for j in range(1, FF_CHUNKS):
            if first:
                gate_copy(j).wait()
                up_copy(j).wait()
            act_ref[:, j * FF_CHUNK:(j + 1) * FF_CHUNK] = chunk_act(h, j)
        if first:
            for c in range(OUT_CHUNKS):
                down_copy(c).wait()
        y = x_ref[...] + 0.5 * jnp.concatenate([_dot(act_ref[...], wd_v[c]) for c in range(OUT_CHUNKS)], axis=1)
        h_next = _rms(xnext_ref[...], g_ref[...]).astype(BF16)
        h_s[nxt] = h_next
        act0_s[nxt] = chunk_act(h_next, 0)
        if final_norm:
            y = _rms(y, fg_ref[...])
        o_ref[...] = y

    first_step = step == 0

    @pl.when(first_step)
    def _():
        for j in range(FF_CHUNKS):
            gate_copy(j).start(priority=0)
            up_copy(j).start(priority=1)
        for c in range(OUT_CHUNKS):
            down_copy(c).start(priority=c % 2)
        body(True)

    @pl.when(jnp.logical_not(first_step))
    def _():
        body(False)


def _ffn(x, gain, w_gu, w_down, final_gain, *, final_norm):
    t = x.shape[0]
    tm = FFN_TOKEN_TILE
    scratch = [
        pltpu.VMEM((2 * FF_CHUNKS, D_MODEL, FF_CHUNK), F32),
        pltpu.VMEM((OUT_CHUNKS, D_FF, MXU_COLS), F32),
        pltpu.VMEM((tm, D_FF), BF16),
        pltpu.VMEM((2, tm, D_MODEL), BF16),
        pltpu.VMEM((2, tm, FF_CHUNK), BF16),
    ]
    return pl.pallas_call(
        functools.partial(_ffn_kernel, final_norm=final_norm),
        grid=(t // tm,),
        in_specs=[
            pl.BlockSpec((tm, D_MODEL), lambda i: (i, 0)),
            pl.BlockSpec((tm, D_MODEL), lambda i: (jnp.minimum(i + 1, t // tm - 1), 0)),
            _resident((1, D_MODEL)),
            pl.BlockSpec(memory_space=pl.ANY),
            pl.BlockSpec(memory_space=pl.ANY),
            _resident((1, D_MODEL)),
        ],
        out_specs=pl.BlockSpec((tm, D_MODEL), lambda i: (i, 0)),
        out_shape=jax.ShapeDtypeStruct((t, D_MODEL), F32),
        scratch_shapes=scratch + [pltpu.SemaphoreType.DMA((2 * FF_CHUNKS + OUT_CHUNKS,))],
        compiler_params=_params(sum(_nbytes(b.shape, b.dtype) for b in scratch),
                                3 * _nbytes((tm, D_MODEL), F32), (tm, D_MODEL), 2),
        name="ffn_final" if final_norm else "ffn",
    )(x, x, gain, w_gu, w_down, final_gain)


def _bias_lut(tab_ref):
    n = lax.broadcasted_iota(jnp.int32, (SWA_HEADS, WINDOW), 1).astype(F32)
    max_exact = REL_BUCKETS // 2
    nf = jnp.maximum(n, 1.0)
    large = max_exact + jnp.floor(jnp.log(nf / max_exact) / math.log(REL_MAX_DIST / max_exact)
                                  * (REL_BUCKETS - max_exact))
    large = jnp.minimum(large, REL_BUCKETS - 1.0)
    bucket = jnp.where(n < max_exact, n, large)
    lut = jnp.broadcast_to(tab_ref[:, 0:1], (SWA_HEADS, WINDOW))
    for b in range(1, REL_BUCKETS):
        lut = jnp.where(bucket >= b, tab_ref[:, b:b + 1], lut)
    return lut


def _key_head_slabs(t):
    lane = lax.broadcasted_iota(jnp.int32, (t.shape[0], PAIR), 1)
    low_half = lane < HEAD_DIM
    los, his = [], []
    for p in range(SWA_KV_HEADS // 2):
        pair = t[:, p * PAIR:(p + 1) * PAIR]
        swapped = pltpu.roll(pair, HEAD_DIM, axis=1)
        zero = jnp.zeros_like(pair)
        los += [jnp.where(low_half, pair, zero), jnp.where(low_half, swapped, zero)]
        his += [jnp.where(low_half, zero, swapped), jnp.where(low_half, zero, pair)]
    return [a.astype(BF16) for a in los], [a.astype(BF16) for a in his]


def _value_head_slabs_t(t):
    los, his = [], []
    for p in range(SWA_KV_HEADS // 2):
        pair_t = t[:, p * PAIR:(p + 1) * PAIR].T
        zero = jnp.zeros((HEAD_DIM, t.shape[0]), F32)
        for head_t in (pair_t[0:HEAD_DIM], pair_t[HEAD_DIM:PAIR]):
            los.append(jnp.concatenate([head_t, zero], axis=0))
            his.append(jnp.concatenate([zero, head_t], axis=0))
    return [a.astype(BF16) for a in los], [a.astype(BF16) for a in his]


def _mix_kernel(x_ref, posq_ref, posk_ref, posq_next_ref, posk_next_ref, tab_ref, sink_ref, g_ref, win_ref,
                convw_ref, wout_ref, o_ref,
                q_s, klo_s, khi_s, vtlo_s, vthi_s, cu_s, merged_s, relm_s, relm_next_s, bias_s, lut_s, *, tiles_per_seq):
    tm = x_ref.shape[0]
    nblk = tm // BLOCK
    step = pl.program_id(0)
    first = (step % tiles_per_seq) == 0
    next_first = ((step + 1) % tiles_per_seq) == 0

    @pl.when(first)
    def _():
        for s in (klo_s, khi_s):
            s[:, 0:BLOCK, :] = jnp.zeros((SWA_KV_HEADS, BLOCK, PAIR), BF16)
        for s in (vtlo_s, vthi_s):
            s[:, 0] = jnp.zeros((SWA_KV_HEADS, PAIR, BLOCK), BF16)
        cu_s[0:SUBLANES, :] = jnp.zeros((SUBLANES, D_MODEL), F32)

    @pl.when(jnp.logical_not(first))
    def _():
        for s in (klo_s, khi_s):
            s[:, 0:BLOCK, :] = s[:, tm:tm + BLOCK, :]
        for s in (vtlo_s, vthi_s):
            s[:, 0] = s[:, nblk]
        cu_s[0:SUBLANES, :] = cu_s[tm:tm + SUBLANES, :]

    def masked_rel(pq_ref, pk_ref, j):
        pos_k = jnp.concatenate(
            [jnp.broadcast_to(pk_ref[0, j:j + 1, c * BLOCK:(c + 1) * BLOCK], (BLOCK, BLOCK)).T
             for c in range(2)], axis=0)
        rel = pq_ref[0, j:j + 1, :] - pos_k
        return jnp.where((rel >= 0) & (rel < WINDOW), rel, -1)

    def block_slots(sequence_start):
        return [jnp.where(sequence_start, nblk, 0)] + list(range(1, nblk))

    def rebuild(slot, relm):
        relm_s[slot] = relm
        shown = relm >= 0
        idx = jnp.maximum(relm, 0)
        for hd in range(SWA_HEADS):
            lut = jnp.broadcast_to(lut_s[hd:hd + 1, :], (2 * BLOCK, WINDOW))
            bias_s[slot, hd] = jnp.where(shown, jnp.take_along_axis(lut, idx, axis=1), NEG_INF)

    slots = block_slots(first)
    next_slots = block_slots(next_first)

    @pl.when(step == 0)
    def _():
        relm_s[...] = jnp.full(relm_s.shape, -2, jnp.int32)
        lut_s[...] = _bias_lut(tab_ref)
        for j in range(nblk):
            rebuild(slots[j], masked_rel(posq_ref, posk_ref, j))

    next_misses = []
    for j in range(nblk):
        relm = masked_rel(posq_next_ref, posk_next_ref, j)
        relm_next_s[j] = relm
        next_misses.append(jnp.sum((relm != relm_s[next_slots[j]]).astype(jnp.int32)))

    h = _rms(x_ref[...], g_ref[...]).astype(BF16)

    k_lo, k_hi = _key_head_slabs(_dot(h, win_ref[:, OFF_K:OFF_K + KV_WIDTH]))
    vt_lo, vt_hi = _value_head_slabs_t(_dot(h, win_ref[:, OFF_V:OFF_V + KV_WIDTH]))
    for g in range(SWA_KV_HEADS):
        klo_s[g, BLOCK:BLOCK + tm, :] = k_lo[g]
        khi_s[g, BLOCK:BLOCK + tm, :] = k_hi[g]
        for j in range(nblk):
            vtlo_s[g, j + 1] = vt_lo[g][:, j * BLOCK:(j + 1) * BLOCK]
            vthi_s[g, j + 1] = vt_hi[g][:, j * BLOCK:(j + 1) * BLOCK]
    q_s[...] = (_dot(h, win_ref[:, OFF_Q:OFF_Q + Q_WIDTH]) * (HEAD_DIM ** -0.5)).astype(BF16)

    def qk(blk, g):
        rows = slice(blk * BLOCK, (blk + 1) * BLOCK)
        keys = slice(blk * BLOCK, (blk + 2) * BLOCK)
        c0 = g * 2 * PAIR
        q2 = jnp.concatenate([q_s[rows, c0:c0 + PAIR], q_s[rows, c0 + PAIR:c0 + 2 * PAIR]], axis=0)
        return _dot_nt(klo_s[g, keys, :], q2), _dot_nt(khi_s[g, keys, :], q2)

    def softmax_pv(blk, g, logits_t):
        probs_t = [[None, None], [None, None]]
        inv = [[None, None], [None, None]]
        for half in range(2):
            for pair in range(2):
                head = g * 4 + pair * 2 + half
                lt = logits_t[half][:, pair * BLOCK:(pair + 1) * BLOCK] + bias_s[slots[blk], head]
                sink = sink_ref[head]
                m = jnp.maximum(jnp.max(lt, axis=0, keepdims=True), sink)
                e = jnp.exp(lt - m)
                inv[half][pair] = 1.0 / (jnp.sum(e, axis=0, keepdims=True) + jnp.exp(sink - m))
                probs_t[half][pair] = e.astype(BF16)
        v_lo_t = jnp.concatenate([vtlo_s[g, blk], vtlo_s[g, blk + 1]], axis=1)
        v_hi_t = jnp.concatenate([vthi_s[g, blk], vthi_s[g, blk + 1]], axis=1)
        out_t = (_dot(v_lo_t, jnp.concatenate(probs_t[0], axis=1))
                 + _dot(v_hi_t, jnp.concatenate(probs_t[1], axis=1)))
        scale = jnp.concatenate(
            [jnp.broadcast_to(jnp.concatenate(inv[0], axis=1), (HEAD_DIM, 2 * BLOCK)),
             jnp.broadcast_to(jnp.concatenate(inv[1], axis=1), (HEAD_DIM, 2 * BLOCK))], axis=0)
        return (out_t * scale).T

    units = [(blk, g) for g in range(SWA_KV_HEADS) for blk in range(nblk)]
    logits_next = qk(*units[0])
    for g in range(SWA_KV_HEADS):
        cols = slice(g * 2 * PAIR, (g + 1) * 2 * PAIR)

        def proj(off):
            return _dot(h, win_ref[:, off + g * 2 * PAIR:off + (g + 1) * 2 * PAIR])

        def conv_taps(_):
            cu = proj(OFF_C) * proj(OFF_U)
            cu_s[SUBLANES:SUBLANES + tm, cols] = cu
            return (convw_ref[0:1, cols] * cu_s[SUBLANES - 2:SUBLANES - 2 + tm, cols]
                    + convw_ref[1:2, cols] * cu_s[SUBLANES - 1:SUBLANES - 1 + tm, cols]
                    + convw_ref[2:3, cols] * cu)

        stages = [conv_taps,
                  lambda conv: proj(OFF_B) * conv,
                  lambda conv: jax.nn.sigmoid(proj(OFF_GC)) * conv]
        attn_blocks = []
        conv = None
        for blk in range(nblk):
            logits_t = logits_next
            nxt = g * nblk + blk + 1
            if nxt < len(units):
                logits_next = qk(*units[nxt])
            if blk < len(stages):
                conv = stages[blk](conv)
            out = softmax_pv(blk, g, logits_t)
            attn_blocks.append(jnp.concatenate([out[0:BLOCK], out[BLOCK:2 * BLOCK]], axis=1))
        for stage in stages[nblk:]:
            conv = stage(conv)
        attn = jnp.concatenate(attn_blocks, axis=0)
        merged_s[:, cols] = (jax.nn.sigmoid(proj(OFF_GA)) * attn + conv).astype(BF16)

    o_ref[...] = x_ref[...] + _dot(merged_s[...], wout_ref[...])

    for j in range(nblk):
        @pl.when(next_misses[j] != 0)
        def _(j=j):
            rebuild(next_slots[j], relm_next_s[j])


def _mix(x, posq, posk, rel_bias_t, sinks, gain, w_in, conv_w, w_out, *, seq_len):
    t = x.shape[0]
    tm = TOKEN_TILE
    nblk = tm // BLOCK
    k_scratch = pltpu.VMEM((SWA_KV_HEADS, BLOCK + tm, PAIR), BF16)
    vt_scratch = pltpu.VMEM((SWA_KV_HEADS, nblk + 1, PAIR, BLOCK), BF16)
    scratch = [
        pltpu.VMEM((tm, Q_WIDTH), BF16),
        k_scratch, k_scratch, vt_scratch, vt_scratch,
        pltpu.VMEM((SUBLANES + tm, D_MODEL), F32),
        pltpu.VMEM((tm, D_MODEL), BF16),
        pltpu.VMEM((nblk + 1, 2 * BLOCK, BLOCK), jnp.int32),
        pltpu.VMEM((nblk, 2 * BLOCK, BLOCK), jnp.int32),
        pltpu.VMEM((nblk + 1, SWA_HEADS, 2 * BLOCK, BLOCK), F32),
        pltpu.VMEM((SWA_HEADS, WINDOW), F32),
    ]
    return pl.pallas_call(
        functools.partial(_mix_kernel, tiles_per_seq=seq_len // tm),
        grid=(t // tm,),
        in_specs=[
            pl.BlockSpec((tm, D_MODEL), lambda i: (i, 0)),
            pl.BlockSpec((1, nblk, BLOCK), lambda i: (i, 0, 0)),
            pl.BlockSpec((1, nblk, 2 * BLOCK), lambda i: (i, 0, 0)),
            pl.BlockSpec((1, nblk, BLOCK), lambda i: (jnp.minimum(i + 1, t // tm - 1), 0, 0)),
            pl.BlockSpec((1, nblk, 2 * BLOCK), lambda i: (jnp.minimum(i + 1, t // tm - 1), 0, 0)),
            _resident((SWA_HEADS, REL_BUCKETS)),
            pl.BlockSpec(memory_space=pltpu.SMEM),
            _resident((1, D_MODEL)),
            _resident((D_MODEL, IN_WIDTH)),
            _resident((CONV_K, D_MODEL)),
            _resident((D_MODEL, D_MODEL)),
        ],
        out_specs=pl.BlockSpec((tm, D_MODEL), lambda i: (i, 0)),
        out_shape=jax.ShapeDtypeStruct((t, D_MODEL), F32),
        scratch_shapes=scratch,
        compiler_params=_params(
            sum(_nbytes(b.shape, b.dtype) for b in scratch) + _nbytes(w_in.shape, F32) + _nbytes(w_out.shape, F32),
            2 * _nbytes((tm, D_MODEL), F32), (tm, D_MODEL), 4),
        name="mix",
    )(x, posq, posk, posq, posk, rel_bias_t, sinks, gain, w_in, conv_w, w_out)


def _mem_proj_kernel(mem_ref, mg_ref, wk_ref, wv_ref, wq_ref, wo_ref, qk_ref, vo_ref, m_s, *, batch):
    @pl.when(pl.program_id(0) == 0)
    def _():
        m_s[...] = _rms(mem_ref[...], mg_ref[...]).astype(BF16)

    k = _dot(m_s[...], wk_ref[...])
    v = _dot(m_s[...], wv_ref[...])
    qk = (_dot_nt(wq_ref[...], k) * (MEM_HEAD_DIM ** -0.5)).astype(BF16)
    vo = _dot(v, wo_ref[...]).astype(BF16)
    for b in range(batch):
        qk_ref[b] = qk[:, b * N_MEM:(b + 1) * N_MEM]
        vo_ref[b] = vo[b * N_MEM:(b + 1) * N_MEM, :]


def _mem_proj(mem, gain, w_kv, w_q, w_o, *, batch):
    dh = MEM_HEAD_DIM
    scratch = [pltpu.VMEM((batch * N_MEM, D_MODEL), BF16)]
    return pl.pallas_call(
        functools.partial(_mem_proj_kernel, batch=batch),
        grid=(MEM_HEADS,),
        in_specs=[
            _resident((batch * N_MEM, D_MODEL)),
            _resident((1, D_MODEL)),
            pl.BlockSpec((D_MODEL, dh), lambda i: (0, i)),
            pl.BlockSpec((D_MODEL, dh), lambda i: (0, MEM_HEADS + i)),
            pl.BlockSpec((D_MODEL, dh), lambda i: (0, i)),
            pl.BlockSpec((dh, D_MODEL), lambda i: (i, 0)),
        ],
        out_specs=[pl.BlockSpec((batch, D_MODEL, N_MEM), lambda i: (0, 0, i)),
                   pl.BlockSpec((batch, N_MEM, D_MODEL), lambda i: (0, i, 0))],
        out_shape=[jax.ShapeDtypeStruct((batch, D_MODEL, MEM_HEADS * N_MEM), BF16),
                   jax.ShapeDtypeStruct((batch, MEM_HEADS * N_MEM, D_MODEL), BF16)],
        scratch_shapes=scratch,
        compiler_params=_params(
            _nbytes((batch * N_MEM, D_MODEL), F32) + sum(_nbytes(b.shape, b.dtype) for b in scratch),
            4 * _nbytes((D_MODEL, dh), F32) + 2 * _nbytes((batch, D_MODEL, N_MEM), BF16),
            (batch * N_MEM, D_MODEL), 10),
        name="mem_proj",
    )(mem, gain, w_kv, w_kv, w_q, w_o)


XATTN_ROW_SPLITS = 4


def _xattn_kernel(x_ref, g_ref, qk_ref, vo_ref, o_ref):
    rows_per_split = x_ref.shape[0] // XATTN_ROW_SPLITS

    def probs(lg):
        out = []
        for hd in range(MEM_HEADS):
            seg = lg[:, hd * N_MEM:(hd + 1) * N_MEM]
            e = jnp.exp(seg - jnp.max(seg, axis=1, keepdims=True))
            out.append((e / jnp.sum(e, axis=1, keepdims=True)).astype(BF16))
        return jnp.concatenate(out, axis=1)

    def logits(r):
        rows = slice(r * rows_per_split, (r + 1) * rows_per_split)
        return rows, _dot(_rms(x_ref[rows, :], g_ref[...]), qk_ref[0])

    pending = logits(0)
    for r in range(XATTN_ROW_SPLITS):
        rows, lg = pending
        if r + 1 < XATTN_ROW_SPLITS:
            pending = logits(r + 1)
        o_ref[rows, :] = x_ref[rows, :] + _dot(probs(lg), vo_ref[0])


def _xattn(x, gain, qk, vo, *, seq_len):
    t = x.shape[0]
    tm = XATTN_TOKEN_TILE
    tiles_per_seq = seq_len // tm
    return pl.pallas_call(
        _xattn_kernel,
        grid=(t // tm,),
        in_specs=[
            pl.BlockSpec((tm, D_MODEL), lambda i: (i, 0)),
            _resident((1, D_MODEL)),
            pl.BlockSpec((1, D_MODEL, D_MODEL), lambda i: (i // tiles_per_seq, 0, 0)),
            pl.BlockSpec((1, D_MODEL, D_MODEL), lambda i: (i // tiles_per_seq, 0, 0)),
        ],
        out_specs=pl.BlockSpec((tm, D_MODEL), lambda i: (i, 0)),
        out_shape=jax.ShapeDtypeStruct((t, D_MODEL), F32),
        compiler_params=_params(
            0, 2 * _nbytes((tm, D_MODEL), F32) + 2 * _nbytes((D_MODEL, D_MODEL), BF16), (tm, D_MODEL), 6),
        name="xattn",
    )(x, gain, qk, vo)


def kernel(x, mem, positions, rel_bias, ffn1_norm, ffn1_w_gu, ffn1_w_down, mix_norm, w_in, sinks, conv_w, w_out, xattn_norm, mem_norm, xattn_wq, xattn_wkv, xattn_wo, ffn2_norm, ffn2_w_gu, ffn2_w_down, final_norm):
    batch, seq_len, _ = x.shape
    depth = w_in.shape[0]
    tokens = batch * seq_len
    nb = seq_len // BLOCK

    xt = x.reshape(tokens, D_MODEL)
    posq = positions.reshape(tokens // TOKEN_TILE, TOKEN_TILE // BLOCK, BLOCK)
    pos_blocks = positions.reshape(batch, nb, BLOCK)
    prev = jnp.concatenate([jnp.full((batch, 1, BLOCK), POS_PAD, positions.dtype), pos_blocks[:, :-1]], axis=1)
    posk = jnp.concatenate([prev, pos_blocks], axis=2).reshape(tokens // TOKEN_TILE, TOKEN_TILE // BLOCK, 2 * BLOCK)
    rel_bias_t = rel_bias.T
    final_gain = final_norm.reshape(1, D_MODEL)

    for l in range(depth):
        xt = _ffn(xt, ffn1_norm[l].reshape(1, D_MODEL), ffn1_w_gu[l], ffn1_w_down[l],
                  final_gain, final_norm=False)
        qk, vo = _mem_proj(mem.reshape(batch * N_MEM, D_MODEL), mem_norm[l].reshape(1, D_MODEL),
                           xattn_wkv[l], xattn_wq[l], xattn_wo[l], batch=batch)
        xt = _mix(xt, posq, posk, rel_bias_t, sinks[l], mix_norm[l].reshape(1, D_MODEL),
                  w_in[l], conv_w[l], w_out[l], seq_len=seq_len)
        xt = _xattn(xt, xattn_norm[l].reshape(1, D_MODEL), qk, vo, seq_len=seq_len)
        xt = _ffn(xt, ffn2_norm[l].reshape(1, D_MODEL), ffn2_w_gu[l], ffn2_w_down[l],
                  final_gain, final_norm=(l == depth - 1))
    return xt.reshape(batch, seq_len, D_MODEL)
```

```python
import functools
import math

import jax
import jax.numpy as jnp
from jax import lax
from jax.experimental import pallas as pl
from jax.experimental.pallas import tpu as pltpu

D_MODEL = 1024
D_FF = 2816
N_MEM = 256
MEM_HEADS = 4
MEM_HEAD_DIM = D_MODEL // MEM_HEADS
HEAD_DIM = 64
SWA_HEADS = 16
SWA_KV_HEADS = 4
WINDOW = 128
BLOCK = 128
REL_BUCKETS = 32
REL_MAX_DIST = 128
CONV_K = 3
EPS = 1e-6
NEG_INF = -1e30
POS_PAD = 1 << 30

Q_WIDTH = SWA_HEADS * HEAD_DIM
KV_WIDTH = SWA_KV_HEADS * HEAD_DIM
OFF_Q = 0
OFF_K = OFF_Q + Q_WIDTH
OFF_V = OFF_K + KV_WIDTH
OFF_C = OFF_V + KV_WIDTH
OFF_B = OFF_C + D_MODEL
OFF_U = OFF_B + D_MODEL
OFF_GA = OFF_U + D_MODEL
OFF_GC = OFF_GA + D_MODEL
IN_WIDTH = OFF_GC + D_MODEL

SUBLANES = 8
MXU_COLS = 256
V7X_VMEM_BYTES = 64 * 1024 * 1024
VMEM_REQUEST_CAP = V7X_VMEM_BYTES - 4 * 1024 * 1024

TOKEN_TILE = 512
XATTN_TOKEN_TILE = 1024
FFN_TOKEN_TILE = 512
FF_CHUNK = MXU_COLS
PAIR = 2 * HEAD_DIM

BF16 = jnp.bfloat16
F32 = jnp.float32


def _rms(x, g):
    return x * lax.rsqrt(jnp.mean(x * x, axis=-1, keepdims=True) + EPS) * g


def _dot(a, b):
    return jnp.dot(a.astype(BF16), b.astype(BF16), preferred_element_type=F32)


def _dot_nt(a, b):
    return lax.dot_general(a.astype(BF16), b.astype(BF16), (((1,), (1,)), ((), ())),
                           preferred_element_type=F32)


def _resident(shape):
    return pl.BlockSpec(shape, lambda i: (0,) * len(shape), pipeline_mode=pl.Buffered(1))


def _nbytes(shape, dtype):
    return math.prod(shape) * jnp.dtype(dtype).itemsize


def _params(resident_bytes, per_step_bytes, tile_shape, tile_temporaries):
    request = resident_bytes + 2 * per_step_bytes + tile_temporaries * _nbytes(tile_shape, F32)
    return pltpu.CompilerParams(dimension_semantics=("arbitrary",),
                                vmem_limit_bytes=min(VMEM_REQUEST_CAP, request))


FF_CHUNKS = D_FF // FF_CHUNK
OUT_CHUNKS = D_MODEL // MXU_COLS


def _ffn_kernel(x_ref, xnext_ref, g_ref, wgu_hbm, wd_hbm, fg_ref, o_ref, wgu_v, wd_v, act_ref, h_s, act0_s, sems,
                *, final_norm):
    def gate_copy(j):
        return pltpu.make_async_copy(wgu_hbm.at[:, pl.ds(j * FF_CHUNK, FF_CHUNK)], wgu_v.at[j], sems.at[j])

    def up_copy(j):
        return pltpu.make_async_copy(wgu_hbm.at[:, pl.ds(D_FF + j * FF_CHUNK, FF_CHUNK)],
                                     wgu_v.at[FF_CHUNKS + j], sems.at[FF_CHUNKS + j])

    def down_copy(c):
        return pltpu.make_async_copy(wd_hbm.at[:, pl.ds(c * MXU_COLS, MXU_COLS)], wd_v.at[c],
                                     sems.at[2 * FF_CHUNKS + c])

    step = pl.program_id(0)
    cur, nxt = step % 2, (step + 1) % 2

    def chunk_act(h, j):
        gate = _dot(h, wgu_v[j])
        up = _dot(h, wgu_v[FF_CHUNKS + j])
        return (gate * jax.nn.sigmoid(gate) * up).astype(BF16)

    def body(first):
        if first:
            h = _rms(x_ref[...], g_ref[...]).astype(BF16)
            gate_copy(0).wait()
            up_copy(0).wait()
            act_ref[:, 0:FF_CHUNK] = chunk_act(h, 0)
        else:
            h = h_s[cur]
            act_ref[:, 0:FF_CHUNK] = act0_s[cur]
        for j in range(1, FF_CHUNKS):
            if first:
                gate_copy(j).wait()
                up_copy(j).wait()
            act_ref[:, j * FF_CHUNK:(j + 1) * FF_CHUNK] = chunk_act(h, j)
        if first:
            for c in range(OUT_CHUNKS):
                down_copy(c).wait()
        y = x_ref[...] + 0.5 * jnp.concatenate([_dot(act_ref[...], wd_v[c]) for c in range(OUT_CHUNKS)], axis=1)
        h_next = _rms(xnext_ref[...], g_ref[...]).astype(BF16)
        h_s[nxt] = h_next
        act0_s[nxt] = chunk_act(h_next, 0)
        if final_norm:
            y = _rms(y, fg_ref[...])
        o_ref[...] = y

    first_step = step == 0

    @pl.when(first_step)
    def _():
        for j in range(FF_CHUNKS):
            gate_copy(j).start(priority=1)
            up_copy(j).start(priority=1)
        for c in range(OUT_CHUNKS):
            down_copy(c).start(priority=1)
        body(True)

    @pl.when(jnp.logical_not(first_step))
    def _():
        body(False)


def _ffn(x, gain, w_gu, w_down, final_gain, *, final_norm):
    t = x.shape[0]
    tm = FFN_TOKEN_TILE
    scratch = [
        pltpu.VMEM((2 * FF_CHUNKS, D_MODEL, FF_CHUNK), F32),
        pltpu.VMEM((OUT_CHUNKS, D_FF, MXU_COLS), F32),
        pltpu.VMEM((tm, D_FF), BF16),
        pltpu.VMEM((2, tm, D_MODEL), BF16),
        pltpu.VMEM((2, tm, FF_CHUNK), BF16),
    ]
    return pl.pallas_call(
        functools.partial(_ffn_kernel, final_norm=final_norm),
        grid=(t // tm,),
        in_specs=[
            pl.BlockSpec((tm, D_MODEL), lambda i: (i, 0)),
            pl.BlockSpec((tm, D_MODEL), lambda i: (jnp.minimum(i + 1, t // tm - 1), 0)),
            _resident((1, D_MODEL)),
            pl.BlockSpec(memory_space=pl.ANY),
            pl.BlockSpec(memory_space=pl.ANY),
            _resident((1, D_MODEL)),
        ],
        out_specs=pl.BlockSpec((tm, D_MODEL), lambda i: (i, 0)),
        out_shape=jax.ShapeDtypeStruct((t, D_MODEL), F32),
        scratch_shapes=scratch + [pltpu.SemaphoreType.DMA((2 * FF_CHUNKS + OUT_CHUNKS,))],
        compiler_params=_params(sum(_nbytes(b.shape, b.dtype) for b in scratch),
                                3 * _nbytes((tm, D_MODEL), F32), (tm, D_MODEL), 2),
        name="ffn_final" if final_norm else "ffn",
    )(x, x, gain, w_gu, w_down, final_gain)


def _bias_lut(tab_ref):
    n = lax.broadcasted_iota(jnp.int32, (SWA_HEADS, WINDOW), 1).astype(F32)
    max_exact = REL_BUCKETS // 2
    nf = jnp.maximum(n, 1.0)
    large = max_exact + jnp.floor(jnp.log(nf / max_exact) / math.log(REL_MAX_DIST / max_exact)
                                  * (REL_BUCKETS - max_exact))
    large = jnp.minimum(large, REL_BUCKETS - 1.0)
    bucket = jnp.where(n < max_exact, n, large)
    lut = jnp.broadcast_to(tab_ref[:, 0:1], (SWA_HEADS, WINDOW))
    for b in range(1, REL_BUCKETS):
        lut = jnp.where(bucket >= b, tab_ref[:, b:b + 1], lut)
    return lut


def _key_head_slabs(t):
    lane = lax.broadcasted_iota(jnp.int32, (t.shape[0], PAIR), 1)
    low_half = lane < HEAD_DIM
    los, his = [], []
    for p in range(SWA_KV_HEADS // 2):
        pair = t[:, p * PAIR:(p + 1) * PAIR]
        swapped = pltpu.roll(pair, HEAD_DIM, axis=1)
        zero = jnp.zeros_like(pair)
        los += [jnp.where(low_half, pair, zero), jnp.where(low_half, swapped, zero)]
        his += [jnp.where(low_half, zero, swapped), jnp.where(low_half, zero, pair)]
    return [a.astype(BF16) for a in los], [a.astype(BF16) for a in his]


def _value_head_slabs_t(t):
    los, his = [], []
    for p in range(SWA_KV_HEADS // 2):
        pair_t = t[:, p * PAIR:(p + 1) * PAIR].T
        zero = jnp.zeros((HEAD_DIM, t.shape[0]), F32)
        for head_t in (pair_t[0:HEAD_DIM], pair_t[HEAD_DIM:PAIR]):
            los.append(jnp.concatenate([head_t, zero], axis=0))
            his.append(jnp.concatenate([zero, head_t], axis=0))
    return [a.astype(BF16) for a in los], [a.astype(BF16) for a in his]


def _mix_kernel(x_ref, posq_ref, posk_ref, posq_next_ref, posk_next_ref, tab_ref, sink_ref, g_ref, win_ref,
                convw_ref, wout_ref, o_ref,
                q_s, klo_s, khi_s, vtlo_s, vthi_s, cu_s, merged_s, relm_s, relm_next_s, bias_s, lut_s, *, tiles_per_seq):
    tm = x_ref.shape[0]
    nblk = tm // BLOCK
    step = pl.program_id(0)
    first = (step % tiles_per_seq) == 0
    next_first = ((step + 1) % tiles_per_seq) == 0

    @pl.when(first)
    def _():
        for s in (klo_s, khi_s):
            s[:, 0:BLOCK, :] = jnp.zeros((SWA_KV_HEADS, BLOCK, PAIR), BF16)
        for s in (vtlo_s, vthi_s):
            s[:, 0] = jnp.zeros((SWA_KV_HEADS, PAIR, BLOCK), BF16)
        cu_s[0:SUBLANES, :] = jnp.zeros((SUBLANES, D_MODEL), F32)

    @pl.when(jnp.logical_not(first))
    def _():
        for s in (klo_s, khi_s):
            s[:, 0:BLOCK, :] = s[:, tm:tm + BLOCK, :]
        for s in (vtlo_s, vthi_s):
            s[:, 0] = s[:, nblk]
        cu_s[0:SUBLANES, :] = cu_s[tm:tm + SUBLANES, :]

    def masked_rel(pq_ref, pk_ref, j):
        pos_k = jnp.concatenate(
            [jnp.broadcast_to(pk_ref[0, j:j + 1, c * BLOCK:(c + 1) * BLOCK], (BLOCK, BLOCK)).T
             for c in range(2)], axis=0)
        rel = pq_ref[0, j:j + 1, :] - pos_k
        return jnp.where((rel >= 0) & (rel < WINDOW), rel, -1)

    def block_slots(sequence_start):
        return [jnp.where(sequence_start, nblk, 0)] + list(range(1, nblk))

    def rebuild(slot, relm):
        relm_s[slot] = relm
        shown = relm >= 0
        idx = jnp.maximum(relm, 0)
        for hd in range(SWA_HEADS):
            lut = jnp.broadcast_to(lut_s[hd:hd + 1, :], (2 * BLOCK, WINDOW))
            bias_s[slot, hd] = jnp.where(shown, jnp.take_along_axis(lut, idx, axis=1), NEG_INF)

    slots = block_slots(first)
    next_slots = block_slots(next_first)

    @pl.when(step == 0)
    def _():
        relm_s[...] = jnp.full(relm_s.shape, -2, jnp.int32)
        lut_s[...] = _bias_lut(tab_ref)
        for j in range(nblk):
            rebuild(slots[j], masked_rel(posq_ref, posk_ref, j))

    next_misses = []
    for j in range(nblk):
        relm = masked_rel(posq_next_ref, posk_next_ref, j)
        relm_next_s[j] = relm
        next_misses.append(jnp.sum((relm != relm_s[next_slots[j]]).astype(jnp.int32)))

    h = _rms(x_ref[...], g_ref[...]).astype(BF16)

    k_lo, k_hi = _key_head_slabs(_dot(h, win_ref[:, OFF_K:OFF_K + KV_WIDTH]))
    vt_lo, vt_hi = _value_head_slabs_t(_dot(h, win_ref[:, OFF_V:OFF_V + KV_WIDTH]))
    for g in range(SWA_KV_HEADS):
        klo_s[g, BLOCK:BLOCK + tm, :] = k_lo[g]
        khi_s[g, BLOCK:BLOCK + tm, :] = k_hi[g]
        for j in range(nblk):
            vtlo_s[g, j + 1] = vt_lo[g][:, j * BLOCK:(j + 1) * BLOCK]
            vthi_s[g, j + 1] = vt_hi[g][:, j * BLOCK:(j + 1) * BLOCK]
    q_s[...] = (_dot(h, win_ref[:, OFF_Q:OFF_Q + Q_WIDTH]) * (HEAD_DIM ** -0.5)).astype(BF16)

    def qk(blk, g):
        rows = slice(blk * BLOCK, (blk + 1) * BLOCK)
        keys = slice(blk * BLOCK, (blk + 2) * BLOCK)
        c0 = g * 2 * PAIR
        q2 = jnp.concatenate([q_s[rows, c0:c0 + PAIR], q_s[rows, c0 + PAIR:c0 + 2 * PAIR]], axis=0)
        return _dot_nt(klo_s[g, keys, :], q2), _dot_nt(khi_s[g, keys, :], q2)

    def softmax_pv(blk, g, logits_t):
        probs_t = [[None, None], [None, None]]
        inv = [[None, None], [None, None]]
        for half in range(2):
            for pair in range(2):
                head = g * 4 + pair * 2 + half
                lt = logits_t[half][:, pair * BLOCK:(pair + 1) * BLOCK] + bias_s[slots[blk], head]
                sink = sink_ref[head]
                m = jnp.maximum(jnp.max(lt, axis=0, keepdims=True), sink)
                e = jnp.exp(lt - m)
                inv[half][pair] = 1.0 / (jnp.sum(e, axis=0, keepdims=True) + jnp.exp(sink - m))
                probs_t[half][pair] = e.astype(BF16)
        v_lo_t = jnp.concatenate([vtlo_s[g, blk], vtlo_s[g, blk + 1]], axis=1)
        v_hi_t = jnp.concatenate([vthi_s[g, blk], vthi_s[g, blk + 1]], axis=1)
        out_t = (_dot(v_lo_t, jnp.concatenate(probs_t[0], axis=1))
                 + _dot(v_hi_t, jnp.concatenate(probs_t[1], axis=1)))
        scale = jnp.concatenate(
            [jnp.broadcast_to(jnp.concatenate(inv[0], axis=1), (HEAD_DIM, 2 * BLOCK)),
             jnp.broadcast_to(jnp.concatenate(inv[1], axis=1), (HEAD_DIM, 2 * BLOCK))], axis=0)
        return (out_t * scale).T

    units = [(blk, g) for g in range(SWA_KV_HEADS) for blk in range(nblk)]
    logits_next = qk(*units[0])
    for g in range(SWA_KV_HEADS):
        cols = slice(g * 2 * PAIR, (g + 1) * 2 * PAIR)

        def proj(off):
            return _dot(h, win_ref[:, off + g * 2 * PAIR:off + (g + 1) * 2 * PAIR])

        def conv_taps(_):
            cu = proj(OFF_C) * proj(OFF_U)
            cu_s[SUBLANES:SUBLANES + tm, cols] = cu
            return (convw_ref[0:1, cols] * cu_s[SUBLANES - 2:SUBLANES - 2 + tm, cols]
                    + convw_ref[1:2, cols] * cu_s[SUBLANES - 1:SUBLANES - 1 + tm, cols]
                    + convw_ref[2:3, cols] * cu)

        stages = [conv_taps,
                  lambda conv: proj(OFF_B) * conv,
                  lambda conv: jax.nn.sigmoid(proj(OFF_GC)) * conv]
        attn_blocks = []
        conv = None
        for blk in range(nblk):
            logits_t = logits_next
            nxt = g * nblk + blk + 1
            if nxt < len(units):
                logits_next = qk(*units[nxt])
            if blk < len(stages):
                conv = stages[blk](conv)
            out = softmax_pv(blk, g, logits_t)
            attn_blocks.append(jnp.concatenate([out[0:BLOCK], out[BLOCK:2 * BLOCK]], axis=1))
        for stage in stages[nblk:]:
            conv = stage(conv)
        attn = jnp.concatenate(attn_blocks, axis=0)
        merged_s[:, cols] = (jax.nn.sigmoid(proj(OFF_GA)) * attn + conv).astype(BF16)

    o_ref[...] = x_ref[...] + _dot(merged_s[...], wout_ref[...])

    for j in range(nblk):
        @pl.when(next_misses[j] != 0)
        def _(j=j):
            rebuild(next_slots[j], relm_next_s[j])


def _mix(x, posq, posk, rel_bias_t, sinks, gain, w_in, conv_w, w_out, *, seq_len):
    t = x.shape[0]
    tm = TOKEN_TILE
    nblk = tm // BLOCK
    k_scratch = pltpu.VMEM((SWA_KV_HEADS, BLOCK + tm, PAIR), BF16)
    vt_scratch = pltpu.VMEM((SWA_KV_HEADS, nblk + 1, PAIR, BLOCK), BF16)
    scratch = [
        pltpu.VMEM((tm, Q_WIDTH), BF16),
        k_scratch, k_scratch, vt_scratch, vt_scratch,
        pltpu.VMEM((SUBLANES + tm, D_MODEL), F32),
        pltpu.VMEM((tm, D_MODEL), BF16),
        pltpu.VMEM((nblk + 1, 2 * BLOCK, BLOCK), jnp.int32),
        pltpu.VMEM((nblk, 2 * BLOCK, BLOCK), jnp.int32),
        pltpu.VMEM((nblk + 1, SWA_HEADS, 2 * BLOCK, BLOCK), F32),
        pltpu.VMEM((SWA_HEADS, WINDOW), F32),
    ]
    return pl.pallas_call(
        functools.partial(_mix_kernel, tiles_per_seq=seq_len // tm),
        grid=(t // tm,),
        in_specs=[
            pl.BlockSpec((tm, D_MODEL), lambda i: (i, 0)),
            pl.BlockSpec((1, nblk, BLOCK), lambda i: (i, 0, 0)),
            pl.BlockSpec((1, nblk, 2 * BLOCK), lambda i: (i, 0, 0)),
            pl.BlockSpec((1, nblk, BLOCK), lambda i: (jnp.minimum(i + 1, t // tm - 1), 0, 0)),
            pl.BlockSpec((1, nblk, 2 * BLOCK), lambda i: (jnp.minimum(i + 1, t // tm - 1), 0, 0)),
            _resident((SWA_HEADS, REL_BUCKETS)),
            pl.BlockSpec(memory_space=pltpu.SMEM),
            _resident((1, D_MODEL)),
            _resident((D_MODEL, IN_WIDTH)),
            _resident((CONV_K, D_MODEL)),
            _resident((D_MODEL, D_MODEL)),
        ],
        out_specs=pl.BlockSpec((tm, D_MODEL), lambda i: (i, 0)),
        out_shape=jax.ShapeDtypeStruct((t, D_MODEL), F32),
        scratch_shapes=scratch,
        compiler_params=_params(
            sum(_nbytes(b.shape, b.dtype) for b in scratch) + _nbytes(w_in.shape, F32) + _nbytes(w_out.shape, F32),
            2 * _nbytes((tm, D_MODEL), F32), (tm, D_MODEL), 4),
        name="mix",
    )(x, posq, posk, posq, posk, rel_bias_t, sinks, gain, w_in, conv_w, w_out)


def _mem_proj_kernel(mem_ref, mg_ref, wk_ref, wv_ref, wq_ref, wo_ref, qk_ref, vo_ref, m_s, *, batch):
    @pl.when(pl.program_id(0) == 0)
    def _():
        m_s[...] = _rms(mem_ref[...], mg_ref[...]).astype(BF16)

    k = _dot(m_s[...], wk_ref[...])
    v = _dot(m_s[...], wv_ref[...])
    qk = (_dot_nt(wq_ref[...], k) * (MEM_HEAD_DIM ** -0.5)).astype(BF16)
    vo = _dot(v, wo_ref[...]).astype(BF16)
    for b in range(batch):
        qk_ref[b] = qk[:, b * N_MEM:(b + 1) * N_MEM]
        vo_ref[b] = vo[b * N_MEM:(b + 1) * N_MEM, :]


def _mem_proj(mem, gain, w_kv, w_q, w_o, *, batch):
    dh = MEM_HEAD_DIM
    scratch = [pltpu.VMEM((batch * N_MEM, D_MODEL), BF16)]
    return pl.pallas_call(
        functools.partial(_mem_proj_kernel, batch=batch),
        grid=(MEM_HEADS,),
        in_specs=[
            _resident((batch * N_MEM, D_MODEL)),
            _resident((1, D_MODEL)),
            pl.BlockSpec((D_MODEL, dh), lambda i: (0, i)),
            pl.BlockSpec((D_MODEL, dh), lambda i: (0, MEM_HEADS + i)),
            pl.BlockSpec((D_MODEL, dh), lambda i: (0, i)),
            pl.BlockSpec((dh, D_MODEL), lambda i: (i, 0)),
        ],
        out_specs=[pl.BlockSpec((batch, D_MODEL, N_MEM), lambda i: (0, 0, i)),
                   pl.BlockSpec((batch, N_MEM, D_MODEL), lambda i: (0, i, 0))],
        out_shape=[jax.ShapeDtypeStruct((batch, D_MODEL, MEM_HEADS * N_MEM), BF16),
                   jax.ShapeDtypeStruct((batch, MEM_HEADS * N_MEM, D_MODEL), BF16)],
        scratch_shapes=scratch,
        compiler_params=_params(
            _nbytes((batch * N_MEM, D_MODEL), F32) + sum(_nbytes(b.shape, b.dtype) for b in scratch),
            4 * _nbytes((D_MODEL, dh), F32) + 2 * _nbytes((batch, D_MODEL, N_MEM), BF16),
            (batch * N_MEM, D_MODEL), 10),
        name="mem_proj",
    )(mem, gain, w_kv, w_kv, w_q, w_o)


XATTN_ROW_SPLITS = 4


def _xattn_kernel(x_ref, g_ref, qk_ref, vo_ref, o_ref):
    rows_per_split = x_ref.shape[0] // XATTN_ROW_SPLITS

    def probs(lg):
        out = []
        for hd in range(MEM_HEADS):
            seg = lg[:, hd * N_MEM:(hd + 1) * N_MEM]
            e = jnp.exp(seg - jnp.max(seg, axis=1, keepdims=True))
            out.append((e / jnp.sum(e, axis=1, keepdims=True)).astype(BF16))
        return jnp.concatenate(out, axis=1)

    def logits(r):
        rows = slice(r * rows_per_split, (r + 1) * rows_per_split)
        return rows, _dot(_rms(x_ref[rows, :], g_ref[...]), qk_ref[0])

    pending = logits(0)
    for r in range(XATTN_ROW_SPLITS):
        rows, lg = pending
        if r + 1 < XATTN_ROW_SPLITS:
            pending = logits(r + 1)
        o_ref[rows, :] = x_ref[rows, :] + _dot(probs(lg), vo_ref[0])


def _xattn(x, gain, qk, vo, *, seq_len):
    t = x.shape[0]
    tm = XATTN_TOKEN_TILE
    tiles_per_seq = seq_len // tm
    return pl.pallas_call(
        _xattn_kernel,
        grid=(t // tm,),
        in_specs=[
            pl.BlockSpec((tm, D_MODEL), lambda i: (i, 0)),
            _resident((1, D_MODEL)),
            pl.BlockSpec((1, D_MODEL, D_MODEL), lambda i: (i // tiles_per_seq, 0, 0)),
            pl.BlockSpec((1, D_MODEL, D_MODEL), lambda i: (i // tiles_per_seq, 0, 0)),
        ],
        out_specs=pl.BlockSpec((tm, D_MODEL), lambda i: (i, 0)),
        out_shape=jax.ShapeDtypeStruct((t, D_MODEL), F32),
        compiler_params=_params(
            0, 2 * _nbytes((tm, D_MODEL), F32) + 2 * _nbytes((D_MODEL, D_MODEL), BF16), (tm, D_MODEL), 6),
        name="xattn",
    )(x, gain, qk, vo)


def kernel(x, mem, positions, rel_bias, ffn1_norm, ffn1_w_gu, ffn1_w_down, mix_norm, w_in, sinks, conv_w, w_out, xattn_norm, mem_norm, xattn_wq, xattn_wkv, xattn_wo, ffn2_norm, ffn2_w_gu, ffn2_w_down, final_norm):
    batch, seq_len, _ = x.shape
    depth = w_in.shape[0]
    tokens = batch * seq_len
    nb = seq_len // BLOCK

    xt = x.reshape(tokens, D_MODEL)
    posq = positions.reshape(tokens // TOKEN_TILE, TOKEN_TILE // BLOCK, BLOCK)
    pos_blocks = positions.reshape(batch, nb, BLOCK)
    prev = jnp.concatenate([jnp.full((batch, 1, BLOCK), POS_PAD, positions.dtype), pos_blocks[:, :-1]], axis=1)
    posk = jnp.concatenate([prev, pos_blocks], axis=2).reshape(tokens // TOKEN_TILE, TOKEN_TILE // BLOCK, 2 * BLOCK)
    rel_bias_t = rel_bias.T
    final_gain = final_norm.reshape(1, D_MODEL)

    for l in range(depth):
        xt = _ffn(xt, ffn1_norm[l].reshape(1, D_MODEL), ffn1_w_gu[l], ffn1_w_down[l],
                  final_gain, final_norm=False)
        qk, vo = _mem_proj(mem.reshape(batch * N_MEM, D_MODEL), mem_norm[l].reshape(1, D_MODEL),
                           xattn_wkv[l], xattn_wq[l], xattn_wo[l], batch=batch)
        xt = _mix(xt, posq, posk, rel_bias_t, sinks[l], mix_norm[l].reshape(1, D_MODEL),
                  w_in[l], conv_w[l], w_out[l], seq_len=seq_len)
        xt = _xattn(xt, xattn_norm[l].reshape(1, D_MODEL), qk, vo, seq_len=seq_len)
        xt = _ffn(xt, ffn2_norm[l].reshape(1, D_MODEL), ffn2_w_gu[l], ffn2_w_down[l],
                  final_gain, final_norm=(l == depth - 1))
    return xt.reshape(batch, seq_len, D_MODEL)
```
